```python
import jax, jax.numpy as jnp
from jax import lax
import numpy as np

D_MODEL = 1024
BATCH = 4
SEQ = 8192
DEPTH = 2

HEAD_DIM = 64
Q_BLOCK = 128
RMS_EPS = 1e-6
ROPE_THETA = 10000.0
FOX_HEADS = 8
MLA_HEADS = 8
MLA_Q_RANK = 384
MLA_KV_RANK = 256
MLA_NOPE_DIM = 64
MLA_ROPE_DIM = 32
MLA_V_DIM = 64
SWA_HEADS = 8
SWA_KV_HEADS = 2
WINDOW = 128
CONV_CHANNELS = 512
CONV_WIDTH = 3
D_FF = 3584
N_EXPERTS = 8
TOP_K = 2

SPLIT_EVEN = (FOX_HEADS * HEAD_DIM, FOX_HEADS * HEAD_DIM, FOX_HEADS * HEAD_DIM, FOX_HEADS,
              MLA_Q_RANK, MLA_KV_RANK, MLA_ROPE_DIM)
SPLIT_ODD = (SWA_HEADS * HEAD_DIM, SWA_KV_HEADS * HEAD_DIM, SWA_KV_HEADS * HEAD_DIM,
             CONV_CHANNELS, CONV_CHANNELS, CONV_CHANNELS)
IN_EVEN = 3 * FOX_HEADS * HEAD_DIM + FOX_HEADS + MLA_Q_RANK + MLA_KV_RANK + MLA_ROPE_DIM
IN_ODD = SWA_HEADS * HEAD_DIM + 2 * SWA_KV_HEADS * HEAD_DIM + 3 * CONV_CHANNELS
MIX_EVEN = FOX_HEADS * HEAD_DIM + MLA_HEADS * MLA_V_DIM
MIX_ODD = SWA_HEADS * HEAD_DIM + CONV_CHANNELS

kernel_name = 'hybrid_fox_mla_swa_conv_moe'


def rms_norm(x, g):
    xf = x.astype(jnp.float32)
    y = xf * lax.rsqrt(jnp.mean(xf * xf, axis=-1, keepdims=True) + RMS_EPS)
    return (y * g.astype(jnp.float32)).astype(x.dtype)


def split_cols(x, sizes):
    out, start = [], 0
    for size in sizes:
        out.append(x[..., start:start + size])
        start += size
    return out


def rope_tables(positions, dim):
    inv_freq = ROPE_THETA ** (-jnp.arange(0, dim, 2, dtype=jnp.float32) / dim)
    ang = positions.astype(jnp.float32)[..., None] * inv_freq
    return jnp.cos(ang), jnp.sin(ang)


def apply_rope(x, cos, sin):
    x1, x2 = jnp.split(x, 2, axis=-1)
    c = cos[:, :, None, :].astype(x.dtype)
    s = sin[:, :, None, :].astype(x.dtype)
    return jnp.concatenate([x1 * c - x2 * s, x2 * c + x1 * s], axis=-1)


def causal_block_attention(q, k, v, scale, decay_cum=None):
    B, S, H, Dk = q.shape
    nb = S // Q_BLOCK
    key_pos = jnp.arange(S)
    q_blocks = jnp.moveaxis(q.reshape(B, nb, Q_BLOCK, H, Dk), 1, 0)
    if decay_cum is not None:
        c_keys = jnp.swapaxes(decay_cum, 1, 2)
        c_blocks = jnp.moveaxis(decay_cum.reshape(B, nb, Q_BLOCK, H), 1, 0)
        xs = (jnp.arange(nb), q_blocks, c_blocks)
    else:
        xs = (jnp.arange(nb), q_blocks)

    def attend(blk):
        n, q_blk = blk[0], blk[1]
        s = jnp.einsum('bqhd,bkhd->bhqk', q_blk, k, preferred_element_type=jnp.float32) * scale
        if decay_cum is not None:
            s = s + jnp.swapaxes(blk[2], 1, 2)[..., :, None] - c_keys[:, :, None, :]
        q_pos = n * Q_BLOCK + jnp.arange(Q_BLOCK)
        s = jnp.where(key_pos[None, :] <= q_pos[:, None], s, -jnp.inf)
        p = jax.nn.softmax(s, axis=-1)
        return jnp.einsum('bhqk,bkhd->bqhd', p.astype(v.dtype), v)

    out = lax.map(attend, xs)
    return jnp.moveaxis(out, 0, 1).reshape(B, S, H, v.shape[-1])


def sliding_window_sink_attention(q, k, v, sinks, scale):
    B, S, Hq, D = q.shape
    Hkv = k.shape[2]
    G = Hq // Hkv
    nb = S // Q_BLOCK
    qb = q.reshape(B, nb, Q_BLOCK, Hkv, G, D)
    kb = k.reshape(B, nb, Q_BLOCK, Hkv, D)
    vb = v.reshape(B, nb, Q_BLOCK, Hkv, D)
    pad = ((0, 0), (1, 0), (0, 0), (0, 0), (0, 0))
    k_band = jnp.concatenate([jnp.pad(kb, pad)[:, :-1], kb], axis=2)
    v_band = jnp.concatenate([jnp.pad(vb, pad)[:, :-1], vb], axis=2)
    s = jnp.einsum('bnqhgd,bnkhd->bnhgqk', qb, k_band, preferred_element_type=jnp.float32) * scale
    kidx = jnp.arange(2 * Q_BLOCK)
    rel = (Q_BLOCK + jnp.arange(Q_BLOCK))[:, None] - kidx[None, :]
    in_window = (rel >= 0) & (rel < WINDOW)
    real_key = (jnp.arange(nb)[:, None] > 0) | (kidx[None, :] >= Q_BLOCK)
    mask = in_window[None, :, :] & real_key[:, None, :]
    s = jnp.where(mask[None, :, None, None], s, -jnp.inf)
    sink = sinks.astype(jnp.float32).reshape(Hkv, G)[None, None, :, :, None, None]
    m = jnp.maximum(jnp.max(s, axis=-1, keepdims=True), sink)
    e = jnp.exp(s - m)
    p = e / (jnp.sum(e, axis=-1, keepdims=True) + jnp.exp(sink - m))
    out = jnp.einsum('bnhgqk,bnkhd->bnqhgd', p.astype(v.dtype), v_band)
    return out.reshape(B, S, Hq * D)


def causal_depthwise_conv(u, w):
    C = u.shape[-1]
    return lax.conv_general_dilated(
        u, w[:, None, :].astype(u.dtype), window_strides=(1,),
        padding=[(CONV_WIDTH - 1, 0)], dimension_numbers=('NWC', 'WIO', 'NWC'),
        feature_group_count=C)


def swiglu(x, w_gate, w_up, w_down):
    return (jax.nn.silu(x @ w_gate) * (x @ w_up)) @ w_down


def moe_swiglu(x, w_router, w_gate, w_up, w_down):
    B, S, D = x.shape
    xt = x.reshape(B * S, D)
    logits = (xt @ w_router).astype(jnp.float32)
    top_vals, top_idx = lax.top_k(logits, TOP_K)
    top_w = jax.nn.softmax(top_vals, axis=-1)
    gates = jnp.einsum('nk,nke->ne', top_w, jax.nn.one_hot(top_idx, N_EXPERTS, dtype=jnp.float32))
    gates = gates.astype(x.dtype)
    y = jnp.zeros_like(xt)
    for e in range(N_EXPERTS):
        y = y + gates[:, e:e + 1] * swiglu(xt, w_gate[e], w_up[e], w_down[e])
    return y.reshape(B, S, D)


def even_layer(h, cos_m, sin_m, norm_mix, w_in, b_forget, q_norm, w_q_up, kv_norm, w_kv_up,
               w_out, norm_ffn, w_gate, w_up, w_down):
    B, S, _ = h.shape
    xn = rms_norm(h, norm_mix)
    fq, fk, fv, f_logit, c_q, c_kv, k_rope = split_cols(xn @ w_in, SPLIT_EVEN)
    log_f = jax.nn.log_sigmoid(f_logit.astype(jnp.float32) + b_forget.astype(jnp.float32))
    decay_cum = jnp.cumsum(log_f, axis=1)
    o_fox = causal_block_attention(
        fq.reshape(B, S, FOX_HEADS, HEAD_DIM), fk.reshape(B, S, FOX_HEADS, HEAD_DIM),
        fv.reshape(B, S, FOX_HEADS, HEAD_DIM), HEAD_DIM ** -0.5, decay_cum)
    q = (rms_norm(c_q, q_norm) @ w_q_up).reshape(B, S, MLA_HEADS, MLA_NOPE_DIM + MLA_ROPE_DIM)
    q_pe = apply_rope(q[..., MLA_NOPE_DIM:], cos_m, sin_m)
    kv = (rms_norm(c_kv, kv_norm) @ w_kv_up).reshape(B, S, MLA_HEADS, MLA_NOPE_DIM + MLA_V_DIM)
    k_nope, v = kv[..., :MLA_NOPE_DIM], kv[..., MLA_NOPE_DIM:]
    k_pe = apply_rope(k_rope.reshape(B, S, 1, MLA_ROPE_DIM), cos_m, sin_m)
    q_full = jnp.concatenate([q[..., :MLA_NOPE_DIM], q_pe], axis=-1)
    k_full = jnp.concatenate(
        [k_nope, jnp.broadcast_to(k_pe, (B, S, MLA_HEADS, MLA_ROPE_DIM))], axis=-1)
    o_mla = causal_block_attention(q_full, k_full, v, (MLA_NOPE_DIM + MLA_ROPE_DIM) ** -0.5)
    mix = jnp.concatenate([o_fox.reshape(B, S, -1), o_mla.reshape(B, S, -1)], axis=-1)
    h = h + mix @ w_out
    return h + swiglu(rms_norm(h, norm_ffn), w_gate, w_up, w_down)


def odd_layer(h, cos_s, sin_s, norm_mix, w_in, sinks, conv_w, w_out, norm_ffn, w_router,
              w_gate, w_up, w_down):
    B, S, _ = h.shape
    xn = rms_norm(h, norm_mix)
    sq, sk, sv, gate_b, gate_c, xc = split_cols(xn @ w_in, SPLIT_ODD)
    q = apply_rope(sq.reshape(B, S, SWA_HEADS, HEAD_DIM), cos_s, sin_s)
    k = apply_rope(sk.reshape(B, S, SWA_KV_HEADS, HEAD_DIM), cos_s, sin_s)
    v = sv.reshape(B, S, SWA_KV_HEADS, HEAD_DIM)
    o_swa = sliding_window_sink_attention(q, k, v, sinks, HEAD_DIM ** -0.5)
    o_conv = gate_b * causal_depthwise_conv(gate_c * xc, conv_w)
    mix = jnp.concatenate([o_swa, o_conv], axis=-1)
    h = h + mix @ w_out
    return h + moe_swiglu(rms_norm(h, norm_ffn), w_router, w_gate, w_up, w_down)


def setup_inputs(seed: int = 0) -> dict:
    key = jax.random.key(seed)
    ks = jax.random.split(key, 26)

    def w(k, shape, fan_in):
        return jax.random.normal(k, shape, jnp.float32) * fan_in ** -0.5

    def gain(k, n):
        return 1.0 + 0.02 * jax.random.normal(k, (n,), jnp.float32)

    x = jax.random.normal(ks[0], (BATCH, SEQ, D_MODEL), jnp.float32)
    offsets = jax.random.randint(ks[1], (BATCH, 1), 0, 4096, dtype=jnp.int32)
    positions = (jnp.arange(SEQ, dtype=jnp.int32)[None, :] + offsets).astype(jnp.int32)
    return {
        'x': x,
        'positions': positions,
        'l0_norm_mix': gain(ks[2], D_MODEL),
        'l0_w_in': w(ks[3], (D_MODEL, IN_EVEN), D_MODEL),
        'l0_b_forget': 2.0 + 0.1 * jax.random.normal(ks[4], (FOX_HEADS,), jnp.float32),
        'l0_q_norm': gain(ks[5], MLA_Q_RANK),
        'l0_w_q_up': w(ks[6], (MLA_Q_RANK, MLA_HEADS * (MLA_NOPE_DIM + MLA_ROPE_DIM)), MLA_Q_RANK),
        'l0_kv_norm': gain(ks[7], MLA_KV_RANK),
        'l0_w_kv_up': w(ks[8], (MLA_KV_RANK, MLA_HEADS * (MLA_NOPE_DIM + MLA_V_DIM)), MLA_KV_RANK),
        'l0_w_out': w(ks[9], (MIX_EVEN, D_MODEL), MIX_EVEN),
        'l0_norm_ffn': gain(ks[10], D_MODEL),
        'l0_w_gate': w(ks[11], (D_MODEL, D_FF), D_MODEL),
        'l0_w_up': w(ks[12], (D_MODEL, D_FF), D_MODEL),
        'l0_w_down': w(ks[13], (D_FF, D_MODEL), D_FF),
        'l1_norm_mix': gain(ks[14], D_MODEL),
        'l1_w_in': w(ks[15], (D_MODEL, IN_ODD), D_MODEL),
        'l1_sinks': 0.5 * jax.random.normal(ks[16], (SWA_HEADS,), jnp.float32),
        'l1_conv_w': w(ks[17], (CONV_WIDTH, CONV_CHANNELS), CONV_WIDTH),
        'l1_w_out': w(ks[18], (MIX_ODD, D_MODEL), MIX_ODD),
        'l1_norm_ffn': gain(ks[19], D_MODEL),
        'l1_w_router': w(ks[20], (D_MODEL, N_EXPERTS), D_MODEL),
        'l1_w_gate': w(ks[21], (N_EXPERTS, D_MODEL, D_FF), D_MODEL),
        'l1_w_up': w(ks[22], (N_EXPERTS, D_MODEL, D_FF), D_MODEL),
        'l1_w_down': w(ks[23], (N_EXPERTS, D_FF, D_MODEL), D_FF),
        'final_norm': gain(ks[24], D_MODEL),
    }


def reference(x, positions, l0_norm_mix, l0_w_in, l0_b_forget, l0_q_norm, l0_w_q_up, l0_kv_norm,
              l0_w_kv_up, l0_w_out, l0_norm_ffn, l0_w_gate, l0_w_up, l0_w_down,
              l1_norm_mix, l1_w_in, l1_sinks, l1_conv_w, l1_w_out, l1_norm_ffn, l1_w_router,
              l1_w_gate, l1_w_up, l1_w_down, final_norm):
    cos_m, sin_m = rope_tables(positions, MLA_ROPE_DIM)
    cos_s, sin_s = rope_tables(positions, HEAD_DIM)
    layer_params = (
        (l0_norm_mix, l0_w_in, l0_b_forget, l0_q_norm, l0_w_q_up, l0_kv_norm, l0_w_kv_up,
         l0_w_out, l0_norm_ffn, l0_w_gate, l0_w_up, l0_w_down),
        (l1_norm_mix, l1_w_in, l1_sinks, l1_conv_w, l1_w_out, l1_norm_ffn, l1_w_router,
         l1_w_gate, l1_w_up, l1_w_down),
    )
    h = x
    for i in range(DEPTH):
        if i % 2 == 0:
            h = even_layer(h, cos_m, sin_m, *layer_params[i])
        else:
            h = odd_layer(h, cos_s, sin_s, *layer_params[i])
    return rms_norm(h, final_norm)
```

```python
import functools

import jax
import jax.numpy as jnp
from jax import lax
from jax.experimental import pallas as pl
from jax.experimental.pallas import tpu as pltpu

F32 = jnp.float32
BF16 = jnp.bfloat16

D_MODEL = 1024
HEAD_DIM = 64
RMS_EPS = 1e-6
ROPE_THETA = 10000.0
FOX_HEADS = 8
MLA_HEADS = 8
MLA_Q_RANK = 384
MLA_KV_RANK = 256
MLA_NOPE_DIM = 64
MLA_ROPE_DIM = 32
MLA_V_DIM = 64
SWA_HEADS = 8
SWA_KV_HEADS = 2
WINDOW = 128
CONV_CHANNELS = 512
CONV_WIDTH = 3
D_FF = 3584
N_EXPERTS = 8

LANES = 128
MASK_VALUE = -1e30
VMEM_LIMIT = 56 * 1024 * 1024

QKV_W = 3 * FOX_HEADS * HEAD_DIM
MLA_PAD = MLA_HEADS * LANES
L0_COLS = QKV_W + MLA_Q_RANK + MLA_KV_RANK + 2 * LANES
FFN_TM = 1024
FFN_TF = 512
L1_COLS = SWA_HEADS * HEAD_DIM + 2 * SWA_KV_HEADS * HEAD_DIM + 3 * CONV_CHANNELS


def _params(*sem):
    return pltpu.CompilerParams(dimension_semantics=sem, vmem_limit_bytes=VMEM_LIMIT)


def _rms(x, g):
    return x * lax.rsqrt(jnp.mean(x * x, axis=-1, keepdims=True) + RMS_EPS) * g


def _dot(a, b):
    return jnp.dot(a, b, preferred_element_type=F32)


def _dot_nt(a, b):
    return lax.dot_general(a, b, (((1,), (1,)), ((), ())), preferred_element_type=F32)


def _rope(x, cos, sin_lo, sin_hi, half):
    return (x * cos + pltpu.roll(x, half, 1) * sin_hi
            + pltpu.roll(x, LANES - half, 1) * sin_lo)


def _l0_pre_kernel(x_ref, pos_ref, g_ref, w_ref, bf_ref, qn_ref, wq_ref, kvn_ref, wk_ref,
                   wv_ref, tri_ref, invf_ref, mlo_ref, mhi_ref,
                   qkv_ref, q_ref, k_ref, v_ref, c_ref, ct_ref, carry_ref):
    @pl.when(pl.program_id(1) == 0)
    def _():
        carry_ref[...] = jnp.zeros_like(carry_ref)

    xn = _rms(x_ref[...], g_ref[...]).astype(BF16)
    y = _dot(xn, w_ref[...])
    qkv_ref[...] = y[:, :QKV_W].astype(BF16)
    o = QKV_W
    cq = y[:, o:o + MLA_Q_RANK]
    o += MLA_Q_RANK
    ckv = y[:, o:o + MLA_KV_RANK]
    o += MLA_KV_RANK
    kr = y[:, o:o + LANES]
    fl = y[:, o + LANES:o + 2 * LANES]

    q = _dot(_rms(cq, qn_ref[...]).astype(BF16), wq_ref[...])
    ckn = _rms(ckv, kvn_ref[...]).astype(BF16)
    kk = _dot(ckn, wk_ref[...])
    v_ref[...] = _dot(ckn, wv_ref[...]).astype(BF16)

    ang = pos_ref[...].astype(F32) * invf_ref[...]
    cos = jnp.cos(ang)
    sin = jnp.sin(ang)
    sin_lo = sin * mlo_ref[...]
    sin_hi = sin * mhi_ref[...]
    half = MLA_ROPE_DIM // 2
    kpe = _rope(kr, cos, sin_lo, sin_hi, half)
    for h in range(MLA_HEADS):
        sl = slice(h * LANES, (h + 1) * LANES)
        q_ref[:, sl] = _rope(q[:, sl], cos, sin_lo, sin_hi, half).astype(BF16)
        k_ref[:, sl] = (kk[:, sl] + kpe).astype(BF16)

    z = fl + bf_ref[...]
    ls = jnp.minimum(z, 0.0) - jnp.log1p(jnp.exp(-jnp.abs(z)))
    hi = ls.astype(BF16)
    r1 = ls - hi.astype(F32)
    mid = r1.astype(BF16)
    lo = (r1 - mid.astype(F32)).astype(BF16)
    tri = tri_ref[...]
    c = _dot(tri, hi) + _dot(tri, mid) + _dot(tri, lo) + carry_ref[...]
    tm = c.shape[0]
    carry_ref[...] = c[tm - 1:tm, :]
    c_ref[...] = c
    ct_ref[0] = c.T[:FOX_HEADS, :]


def _l0_pre(x2, pos2, g, w_ext, bf_row, qn, wq, kvn, wk, wv, B, S, tm):
    N = B * S
    ns = S // tm
    tri = (jnp.arange(tm)[:, None] >= jnp.arange(tm)[None, :]).astype(BF16)
    lane = jnp.arange(LANES)
    half = MLA_ROPE_DIM // 2
    in_pe = (lane >= MLA_NOPE_DIM) & (lane < MLA_NOPE_DIM + MLA_ROPE_DIM)
    freq = ROPE_THETA ** (-jnp.arange(0, MLA_ROPE_DIM, 2, dtype=F32) / MLA_ROPE_DIM)
    invf = jnp.where(in_pe, freq[(lane - MLA_NOPE_DIM) % half], 0.0)[None, :].astype(F32)
    m_lo = jnp.where(in_pe & (lane < MLA_NOPE_DIM + half), -1.0, 0.0)[None, :].astype(F32)
    m_hi = jnp.where(in_pe & (lane >= MLA_NOPE_DIM + half), 1.0, 0.0)[None, :].astype(F32)

    row = lambda b, s: (b * ns + s, 0)
    const = lambda b, s: (0, 0)
    full = lambda a: pl.BlockSpec(a.shape, const)
    return pl.pallas_call(
        _l0_pre_kernel,
        grid=(B, ns),
        in_specs=[pl.BlockSpec((tm, D_MODEL), row), pl.BlockSpec((tm, 1), row),
                  full(g), full(w_ext), full(bf_row), full(qn), full(wq), full(kvn), full(wk),
                  full(wv), full(tri), full(invf), full(m_lo), full(m_hi)],
        out_specs=[pl.BlockSpec((tm, QKV_W), row), pl.BlockSpec((tm, MLA_PAD), row),
                   pl.BlockSpec((tm, MLA_PAD), row), pl.BlockSpec((tm, MLA_HEADS * MLA_V_DIM), row),
                   pl.BlockSpec((tm, LANES), row),
                   pl.BlockSpec((1, FOX_HEADS, tm), lambda b, s: (b, 0, s))],
        out_shape=[jax.ShapeDtypeStruct((N, QKV_W), BF16), jax.ShapeDtypeStruct((N, MLA_PAD), BF16),
                   jax.ShapeDtypeStruct((N, MLA_PAD), BF16),
                   jax.ShapeDtypeStruct((N, MLA_HEADS * MLA_V_DIM), BF16),
                   jax.ShapeDtypeStruct((N, LANES), F32),
                   jax.ShapeDtypeStruct((B, FOX_HEADS, S), F32)],
        scratch_shapes=[pltpu.VMEM((1, LANES), F32)],
        compiler_params=_params("arbitrary", "arbitrary"),
        name="l0_pre",
    )(x2, pos2, g, w_ext, bf_row, qn, wq, kvn, wk, wv, tri, invf, m_lo, m_hi)


def _attn_kernel(*refs, fox, tq, tk):
    if fox:
        q_ref, k_ref, v_ref, c_ref, ct_ref, o_ref, m_ref, l_ref, acc_ref = refs
    else:
        q_ref, k_ref, v_ref, o_ref, m_ref, l_ref, acc_ref = refs
    hp = pl.program_id(1)
    qi = pl.program_id(2)
    first = lax.broadcasted_iota(jnp.int32, (1, LANES), 1) < HEAD_DIM

    q = q_ref[0]
    if fox:
        zero = jnp.zeros_like(q)
        qs = (jnp.where(first, q, zero), jnp.where(first, zero, q))
        c = c_ref[0]
        lane = lax.broadcasted_iota(jnp.int32, c.shape, 1)
        cq = tuple(jnp.sum(jnp.where(lane == 2 * hp + i, c, 0.0), axis=1, keepdims=True)
                   for i in range(2))
    else:
        qs = (q[:, :LANES], q[:, LANES:])

    m_ref[...] = jnp.full_like(m_ref, MASK_VALUE)
    l_ref[...] = jnp.zeros_like(l_ref)
    acc_ref[...] = jnp.zeros_like(acc_ref)

    def step(j, masked):
        start = pl.multiple_of(j * tk, tk)
        ks = k_ref[0, pl.ds(start, tk), :]
        vs = v_ref[0, pl.ds(start, tk), :]
        vzero = jnp.zeros_like(vs)
        vparts = (jnp.where(first, vs, vzero), jnp.where(first, vzero, vs))
        if masked:
            rows = qi * tq + lax.broadcasted_iota(jnp.int32, (tq, tk), 0)
            cols = start + lax.broadcasted_iota(jnp.int32, (tq, tk), 1)
            keep = rows >= cols
        pv = None
        alphas = []
        for i in range(2):
            kh = ks if fox else ks[:, i * LANES:(i + 1) * LANES]
            s = _dot_nt(qs[i], kh)
            if fox:
                s = s - ct_ref[0, 0, i:i + 1, pl.ds(start, tk)]
            if masked:
                s = jnp.where(keep, s, MASK_VALUE)
            m_prev = m_ref[i]
            row_max = jnp.max(s, axis=1, keepdims=True)
            if fox:
                m_new = jnp.maximum(m_prev, row_max + cq[i])
                shift = m_new - cq[i]
            else:
                m_new = jnp.maximum(m_prev, row_max)
                shift = m_new
            alpha = jnp.exp(m_prev - m_new)
            p = jnp.exp(s - shift)
            l_ref[i] = alpha * l_ref[i] + jnp.sum(p, axis=1, keepdims=True)
            m_ref[i] = m_new
            alphas.append(alpha)
            d = _dot(p.astype(BF16), vparts[i])
            pv = d if pv is None else pv + d
        acc_ref[...] = acc_ref[...] * jnp.where(first, alphas[0], alphas[1]) + pv

    n_full = qi * (tq // tk)
    lax.fori_loop(0, n_full, lambda j, _: (step(j, False), 0)[1], 0)
    for d in range(tq // tk):
        step(n_full + d, True)

    inv = jnp.where(first, 1.0 / l_ref[0], 1.0 / l_ref[1])
    o_ref[0] = (acc_ref[...] * inv).astype(o_ref.dtype)


def _attention(q, k, v, c, ct, B, S, tq, tk, fox):
    n_pairs = FOX_HEADS // 2
    nq = S // tq
    qw = LANES if fox else 2 * LANES
    if fox:
        in_specs = [pl.BlockSpec((1, tq, LANES), lambda b, h, i: (b, i, h)),
                    pl.BlockSpec((1, S, LANES), lambda b, h, i: (b, 0, n_pairs + h)),
                    pl.BlockSpec((1, S, LANES), lambda b, h, i: (b, 0, 2 * n_pairs + h)),
                    pl.BlockSpec((1, tq, LANES), lambda b, h, i: (b, i, 0)),
                    pl.BlockSpec((1, 1, 2, S), lambda b, h, i: (b, h, 0, 0))]
        args = (q, k, v, c, ct)
    else:
        in_specs = [pl.BlockSpec((1, tq, qw), lambda b, h, i: (b, i, h)),
                    pl.BlockSpec((1, S, qw), lambda b, h, i: (b, 0, h)),
                    pl.BlockSpec((1, S, LANES), lambda b, h, i: (b, 0, h))]
        args = (q, k, v)
    return pl.pallas_call(
        functools.partial(_attn_kernel, fox=fox, tq=tq, tk=tk),
        grid=(B, n_pairs, nq),
        in_specs=in_specs,
        out_specs=pl.BlockSpec((1, tq, LANES), lambda b, h, i: (b, i, h)),
        out_shape=jax.ShapeDtypeStruct((B, S, n_pairs * LANES), BF16),
        scratch_shapes=[pltpu.VMEM((2, tq, 1), F32), pltpu.VMEM((2, tq, 1), F32),
                        pltpu.VMEM((tq, LANES), F32)],
        compiler_params=_params("arbitrary", "arbitrary", "arbitrary"),
        name="fox_attn" if fox else "mla_attn",
    )(*args)


def _ffn_kernel(*refs, routed, final):
    it = iter(refs)
    h_ref, a_ref, b_ref, wa_ref, wb_ref, g_ref = (next(it) for _ in range(6))
    wr_ref = next(it) if routed else None
    wg_ref, wu_ref, wd_ref = next(it), next(it), next(it)
    fn_ref = next(it) if final else None
    o_ref, hres_ref, xn_ref, acc_ref = next(it), next(it), next(it), next(it)
    gates_ref = next(it) if routed else None
    gcol_ref = next(it) if routed else None
    e = pl.program_id(1)
    f = pl.program_id(2)
    first_step = (e == 0) & (f == 0)
    last_step = (e == pl.num_programs(1) - 1) & (f == pl.num_programs(2) - 1)

    @pl.when(first_step)
    def _():
        hres = h_ref[...] + _dot(a_ref[...], wa_ref[...]) + _dot(b_ref[...], wb_ref[...])
        hres_ref[...] = hres
        xn = _rms(hres, g_ref[...])
        xn_ref[...] = xn.astype(BF16)
        acc_ref[...] = jnp.zeros_like(acc_ref)
        if routed:
            xh = xn.astype(BF16)
            xl = (xn - xh.astype(F32)).astype(BF16)
            wr = wr_ref[...]
            wh = wr.astype(BF16)
            wl = (wr - wh.astype(F32)).astype(BF16)
            logits = _dot(xh, wh) + (_dot(xh, wl) + _dot(xl, wh))
            lane = lax.broadcasted_iota(jnp.int32, logits.shape, 1)
            logits = jnp.where(lane < N_EXPERTS, logits, MASK_VALUE)
            m1 = jnp.max(logits, axis=1, keepdims=True)
            i1 = jnp.min(jnp.where(logits == m1, lane, LANES), axis=1, keepdims=True)
            rest = jnp.where(lane == i1, MASK_VALUE, logits)
            m2 = jnp.max(rest, axis=1, keepdims=True)
            i2 = jnp.min(jnp.where(rest == m2, lane, LANES), axis=1, keepdims=True)
            e2 = jnp.exp(m2 - m1)
            w1 = 1.0 / (1.0 + e2)
            w2 = e2 / (1.0 + e2)
            gates_ref[...] = jnp.where(lane == i1, w1, 0.0) + jnp.where(lane == i2, w2, 0.0)

    if routed:
        @pl.when(f == 0)
        def _():
            gts = gates_ref[...]
            lane = lax.broadcasted_iota(jnp.int32, gts.shape, 1)
            gcol_ref[...] = jnp.sum(jnp.where(lane == e, gts, 0.0), axis=1, keepdims=True)

    xn = xn_ref[...]
    gate = _dot(xn, wg_ref[0])
    up = _dot(xn, wu_ref[0])
    act = gate * (1.0 / (1.0 + jnp.exp(-gate))) * up
    if routed:
        act = act * gcol_ref[...]
    acc_ref[...] += _dot(act.astype(BF16), wd_ref[0])

    @pl.when(last_step)
    def _():
        out = hres_ref[...] + acc_ref[...]
        if final:
            out = _rms(out, fn_ref[...])
        o_ref[...] = out


def _ffn(h, a, b, wa, wb, g, w_router, w_gate, w_up, w_down, final_g, tm, tf):
    N = h.shape[0]
    n_exp = w_gate.shape[0]
    routed = w_router is not None
    final = final_g is not None
    nf = D_FF // tf
    row = lambda i, e, f: (i, 0)
    const = lambda i, e, f: (0, 0)
    full = lambda x: pl.BlockSpec(x.shape, const)
    half_w = a.shape[1]
    in_specs = [pl.BlockSpec((tm, D_MODEL), row), pl.BlockSpec((tm, half_w), row),
                pl.BlockSpec((tm, half_w), row), full(wa), full(wb), full(g)]
    args = [h, a, b, wa, wb, g]
    if routed:
        in_specs.append(full(w_router))
        args.append(w_router)
    in_specs += [pl.BlockSpec((1, D_MODEL, tf), lambda i, e, f: (e, 0, f)),
                 pl.BlockSpec((1, D_MODEL, tf), lambda i, e, f: (e, 0, f)),
                 pl.BlockSpec((1, tf, D_MODEL), lambda i, e, f: (e, f, 0))]
    args += [w_gate, w_up, w_down]
    if final:
        in_specs.append(full(final_g))
        args.append(final_g)
    scratch = [pltpu.VMEM((tm, D_MODEL), F32), pltpu.VMEM((tm, D_MODEL), BF16),
               pltpu.VMEM((tm, D_MODEL), F32)]
    if routed:
        scratch += [pltpu.VMEM((tm, LANES), F32), pltpu.VMEM((tm, 1), F32)]
    return pl.pallas_call(
        functools.partial(_ffn_kernel, routed=routed, final=final),
        grid=(N // tm, n_exp, nf),
        in_specs=in_specs,
        out_specs=pl.BlockSpec((tm, D_MODEL), row),
        out_shape=jax.ShapeDtypeStruct((N, D_MODEL), F32),
        scratch_shapes=scratch,
        compiler_params=_params("arbitrary", "arbitrary", "arbitrary"),
        name="moe_ffn" if routed else "dense_ffn",
    )(*args)


def _l1_pre_kernel(x_ref, pos_ref, g_ref, w_ref, cw_ref, invf_ref, mlo_ref, mhi_ref,
                   q_ref, kv_ref, oc_ref, tail_ref):
    @pl.when(pl.program_id(1) == 0)
    def _():
        tail_ref[...] = jnp.zeros_like(tail_ref)

    xn = _rms(x_ref[...], g_ref[...]).astype(BF16)
    y = _dot(xn, w_ref[...])
    ang = pos_ref[...].astype(F32) * invf_ref[...]
    cos = jnp.cos(ang)
    sin = jnp.sin(ang)
    sin_lo = sin * mlo_ref[...]
    sin_hi = sin * mhi_ref[...]
    half = HEAD_DIM // 2
    qw = SWA_HEADS * HEAD_DIM
    for j in range(qw // LANES):
        sl = slice(j * LANES, (j + 1) * LANES)
        q_ref[:, sl] = _rope(y[:, sl], cos, sin_lo, sin_hi, half).astype(BF16)
    kv_ref[:, :LANES] = _rope(y[:, qw:qw + LANES], cos, sin_lo, sin_hi, half).astype(BF16)
    kv_ref[:, LANES:] = y[:, qw + LANES:qw + 2 * LANES].astype(BF16)

    o = qw + 2 * LANES
    C = CONV_CHANNELS
    gate_b = y[:, o:o + C]
    u = y[:, o + C:o + 2 * C] * y[:, o + 2 * C:o + 3 * C]
    tm = u.shape[0]
    rows = lax.broadcasted_iota(jnp.int32, u.shape, 0)
    tail = tail_ref[...]
    u1 = jnp.where(rows == 0, tail[7:8, :], pltpu.roll(u, 1, 0))
    u2 = jnp.where(rows == 0, tail[6:7, :], jnp.where(rows == 1, tail[7:8, :], pltpu.roll(u, 2, 0)))
    cw = cw_ref[...]
    oc_ref[...] = (gate_b * (cw[0:1, :] * u2 + cw[1:2, :] * u1 + cw[2:3, :] * u)).astype(BF16)
    tail_ref[...] = u[tm - 8:tm, :]


def _l1_pre(x2, pos2, g, w_in, conv_w, B, S, tm):
    N = B * S
    ns = S // tm
    lane = jnp.arange(LANES)
    half = HEAD_DIM // 2
    freq = ROPE_THETA ** (-jnp.arange(0, HEAD_DIM, 2, dtype=F32) / HEAD_DIM)
    invf = freq[lane % half][None, :].astype(F32)
    m_lo = jnp.where(lane % HEAD_DIM < half, -1.0, 0.0)[None, :].astype(F32)
    m_hi = jnp.where(lane % HEAD_DIM >= half, 1.0, 0.0)[None, :].astype(F32)
    row = lambda b, s: (b * ns + s, 0)
    const = lambda b, s: (0, 0)
    full = lambda a: pl.BlockSpec(a.shape, const)
    return pl.pallas_call(
        _l1_pre_kernel,
        grid=(B, ns),
        in_specs=[pl.BlockSpec((tm, D_MODEL), row), pl.BlockSpec((tm, 1), row),
                  full(g), full(w_in), full(conv_w), full(invf), full(m_lo), full(m_hi)],
        out_specs=[pl.BlockSpec((tm, SWA_HEADS * HEAD_DIM), row),
                   pl.BlockSpec((tm, 2 * LANES), row),
                   pl.BlockSpec((tm, CONV_CHANNELS), row)],
        out_shape=[jax.ShapeDtypeStruct((N, SWA_HEADS * HEAD_DIM), BF16),
                   jax.ShapeDtypeStruct((N, 2 * LANES), BF16),
                   jax.ShapeDtypeStruct((N, CONV_CHANNELS), BF16)],
        scratch_shapes=[pltpu.VMEM((8, CONV_CHANNELS), F32)],
        compiler_params=_params("arbitrary", "arbitrary"),
        name="l1_pre",
    )(x2, pos2, g, w_in, conv_w, invf, m_lo, m_hi)


def _swa_kernel(sink_ref, q_ref, kv_ref, kvp_ref, o_ref, *, tq):
    s_idx = pl.program_id(1)
    first = lax.broadcasted_iota(jnp.int32, (1, LANES), 1) < HEAD_DIM
    G = SWA_HEADS // SWA_KV_HEADS
    W = WINDOW

    def dup(x):
        r = pltpu.roll(x, HEAD_DIM, 1)
        return (jnp.where(first, x, r), jnp.where(first, r, x))

    kv = kv_ref[0]
    kvp = kvp_ref[0]
    k_all = jnp.concatenate([kvp[:, :LANES], kv[:, :LANES]], axis=0)
    v_all = jnp.concatenate([kvp[:, LANES:], kv[:, LANES:]], axis=0)
    k_dup = dup(k_all)
    v_dup = dup(v_all)
    q = q_ref[0]
    zero = jnp.zeros((W, LANES), q.dtype)
    rel = (lax.broadcasted_iota(jnp.int32, (W, 2 * W), 0) + W
           - lax.broadcasted_iota(jnp.int32, (W, 2 * W), 1))
    in_window = (rel >= 0) & (rel < W)
    for r in range(tq // W):
        key_pos = (s_idx * tq + r * W - W
                   + lax.broadcasted_iota(jnp.int32, (W, 2 * W), 1))
        keep = in_window & (key_pos >= 0)
        keep4 = jnp.concatenate([keep] * (2 * (G // 2)), axis=0)
        for g in range(SWA_KV_HEADS):
            kb = k_dup[g][r * W:r * W + 2 * W]
            vb = v_dup[g][r * W:r * W + 2 * W]
            parts = []
            sinks = []
            for j in range(G // 2):
                blk = g * (G // 2) + j
                q2 = q[r * W:(r + 1) * W, blk * LANES:(blk + 1) * LANES]
                parts += [jnp.where(first, q2, zero), jnp.where(first, zero, q2)]
                sinks += [jnp.full((W, 1), sink_ref[2 * blk], F32),
                          jnp.full((W, 1), sink_ref[2 * blk + 1], F32)]
            qst = jnp.concatenate(parts, axis=0)
            sink = jnp.concatenate(sinks, axis=0)
            s = _dot_nt(qst, kb)
            s = jnp.where(keep4, s, MASK_VALUE)
            m = jnp.maximum(jnp.max(s, axis=1, keepdims=True), sink)
            e = jnp.exp(s - m)
            den = jnp.sum(e, axis=1, keepdims=True) + jnp.exp(sink - m)
            o = _dot((e * (1.0 / den)).astype(BF16), vb)
            for j in range(G // 2):
                blk = g * (G // 2) + j
                oa = o[(2 * j) * W:(2 * j + 1) * W]
                ob = o[(2 * j + 1) * W:(2 * j + 2) * W]
                o_ref[0, r * W:(r + 1) * W, blk * LANES:(blk + 1) * LANES] = (
                    jnp.where(first, oa, ob).astype(o_ref.dtype))


def _swa(sinks, q, kv, B, S, tq):
    ns = S // tq
    per = tq // WINDOW
    return pl.pallas_call(
        functools.partial(_swa_kernel, tq=tq),
        grid=(B, ns),
        in_specs=[pl.BlockSpec(memory_space=pltpu.SMEM),
                  pl.BlockSpec((1, tq, SWA_HEADS * HEAD_DIM), lambda b, s: (b, s, 0)),
                  pl.BlockSpec((1, tq, 2 * LANES), lambda b, s: (b, s, 0)),
                  pl.BlockSpec((1, WINDOW, 2 * LANES),
                               lambda b, s: (b, jnp.maximum(s * per - 1, 0), 0))],
        out_specs=pl.BlockSpec((1, tq, SWA_HEADS * HEAD_DIM), lambda b, s: (b, s, 0)),
        out_shape=jax.ShapeDtypeStruct((B, S, SWA_HEADS * HEAD_DIM), BF16),
        compiler_params=_params("arbitrary", "arbitrary"),
        name="swa_attn",
    )(sinks, q, kv, kv)


def _pad_heads(w, n_heads, width, offset=0):
    K = w.shape[0]
    w3 = w.reshape(K, n_heads, width)
    out = jnp.zeros((K, n_heads, LANES), w.dtype).at[:, :, offset:offset + width].set(w3)
    return out.reshape(K, n_heads * LANES)


def kernel(x, positions, l0_norm_mix, l0_w_in, l0_b_forget, l0_q_norm, l0_w_q_up, l0_kv_norm,
           l0_w_kv_up, l0_w_out, l0_norm_ffn, l0_w_gate, l0_w_up, l0_w_down,
           l1_norm_mix, l1_w_in, l1_sinks, l1_conv_w, l1_w_out, l1_norm_ffn, l1_w_router,
           l1_w_gate, l1_w_up, l1_w_down, final_norm):
    B, S, D = x.shape
    assert D == D_MODEL and S % 512 == 0
    N = B * S
    tm = 512
    x2 = x.reshape(N, D)
    pos2 = positions.reshape(N, 1)
    rowv = lambda v: v.reshape(1, -1).astype(F32)

    fox_w = FOX_HEADS * HEAD_DIM
    o = 3 * fox_w
    w_fq = l0_w_in[:, :fox_w] * (HEAD_DIM ** -0.5)
    w_f = l0_w_in[:, o:o + FOX_HEADS]
    o += FOX_HEADS
    w_cq = l0_w_in[:, o:o + MLA_Q_RANK]
    o += MLA_Q_RANK
    w_ckv = l0_w_in[:, o:o + MLA_KV_RANK]
    o += MLA_KV_RANK
    w_kr = l0_w_in[:, o:o + MLA_ROPE_DIM]
    w_ext = jnp.concatenate(
        [w_fq, l0_w_in[:, fox_w:3 * fox_w], w_cq, w_ckv,
         _pad_heads(w_kr, 1, MLA_ROPE_DIM, MLA_NOPE_DIM), _pad_heads(w_f, 1, FOX_HEADS)],
        axis=1).astype(BF16)
    bf_row = jnp.zeros((1, LANES), F32).at[0, :FOX_HEADS].set(l0_b_forget.astype(F32))
    qk_dim = MLA_NOPE_DIM + MLA_ROPE_DIM
    wq = _pad_heads(l0_w_q_up * (qk_dim ** -0.5), MLA_HEADS, qk_dim).astype(BF16)
    kv3 = l0_w_kv_up.reshape(MLA_KV_RANK, MLA_HEADS, MLA_NOPE_DIM + MLA_V_DIM)
    wk = _pad_heads(kv3[:, :, :MLA_NOPE_DIM].reshape(MLA_KV_RANK, -1), MLA_HEADS,
                    MLA_NOPE_DIM).astype(BF16)
    wv = kv3[:, :, MLA_NOPE_DIM:].reshape(MLA_KV_RANK, -1).astype(BF16)

    qkv, q_m, k_m, v_m, c, ct = _l0_pre(
        x2, pos2, rowv(l0_norm_mix), w_ext, bf_row, rowv(l0_q_norm), wq, rowv(l0_kv_norm),
        wk, wv, B, S, tm)

    tq = tk = 512
    qkv3 = qkv.reshape(B, S, QKV_W)
    o_fox = _attention(qkv3, qkv3, qkv3, c.reshape(B, S, LANES),
                       ct.reshape(B, FOX_HEADS // 2, 2, S), B, S, tq, tk, True)
    o_mla = _attention(q_m.reshape(B, S, MLA_PAD), k_m.reshape(B, S, MLA_PAD),
                       v_m.reshape(B, S, MLA_HEADS * MLA_V_DIM), None, None, B, S, tq, tk, False)

    w_out0 = l0_w_out.astype(BF16)
    h1 = _ffn(x2, o_fox.reshape(N, -1), o_mla.reshape(N, -1), w_out0[:fox_w], w_out0[fox_w:],
              rowv(l0_norm_ffn), None, l0_w_gate.astype(BF16)[None], l0_w_up.astype(BF16)[None],
              l0_w_down.astype(BF16)[None], None, FFN_TM, FFN_TF)

    swa_w = SWA_HEADS * HEAD_DIM
    w_in1 = jnp.concatenate([l1_w_in[:, :swa_w] * (HEAD_DIM ** -0.5), l1_w_in[:, swa_w:]],
                            axis=1).astype(BF16)
    q_s, kv_s, o_conv = _l1_pre(h1, pos2, rowv(l1_norm_mix), w_in1, l1_conv_w.astype(F32),
                                B, S, tm)
    o_swa = _swa(l1_sinks.astype(F32), q_s.reshape(B, S, swa_w), kv_s.reshape(B, S, 2 * LANES),
                 B, S, 512)
    w_out1 = l1_w_out.astype(BF16)
    w_router = jnp.zeros((D_MODEL, LANES), F32).at[:, :N_EXPERTS].set(l1_w_router.astype(F32))
    out = _ffn(h1, o_swa.reshape(N, -1), o_conv, w_out1[:swa_w], w_out1[swa_w:],
               rowv(l1_norm_ffn), w_router, l1_w_gate.astype(BF16), l1_w_up.astype(BF16),
               l1_w_down.astype(BF16), rowv(final_norm), FFN_TM, FFN_TF)
    return out.reshape(B, S, D)
```

```python
import functools

import jax
import jax.numpy as jnp
from jax import lax
from jax.experimental import pallas as pl
from jax.experimental.pallas import tpu as pltpu

F32 = jnp.float32
BF16 = jnp.bfloat16

D_MODEL = 1024
HEAD_DIM = 64
RMS_EPS = 1e-6
ROPE_THETA = 10000.0
FOX_HEADS = 8
MLA_HEADS = 8
MLA_Q_RANK = 384
MLA_KV_RANK = 256
MLA_NOPE_DIM = 64
MLA_ROPE_DIM = 32
MLA_V_DIM = 64
SWA_HEADS = 8
SWA_KV_HEADS = 2
WINDOW = 128
CONV_CHANNELS = 512
CONV_WIDTH = 3
D_FF = 3584
N_EXPERTS = 8

LANES = 128
MASK_VALUE = -1e30
LOG2E = 1.4426950408889634
VMEM_LIMIT = 56 * 1024 * 1024

QKV_W = 3 * FOX_HEADS * HEAD_DIM
MLA_PAD = MLA_HEADS * LANES
L0_COLS = QKV_W + MLA_Q_RANK + MLA_KV_RANK + 2 * LANES
ATTN_TQ = 512
ATTN_TK = 512
ATTN_TW = 256
FFN_TM = 1024
FFN_TF = 512
L1_COLS = SWA_HEADS * HEAD_DIM + 2 * SWA_KV_HEADS * HEAD_DIM + 3 * CONV_CHANNELS


def _params(*sem):
    return pltpu.CompilerParams(dimension_semantics=sem, vmem_limit_bytes=VMEM_LIMIT)


def _rms(x, g):
    return x * lax.rsqrt(jnp.mean(x * x, axis=-1, keepdims=True) + RMS_EPS) * g


def _dot(a, b):
    return jnp.dot(a, b, preferred_element_type=F32)


def _dot_nt(a, b):
    return lax.dot_general(a, b, (((1,), (1,)), ((), ())), preferred_element_type=F32)


def _rope(x, cos, sin_lo, sin_hi, half):
    return (x * cos + pltpu.roll(x, half, 1) * sin_hi
            + pltpu.roll(x, LANES - half, 1) * sin_lo)


def _l0_pre_kernel(x_ref, pos_ref, g_ref, w_ref, bf_ref, qn_ref, wq_ref, kvn_ref, wk_ref,
                   wv_ref, tri_ref, place_ref, invf_ref, mlo_ref, mhi_ref,
                   qf_ref, kf_ref, vtf_ref, q_ref, k_ref, vt_ref, ct_ref, carry_ref):
    @pl.when(pl.program_id(1) == 0)
    def _():
        carry_ref[...] = jnp.zeros_like(carry_ref)

    fox_w = FOX_HEADS * HEAD_DIM
    xn = _rms(x_ref[...], g_ref[...]).astype(BF16)
    y = _dot(xn, w_ref[...])
    qf_ref[...] = y[:, :fox_w].astype(BF16)
    vtf_ref[0] = y[:, 2 * fox_w:3 * fox_w].T.astype(BF16)
    o = QKV_W
    cq = y[:, o:o + MLA_Q_RANK]
    o += MLA_Q_RANK
    ckv = y[:, o:o + MLA_KV_RANK]
    o += MLA_KV_RANK
    kr = y[:, o:o + LANES]
    fl = y[:, o + LANES:o + 2 * LANES]

    q = _dot(_rms(cq, qn_ref[...]).astype(BF16), wq_ref[...])
    ckn = _rms(ckv, kvn_ref[...]).astype(BF16)
    kk = _dot(ckn, wk_ref[...])
    vt_ref[0] = _dot(ckn, wv_ref[...]).T.astype(BF16)

    ang = pos_ref[...].astype(F32) * invf_ref[...]
    cos = jnp.cos(ang)
    sin = jnp.sin(ang)
    sin_lo = sin * mlo_ref[...]
    sin_hi = sin * mhi_ref[...]
    half = MLA_ROPE_DIM // 2
    kpe = _rope(kr, cos, sin_lo, sin_hi, half)
    for h in range(MLA_HEADS):
        sl = slice(h * LANES, (h + 1) * LANES)
        q_ref[:, sl] = _rope(q[:, sl], cos, sin_lo, sin_hi, half).astype(BF16)
        k_ref[:, sl] = (kk[:, sl] + kpe).astype(BF16)

    z = fl + bf_ref[...]
    ls = jnp.minimum(z, 0.0) - jnp.log1p(jnp.exp(-jnp.abs(z)))
    tri = tri_ref[...]
    c = carry_ref[...]
    for part in _split3(ls):
        c = c + _dot(tri, part)
    tm = c.shape[0]
    carry_ref[...] = c[tm - 1:tm, :]
    c2 = c * LOG2E
    ct_ref[0] = c2.T[:FOX_HEADS, :]
    bias = None
    for i, part in enumerate(_split3(c2)):
        d = _dot(part, place_ref[i])
        bias = d if bias is None else bias + d
    for hp in range(FOX_HEADS // 2):
        kf_ref[:, 2 * hp * LANES:(2 * hp + 1) * LANES] = (
            y[:, fox_w + hp * LANES:fox_w + (hp + 1) * LANES].astype(BF16))
        kf_ref[:, (2 * hp + 1) * LANES:(2 * hp + 2) * LANES] = (
            bias[:, hp * LANES:(hp + 1) * LANES].astype(BF16))


def _split3(x):
    hi = x.astype(BF16)
    r1 = x - hi.astype(F32)
    mid = r1.astype(BF16)
    lo = (r1 - mid.astype(F32)).astype(BF16)
    return hi, mid, lo


def _l0_pre(x2, pos2, g, w_ext, bf_row, qn, wq, kvn, wk, wv, B, S, tm):
    N = B * S
    ns = S // tm
    tri = (jnp.arange(tm)[:, None] >= jnp.arange(tm)[None, :]).astype(BF16)
    n_pairs = FOX_HEADS // 2
    place = jnp.zeros((3, LANES, n_pairs * LANES), F32)
    for hp in range(n_pairs):
        for i in range(3):
            place = place.at[i, 2 * hp, hp * LANES + i].set(-1.0)
            place = place.at[i, 2 * hp + 1, hp * LANES + 3 + i].set(-1.0)
    place = place.astype(BF16)
    lane = jnp.arange(LANES)
    half = MLA_ROPE_DIM // 2
    in_pe = (lane >= MLA_NOPE_DIM) & (lane < MLA_NOPE_DIM + MLA_ROPE_DIM)
    freq = ROPE_THETA ** (-jnp.arange(0, MLA_ROPE_DIM, 2, dtype=F32) / MLA_ROPE_DIM)
    invf = jnp.where(in_pe, freq[(lane - MLA_NOPE_DIM) % half], 0.0)[None, :].astype(F32)
    m_lo = jnp.where(in_pe & (lane < MLA_NOPE_DIM + half), -1.0, 0.0)[None, :].astype(F32)
    m_hi = jnp.where(in_pe & (lane >= MLA_NOPE_DIM + half), 1.0, 0.0)[None, :].astype(F32)

    row = lambda b, s: (b * ns + s, 0)
    tcol = lambda b, s: (b, 0, s)
    full = lambda a: pl.BlockSpec(a.shape, lambda b, s: (0,) * a.ndim)
    fox_w = FOX_HEADS * HEAD_DIM
    vw = MLA_HEADS * MLA_V_DIM
    return pl.pallas_call(
        _l0_pre_kernel,
        grid=(B, ns),
        in_specs=[pl.BlockSpec((tm, D_MODEL), row), pl.BlockSpec((tm, 1), row),
                  full(g), full(w_ext), full(bf_row), full(qn), full(wq), full(kvn), full(wk),
                  full(wv), full(tri), full(place), full(invf), full(m_lo), full(m_hi)],
        out_specs=[pl.BlockSpec((tm, fox_w), row), pl.BlockSpec((tm, 2 * fox_w), row),
                   pl.BlockSpec((1, fox_w, tm), tcol),
                   pl.BlockSpec((tm, MLA_PAD), row), pl.BlockSpec((tm, MLA_PAD), row),
                   pl.BlockSpec((1, vw, tm), tcol),
                   pl.BlockSpec((1, FOX_HEADS, tm), tcol)],
        out_shape=[jax.ShapeDtypeStruct((N, fox_w), BF16),
                   jax.ShapeDtypeStruct((N, 2 * fox_w), BF16),
                   jax.ShapeDtypeStruct((B, fox_w, S), BF16),
                   jax.ShapeDtypeStruct((N, MLA_PAD), BF16),
                   jax.ShapeDtypeStruct((N, MLA_PAD), BF16),
                   jax.ShapeDtypeStruct((B, vw, S), BF16),
                   jax.ShapeDtypeStruct((B, FOX_HEADS, S), F32)],
        scratch_shapes=[pltpu.VMEM((1, LANES), F32)],
        compiler_params=_params("arbitrary", "arbitrary"),
        name="l0_pre",
    )(x2, pos2, g, w_ext, bf_row, qn, wq, kvn, wk, wv, tri, place, invf, m_lo, m_hi)


def _attn_kernel(*refs, fox, tq, tk, tw):
    if fox:
        q_ref, k_ref, vt_ref, cq_ref, ones_ref, o_ref, m_ref, l_ref, acc_ref = refs
    else:
        q_ref, k_ref, vt_ref, o_ref, m_ref, l_ref, acc_ref = refs
    qi = pl.program_id(2)
    n_strips = tq // tw

    q = q_ref[0]
    if fox:
        lane = lax.broadcasted_iota(jnp.int32, (1, LANES), 1)
        zero = jnp.zeros_like(q)
        qa = (jnp.concatenate([jnp.where(lane < HEAD_DIM, q, zero), ones_ref[0]], axis=1),
              jnp.concatenate([jnp.where(lane < HEAD_DIM, zero, q), ones_ref[1]], axis=1))
    else:
        qa = (q[:, :LANES], q[:, LANES:])

    m_ref[...] = jnp.full_like(m_ref, MASK_VALUE)
    l_ref[...] = jnp.zeros_like(l_ref)
    acc_ref[...] = jnp.zeros_like(acc_ref)

    sub = LANES

    def chunk(start, strips):
        units = [(i, r, n_sub, mask_from) for i in range(2) for (r, n_sub, mask_from) in strips]
        kc = k_ref[0, pl.ds(start, tk), :]
        logits = []
        for i, r, n_sub, _ in units:
            kh = kc[:n_sub * sub] if fox else kc[:n_sub * sub, i * LANES:(i + 1) * LANES]
            logits.append(_dot_nt(kh, qa[i][r * tw:(r + 1) * tw]))
        stats = []
        for (i, r, n_sub, mask_from), s in zip(units, logits):
            tiles = []
            for c in range(n_sub):
                t = s[c * sub:(c + 1) * sub]
                if mask_from is not None and c >= mask_from:
                    keys = start + c * sub + lax.broadcasted_iota(jnp.int32, (sub, tw), 0)
                    qpos = qi * tq + r * tw + lax.broadcasted_iota(jnp.int32, (sub, tw), 1)
                    t = jnp.where(keys <= qpos, t, MASK_VALUE)
                tiles.append(t)
            col_max = tiles[0]
            for t in tiles[1:]:
                col_max = jnp.maximum(col_max, t)
            col_max = jnp.max(col_max, axis=0, keepdims=True)
            idx = i * n_strips + r
            m_prev = m_ref[idx]
            if fox:
                cq = cq_ref[0, 0, i:i + 1, r * tw:(r + 1) * tw]
                m_new = jnp.maximum(m_prev, col_max + cq)
                shift = m_new - cq
            else:
                m_new = jnp.maximum(m_prev, col_max)
                shift = m_new
            m_ref[idx] = m_new
            stats.append((tiles, jnp.exp2(m_prev - m_new), shift))
        for (i, r, n_sub, _), (tiles, alpha, shift) in zip(units, stats):
            idx = i * n_strips + r
            psum = None
            probs = []
            for t in tiles:
                p = jnp.exp2(t - shift)
                psum = p if psum is None else psum + p
                probs.append(p.astype(BF16))
            l_ref[idx] = alpha * l_ref[idx] + jnp.sum(psum, axis=0, keepdims=True)
            vt = vt_ref[0, i * HEAD_DIM:(i + 1) * HEAD_DIM, pl.ds(start, n_sub * sub)]
            rows = slice(i * HEAD_DIM, (i + 1) * HEAD_DIM)
            cols = slice(r * tw, (r + 1) * tw)
            acc_ref[rows, cols] = (acc_ref[rows, cols] * alpha
                                   + _dot(vt, jnp.concatenate(probs, axis=0)))

    n_sub_full = tk // sub

    def full_chunk(j, carry):
        chunk(pl.multiple_of(j * tk, tk), [(r, n_sub_full, None) for r in range(n_strips)])
        return carry

    lax.fori_loop(0, qi * (tq // tk), full_chunk, 0)
    per = tw // sub
    for d in range(tq // tk):
        strips = []
        for r in range(n_strips):
            n_sub = min(max((r + 1) * per - d * n_sub_full, 0), n_sub_full)
            if n_sub > 0:
                strips.append((r, n_sub, max(r * per - d * n_sub_full, 0)))
        chunk(pl.multiple_of(qi * tq + d * tk, tk), strips)

    heads = []
    for i in range(2):
        rows = slice(i * HEAD_DIM, (i + 1) * HEAD_DIM)
        heads.append(jnp.concatenate(
            [acc_ref[rows, r * tw:(r + 1) * tw] * (1.0 / l_ref[i * n_strips + r])
             for r in range(n_strips)], axis=1))
    o_ref[0] = jnp.concatenate(heads, axis=0).T.astype(o_ref.dtype)


def _attention(q, k, vt, ct, B, S, tq, tk, tw, fox):
    n_pairs = FOX_HEADS // 2
    nq = S // tq
    qw = LANES if fox else 2 * LANES
    in_specs = [pl.BlockSpec((1, tq, qw), lambda b, h, i: (b, i, h)),
                pl.BlockSpec((1, S, 2 * LANES), lambda b, h, i: (b, 0, h)),
                pl.BlockSpec((1, LANES, S), lambda b, h, i: (b, h, 0))]
    args = [q, k, vt]
    if fox:
        in_specs.append(pl.BlockSpec((1, 1, 2, tq), lambda b, h, i: (b, h, 0, i)))
        args.append(ct)
        lane = jnp.arange(LANES)
        ones = jnp.stack([lane < 3, (lane >= 3) & (lane < 6)]).astype(BF16)
        in_specs.append(pl.BlockSpec((2, tq, LANES), lambda b, h, i: (0, 0, 0)))
        args.append(jnp.broadcast_to(ones[:, None, :], (2, tq, LANES)))
    n_strips = tq // tw
    return pl.pallas_call(
        functools.partial(_attn_kernel, fox=fox, tq=tq, tk=tk, tw=tw),
        grid=(B, n_pairs, nq),
        in_specs=in_specs,
        out_specs=pl.BlockSpec((1, tq, LANES), lambda b, h, i: (b, i, h)),
        out_shape=jax.ShapeDtypeStruct((B, S, n_pairs * LANES), BF16),
        scratch_shapes=[pltpu.VMEM((2 * n_strips, 1, tw), F32),
                        pltpu.VMEM((2 * n_strips, 1, tw), F32),
                        pltpu.VMEM((LANES, tq), F32)],
        compiler_params=_params("arbitrary", "arbitrary", "arbitrary"),
        name="fox_attn" if fox else "mla_attn",
    )(*args)


def _ffn_kernel(*refs, routed, final):
    it = iter(refs)
    h_ref, a_ref, b_ref, wa_ref, wb_ref, g_ref = (next(it) for _ in range(6))
    wr_ref = next(it) if routed else None
    wg_ref, wu_ref, wd_ref = next(it), next(it), next(it)
    fn_ref = next(it) if final else None
    o_ref, hres_ref, xn_ref, acc_ref = next(it), next(it), next(it), next(it)
    gates_ref = next(it) if routed else None
    gcol_ref = next(it) if routed else None
    e = pl.program_id(1)
    f = pl.program_id(2)
    first_step = (e == 0) & (f == 0)
    last_step = (e == pl.num_programs(1) - 1) & (f == pl.num_programs(2) - 1)

    @pl.when(first_step)
    def _():
        hres = h_ref[...] + _dot(a_ref[...], wa_ref[...]) + _dot(b_ref[...], wb_ref[...])
        hres_ref[...] = hres
        xn = _rms(hres, g_ref[...])
        xn_ref[...] = xn.astype(BF16)
        acc_ref[...] = jnp.zeros_like(acc_ref)
        if routed:
            xh = xn.astype(BF16)
            xl = (xn - xh.astype(F32)).astype(BF16)
            wr = wr_ref[...]
            wh = wr.astype(BF16)
            wl = (wr - wh.astype(F32)).astype(BF16)
            logits = _dot(xh, wh) + (_dot(xh, wl) + _dot(xl, wh))
            lane = lax.broadcasted_iota(jnp.int32, logits.shape, 1)
            logits = jnp.where(lane < N_EXPERTS, logits, MASK_VALUE)
            m1 = jnp.max(logits, axis=1, keepdims=True)
            i1 = jnp.min(jnp.where(logits == m1, lane, LANES), axis=1, keepdims=True)
            rest = jnp.where(lane == i1, MASK_VALUE, logits)
            m2 = jnp.max(rest, axis=1, keepdims=True)
            i2 = jnp.min(jnp.where(rest == m2, lane, LANES), axis=1, keepdims=True)
            e2 = jnp.exp(m2 - m1)
            w1 = 1.0 / (1.0 + e2)
            w2 = e2 / (1.0 + e2)
            gates_ref[...] = jnp.where(lane == i1, w1, 0.0) + jnp.where(lane == i2, w2, 0.0)

    if routed:
        @pl.when(f == 0)
        def _():
            gts = gates_ref[...]
            lane = lax.broadcasted_iota(jnp.int32, gts.shape, 1)
            gcol_ref[...] = jnp.sum(jnp.where(lane == e, gts, 0.0), axis=1, keepdims=True)

    xn = xn_ref[...]
    gate = _dot(xn, wg_ref[0])
    up = _dot(xn, wu_ref[0])
    act = gate * (1.0 / (1.0 + jnp.exp(-gate))) * up
    if routed:
        act = act * gcol_ref[...]
    acc_ref[...] += _dot(act.astype(BF16), wd_ref[0])

    @pl.when(last_step)
    def _():
        out = hres_ref[...] + acc_ref[...]
        if final:
            out = _rms(out, fn_ref[...])
        o_ref[...] = out


def _ffn(h, a, b, wa, wb, g, w_router, w_gate, w_up, w_down, final_g, tm, tf):
    N = h.shape[0]
    n_exp = w_gate.shape[0]
    routed = w_router is not None
    final = final_g is not None
    nf = D_FF // tf
    row = lambda i, e, f: (i, 0)
    const = lambda i, e, f: (0, 0)
    full = lambda x: pl.BlockSpec(x.shape, const)
    half_w = a.shape[1]
    in_specs = [pl.BlockSpec((tm, D_MODEL), row), pl.BlockSpec((tm, half_w), row),
                pl.BlockSpec((tm, half_w), row), full(wa), full(wb), full(g)]
    args = [h, a, b, wa, wb, g]
    if routed:
        in_specs.append(full(w_router))
        args.append(w_router)
    in_specs += [pl.BlockSpec((1, D_MODEL, tf), lambda i, e, f: (e, 0, f)),
                 pl.BlockSpec((1, D_MODEL, tf), lambda i, e, f: (e, 0, f)),
                 pl.BlockSpec((1, tf, D_MODEL), lambda i, e, f: (e, f, 0))]
    args += [w_gate, w_up, w_down]
    if final:
        in_specs.append(full(final_g))
        args.append(final_g)
    scratch = [pltpu.VMEM((tm, D_MODEL), F32), pltpu.VMEM((tm, D_MODEL), BF16),
               pltpu.VMEM((tm, D_MODEL), F32)]
    if routed:
        scratch += [pltpu.VMEM((tm, LANES), F32), pltpu.VMEM((tm, 1), F32)]
    return pl.pallas_call(
        functools.partial(_ffn_kernel, routed=routed, final=final),
        grid=(N // tm, n_exp, nf),
        in_specs=in_specs,
        out_specs=pl.BlockSpec((tm, D_MODEL), row),
        out_shape=jax.ShapeDtypeStruct((N, D_MODEL), F32),
        scratch_shapes=scratch,
        compiler_params=_params("arbitrary", "arbitrary", "arbitrary"),
        name="moe_ffn" if routed else "dense_ffn",
    )(*args)


def _l1_pre_kernel(x_ref, pos_ref, g_ref, w_ref, cw_ref, invf_ref, mlo_ref, mhi_ref,
                   q_ref, kv_ref, oc_ref, tail_ref):
    @pl.when(pl.program_id(1) == 0)
    def _():
        tail_ref[...] = jnp.zeros_like(tail_ref)

    xn = _rms(x_ref[...], g_ref[...]).astype(BF16)
    y = _dot(xn, w_ref[...])
    ang = pos_ref[...].astype(F32) * invf_ref[...]
    cos = jnp.cos(ang)
    sin = jnp.sin(ang)
    sin_lo = sin * mlo_ref[...]
    sin_hi = sin * mhi_ref[...]
    half = HEAD_DIM // 2
    qw = SWA_HEADS * HEAD_DIM
    for j in range(qw // LANES):
        sl = slice(j * LANES, (j + 1) * LANES)
        q_ref[:, sl] = _rope(y[:, sl], cos, sin_lo, sin_hi, half).astype(BF16)
    kv_ref[:, :LANES] = _rope(y[:, qw:qw + LANES], cos, sin_lo, sin_hi, half).astype(BF16)
    kv_ref[:, LANES:] = y[:, qw + LANES:qw + 2 * LANES].astype(BF16)

    o = qw + 2 * LANES
    C = CONV_CHANNELS
    gate_b = y[:, o:o + C]
    u = y[:, o + C:o + 2 * C] * y[:, o + 2 * C:o + 3 * C]
    tm = u.shape[0]
    rows = lax.broadcasted_iota(jnp.int32, u.shape, 0)
    tail = tail_ref[...]
    u1 = jnp.where(rows == 0, tail[7:8, :], pltpu.roll(u, 1, 0))
    u2 = jnp.where(rows == 0, tail[6:7, :], jnp.where(rows == 1, tail[7:8, :], pltpu.roll(u, 2, 0)))
    cw = cw_ref[...]
    oc_ref[...] = (gate_b * (cw[0:1, :] * u2 + cw[1:2, :] * u1 + cw[2:3, :] * u)).astype(BF16)
    tail_ref[...] = u[tm - 8:tm, :]


def _l1_pre(x2, pos2, g, w_in, conv_w, B, S, tm):
    N = B * S
    ns = S // tm
    lane = jnp.arange(LANES)
    half = HEAD_DIM // 2
    freq = ROPE_THETA ** (-jnp.arange(0, HEAD_DIM, 2, dtype=F32) / HEAD_DIM)
    invf = freq[lane % half][None, :].astype(F32)
    m_lo = jnp.where(lane % HEAD_DIM < half, -1.0, 0.0)[None, :].astype(F32)
    m_hi = jnp.where(lane % HEAD_DIM >= half, 1.0, 0.0)[None, :].astype(F32)
    row = lambda b, s: (b * ns + s, 0)
    const = lambda b, s: (0, 0)
    full = lambda a: pl.BlockSpec(a.shape, const)
    return pl.pallas_call(
        _l1_pre_kernel,
        grid=(B, ns),
        in_specs=[pl.BlockSpec((tm, D_MODEL), row), pl.BlockSpec((tm, 1), row),
                  full(g), full(w_in), full(conv_w), full(invf), full(m_lo), full(m_hi)],
        out_specs=[pl.BlockSpec((tm, SWA_HEADS * HEAD_DIM), row),
                   pl.BlockSpec((tm, 2 * LANES), row),
                   pl.BlockSpec((tm, CONV_CHANNELS), row)],
        out_shape=[jax.ShapeDtypeStruct((N, SWA_HEADS * HEAD_DIM), BF16),
                   jax.ShapeDtypeStruct((N, 2 * LANES), BF16),
                   jax.ShapeDtypeStruct((N, CONV_CHANNELS), BF16)],
        scratch_shapes=[pltpu.VMEM((8, CONV_CHANNELS), F32)],
        compiler_params=_params("arbitrary", "arbitrary"),
        name="l1_pre",
    )(x2, pos2, g, w_in, conv_w, invf, m_lo, m_hi)


def _swa_kernel(sink_ref, q_ref, kv_ref, kvp_ref, o_ref, *, tq):
    s_idx = pl.program_id(1)
    first = lax.broadcasted_iota(jnp.int32, (1, LANES), 1) < HEAD_DIM
    G = SWA_HEADS // SWA_KV_HEADS
    W = WINDOW

    def dup(x):
        r = pltpu.roll(x, HEAD_DIM, 1)
        return (jnp.where(first, x, r), jnp.where(first, r, x))

    kv = kv_ref[0]
    kvp = kvp_ref[0]
    k_all = jnp.concatenate([kvp[:, :LANES], kv[:, :LANES]], axis=0)
    v_all = jnp.concatenate([kvp[:, LANES:], kv[:, LANES:]], axis=0)
    k_dup = dup(k_all)
    v_dup = dup(v_all)
    q = q_ref[0]
    zero = jnp.zeros((W, LANES), q.dtype)
    rel = (lax.broadcasted_iota(jnp.int32, (W, 2 * W), 0) + W
           - lax.broadcasted_iota(jnp.int32, (W, 2 * W), 1))
    in_window = (rel >= 0) & (rel < W)
    for r in range(tq // W):
        key_pos = (s_idx * tq + r * W - W
                   + lax.broadcasted_iota(jnp.int32, (W, 2 * W), 1))
        keep = in_window & (key_pos >= 0)
        keep4 = jnp.concatenate([keep] * (2 * (G // 2)), axis=0)
        for g in range(SWA_KV_HEADS):
            kb = k_dup[g][r * W:r * W + 2 * W]
            vb = v_dup[g][r * W:r * W + 2 * W]
            parts = []
            sinks = []
            for j in range(G // 2):
                blk = g * (G // 2) + j
                q2 = q[r * W:(r + 1) * W, blk * LANES:(blk + 1) * LANES]
                parts += [jnp.where(first, q2, zero), jnp.where(first, zero, q2)]
                sinks += [jnp.full((W, 1), sink_ref[2 * blk], F32),
                          jnp.full((W, 1), sink_ref[2 * blk + 1], F32)]
            qst = jnp.concatenate(parts, axis=0)
            sink = jnp.concatenate(sinks, axis=0)
            s = _dot_nt(qst, kb)
            s = jnp.where(keep4, s, MASK_VALUE)
            m = jnp.maximum(jnp.max(s, axis=1, keepdims=True), sink)
            e = jnp.exp(s - m)
            den = jnp.sum(e, axis=1, keepdims=True) + jnp.exp(sink - m)
            o = _dot((e * (1.0 / den)).astype(BF16), vb)
            for j in range(G // 2):
                blk = g * (G // 2) + j
                oa = o[(2 * j) * W:(2 * j + 1) * W]
                ob = o[(2 * j + 1) * W:(2 * j + 2) * W]
                o_ref[0, r * W:(r + 1) * W, blk * LANES:(blk + 1) * LANES] = (
                    jnp.where(first, oa, ob).astype(o_ref.dtype))


def _swa(sinks, q, kv, B, S, tq):
    ns = S // tq
    per = tq // WINDOW
    return pl.pallas_call(
        functools.partial(_swa_kernel, tq=tq),
        grid=(B, ns),
        in_specs=[pl.BlockSpec(memory_space=pltpu.SMEM),
                  pl.BlockSpec((1, tq, SWA_HEADS * HEAD_DIM), lambda b, s: (b, s, 0)),
                  pl.BlockSpec((1, tq, 2 * LANES), lambda b, s: (b, s, 0)),
                  pl.BlockSpec((1, WINDOW, 2 * LANES),
                               lambda b, s: (b, jnp.maximum(s * per - 1, 0), 0))],
        out_specs=pl.BlockSpec((1, tq, SWA_HEADS * HEAD_DIM), lambda b, s: (b, s, 0)),
        out_shape=jax.ShapeDtypeStruct((B, S, SWA_HEADS * HEAD_DIM), BF16),
        compiler_params=_params("arbitrary", "arbitrary"),
        name="swa_attn",
    )(sinks, q, kv, kv)


def _pad_heads(w, n_heads, width, offset=0):
    K = w.shape[0]
    w3 = w.reshape(K, n_heads, width)
    out = jnp.zeros((K, n_heads, LANES), w.dtype).at[:, :, offset:offset + width].set(w3)
    return out.reshape(K, n_heads * LANES)


def kernel(x, positions, l0_norm_mix, l0_w_in, l0_b_forget, l0_q_norm, l0_w_q_up, l0_kv_norm,
           l0_w_kv_up, l0_w_out, l0_norm_ffn, l0_w_gate, l0_w_up, l0_w_down,
           l1_norm_mix, l1_w_in, l1_sinks, l1_conv_w, l1_w_out, l1_norm_ffn, l1_w_router,
           l1_w_gate, l1_w_up, l1_w_down, final_norm):
    B, S, D = x.shape
    assert D == D_MODEL and S % 512 == 0
    N = B * S
    tm = 512
    x2 = x.reshape(N, D)
    pos2 = positions.reshape(N, 1)
    rowv = lambda v: v.reshape(1, -1).astype(F32)

    fox_w = FOX_HEADS * HEAD_DIM
    o = 3 * fox_w
    w_fq = l0_w_in[:, :fox_w] * (HEAD_DIM ** -0.5 * LOG2E)
    w_f = l0_w_in[:, o:o + FOX_HEADS]
    o += FOX_HEADS
    w_cq = l0_w_in[:, o:o + MLA_Q_RANK]
    o += MLA_Q_RANK
    w_ckv = l0_w_in[:, o:o + MLA_KV_RANK]
    o += MLA_KV_RANK
    w_kr = l0_w_in[:, o:o + MLA_ROPE_DIM]
    w_ext = jnp.concatenate(
        [w_fq, l0_w_in[:, fox_w:3 * fox_w], w_cq, w_ckv,
         _pad_heads(w_kr, 1, MLA_ROPE_DIM, MLA_NOPE_DIM), _pad_heads(w_f, 1, FOX_HEADS)],
        axis=1).astype(BF16)
    bf_row = jnp.zeros((1, LANES), F32).at[0, :FOX_HEADS].set(l0_b_forget.astype(F32))
    qk_dim = MLA_NOPE_DIM + MLA_ROPE_DIM
    wq = _pad_heads(l0_w_q_up * (qk_dim ** -0.5 * LOG2E), MLA_HEADS, qk_dim).astype(BF16)
    kv3 = l0_w_kv_up.reshape(MLA_KV_RANK, MLA_HEADS, MLA_NOPE_DIM + MLA_V_DIM)
    wk = _pad_heads(kv3[:, :, :MLA_NOPE_DIM].reshape(MLA_KV_RANK, -1), MLA_HEADS,
                    MLA_NOPE_DIM).astype(BF16)
    wv = kv3[:, :, MLA_NOPE_DIM:].reshape(MLA_KV_RANK, -1).astype(BF16)

    q_f, k_f, vt_f, q_m, k_m, vt_m, ct = _l0_pre(
        x2, pos2, rowv(l0_norm_mix), w_ext, bf_row, rowv(l0_q_norm), wq, rowv(l0_kv_norm),
        wk, wv, B, S, tm)

    tq, tk, tw = ATTN_TQ, ATTN_TK, ATTN_TW
    o_fox = _attention(q_f.reshape(B, S, fox_w), k_f.reshape(B, S, 2 * fox_w), vt_f,
                       ct.reshape(B, FOX_HEADS // 2, 2, S), B, S, tq, tk, tw, True)
    o_mla = _attention(q_m.reshape(B, S, MLA_PAD), k_m.reshape(B, S, MLA_PAD), vt_m, None,
                       B, S, tq, tk, tw, False)

    w_out0 = l0_w_out.astype(BF16)
    h1 = _ffn(x2, o_fox.reshape(N, -1), o_mla.reshape(N, -1), w_out0[:fox_w], w_out0[fox_w:],
              rowv(l0_norm_ffn), None, l0_w_gate.astype(BF16)[None], l0_w_up.astype(BF16)[None],
              l0_w_down.astype(BF16)[None], None, FFN_TM, FFN_TF)

    swa_w = SWA_HEADS * HEAD_DIM
    w_in1 = jnp.concatenate([l1_w_in[:, :swa_w] * (HEAD_DIM ** -0.5), l1_w_in[:, swa_w:]],
                            axis=1).astype(BF16)
    q_s, kv_s, o_conv = _l1_pre(h1, pos2, rowv(l1_norm_mix), w_in1, l1_conv_w.astype(F32),
                                B, S, tm)
    o_swa = _swa(l1_sinks.astype(F32), q_s.reshape(B, S, swa_w), kv_s.reshape(B, S, 2 * LANES),
                 B, S, 512)
    w_out1 = l1_w_out.astype(BF16)
    w_router = jnp.zeros((D_MODEL, LANES), F32).at[:, :N_EXPERTS].set(l1_w_router.astype(F32))
    out = _ffn(h1, o_swa.reshape(N, -1), o_conv, w_out1[:swa_w], w_out1[swa_w:],
               rowv(l1_norm_ffn), w_router, l1_w_gate.astype(BF16), l1_w_up.astype(BF16),
               l1_w_down.astype(BF16), rowv(final_norm), FFN_TM, FFN_TF)
    return out.reshape(B, S, D)
```

```python
import functools

import jax
import jax.numpy as jnp
from jax import lax
from jax.experimental import pallas as pl
from jax.experimental.pallas import tpu as pltpu

F32 = jnp.float32
BF16 = jnp.bfloat16

D_MODEL = 1024
HEAD_DIM = 64
RMS_EPS = 1e-6
ROPE_THETA = 10000.0
FOX_HEADS = 8
MLA_HEADS = 8
MLA_Q_RANK = 384
MLA_KV_RANK = 256
MLA_NOPE_DIM = 64
MLA_ROPE_DIM = 32
MLA_V_DIM = 64
SWA_HEADS = 8
SWA_KV_HEADS = 2
WINDOW = 128
CONV_CHANNELS = 512
CONV_WIDTH = 3
D_FF = 3584
N_EXPERTS = 8

LANES = 128
MASK_VALUE = -1e30
LOG2E = 1.4426950408889634
VMEM_LIMIT = 56 * 1024 * 1024

QKV_W = 3 * FOX_HEADS * HEAD_DIM
MLA_PAD = MLA_HEADS * LANES
L0_COLS = QKV_W + MLA_Q_RANK + MLA_KV_RANK + 2 * LANES
ATTN_TQ = 512
ATTN_TK = 512
ATTN_TW = 256
MOE_TM = 512
MOE_TR = 512
FFN_TM = 1024
FFN_TF = 512
L1_COLS = SWA_HEADS * HEAD_DIM + 2 * SWA_KV_HEADS * HEAD_DIM + 3 * CONV_CHANNELS


def _params(*sem):
    return pltpu.CompilerParams(dimension_semantics=sem, vmem_limit_bytes=VMEM_LIMIT)


def _rms(x, g):
    return x * lax.rsqrt(jnp.mean(x * x, axis=-1, keepdims=True) + RMS_EPS) * g


def _dot(a, b):
    return jnp.dot(a, b, preferred_element_type=F32)


def _dot_nt(a, b):
    return lax.dot_general(a, b, (((1,), (1,)), ((), ())), preferred_element_type=F32)


def _rope(x, cos, sin_lo, sin_hi, half):
    return (x * cos + pltpu.roll(x, half, 1) * sin_hi
            + pltpu.roll(x, LANES - half, 1) * sin_lo)


def _l0_pre_kernel(x_ref, pos_ref, g_ref, w_ref, bf_ref, qn_ref, wq_ref, kvn_ref, wk_ref,
                   wv_ref, tri_ref, place_ref, invf_ref, mlo_ref, mhi_ref,
                   qf_ref, kf_ref, vtf_ref, q_ref, k_ref, vt_ref, ct_ref, carry_ref):
    @pl.when(pl.program_id(1) == 0)
    def _():
        carry_ref[...] = jnp.zeros_like(carry_ref)

    fox_w = FOX_HEADS * HEAD_DIM
    xn = _rms(x_ref[...], g_ref[...]).astype(BF16)
    y = _dot(xn, w_ref[...])
    qf_ref[...] = y[:, :fox_w].astype(BF16)
    vtf_ref[0] = y[:, 2 * fox_w:3 * fox_w].T.astype(BF16)
    o = QKV_W
    cq = y[:, o:o + MLA_Q_RANK]
    o += MLA_Q_RANK
    ckv = y[:, o:o + MLA_KV_RANK]
    o += MLA_KV_RANK
    kr = y[:, o:o + LANES]
    fl = y[:, o + LANES:o + 2 * LANES]

    q = _dot(_rms(cq, qn_ref[...]).astype(BF16), wq_ref[...])
    ckn = _rms(ckv, kvn_ref[...]).astype(BF16)
    kk = _dot(ckn, wk_ref[...])
    vt_ref[0] = _dot(ckn, wv_ref[...]).T.astype(BF16)

    ang = pos_ref[...].astype(F32) * invf_ref[...]
    cos = jnp.cos(ang)
    sin = jnp.sin(ang)
    sin_lo = sin * mlo_ref[...]
    sin_hi = sin * mhi_ref[...]
    half = MLA_ROPE_DIM // 2
    kpe = _rope(kr, cos, sin_lo, sin_hi, half)
    for h in range(MLA_HEADS):
        sl = slice(h * LANES, (h + 1) * LANES)
        q_ref[:, sl] = _rope(q[:, sl], cos, sin_lo, sin_hi, half).astype(BF16)
        k_ref[:, sl] = (kk[:, sl] + kpe).astype(BF16)

    z = fl + bf_ref[...]
    ls = jnp.minimum(z, 0.0) - jnp.log1p(jnp.exp(-jnp.abs(z)))
    tri = tri_ref[...]
    c = carry_ref[...]
    for part in _split3(ls):
        c = c + _dot(tri, part)
    tm = c.shape[0]
    carry_ref[...] = c[tm - 1:tm, :]
    c2 = c * LOG2E
    ct_ref[0] = c2.T[:FOX_HEADS, :]
    bias = None
    for i, part in enumerate(_split3(c2)):
        d = _dot(part, place_ref[i])
        bias = d if bias is None else bias + d
    for hp in range(FOX_HEADS // 2):
        kf_ref[:, 2 * hp * LANES:(2 * hp + 1) * LANES] = (
            y[:, fox_w + hp * LANES:fox_w + (hp + 1) * LANES].astype(BF16))
        kf_ref[:, (2 * hp + 1) * LANES:(2 * hp + 2) * LANES] = (
            bias[:, hp * LANES:(hp + 1) * LANES].astype(BF16))


def _split3(x):
    hi = x.astype(BF16)
    r1 = x - hi.astype(F32)
    mid = r1.astype(BF16)
    lo = (r1 - mid.astype(F32)).astype(BF16)
    return hi, mid, lo


def _l0_pre(x2, pos2, g, w_ext, bf_row, qn, wq, kvn, wk, wv, B, S, tm):
    N = B * S
    ns = S // tm
    tri = (jnp.arange(tm)[:, None] >= jnp.arange(tm)[None, :]).astype(BF16)
    n_pairs = FOX_HEADS // 2
    place = jnp.zeros((3, LANES, n_pairs * LANES), F32)
    for hp in range(n_pairs):
        for i in range(3):
            place = place.at[i, 2 * hp, hp * LANES + i].set(-1.0)
            place = place.at[i, 2 * hp + 1, hp * LANES + 3 + i].set(-1.0)
    place = place.astype(BF16)
    lane = jnp.arange(LANES)
    half = MLA_ROPE_DIM // 2
    in_pe = (lane >= MLA_NOPE_DIM) & (lane < MLA_NOPE_DIM + MLA_ROPE_DIM)
    freq = ROPE_THETA ** (-jnp.arange(0, MLA_ROPE_DIM, 2, dtype=F32) / MLA_ROPE_DIM)
    invf = jnp.where(in_pe, freq[(lane - MLA_NOPE_DIM) % half], 0.0)[None, :].astype(F32)
    m_lo = jnp.where(in_pe & (lane < MLA_NOPE_DIM + half), -1.0, 0.0)[None, :].astype(F32)
    m_hi = jnp.where(in_pe & (lane >= MLA_NOPE_DIM + half), 1.0, 0.0)[None, :].astype(F32)

    row = lambda b, s: (b * ns + s, 0)
    tcol = lambda b, s: (b, 0, s)
    full = lambda a: pl.BlockSpec(a.shape, lambda b, s: (0,) * a.ndim)
    fox_w = FOX_HEADS * HEAD_DIM
    vw = MLA_HEADS * MLA_V_DIM
    return pl.pallas_call(
        _l0_pre_kernel,
        grid=(B, ns),
        in_specs=[pl.BlockSpec((tm, D_MODEL), row), pl.BlockSpec((tm, 1), row),
                  full(g), full(w_ext), full(bf_row), full(qn), full(wq), full(kvn), full(wk),
                  full(wv), full(tri), full(place), full(invf), full(m_lo), full(m_hi)],
        out_specs=[pl.BlockSpec((tm, fox_w), row), pl.BlockSpec((tm, 2 * fox_w), row),
                   pl.BlockSpec((1, fox_w, tm), tcol),
                   pl.BlockSpec((tm, MLA_PAD), row), pl.BlockSpec((tm, MLA_PAD), row),
                   pl.BlockSpec((1, vw, tm), tcol),
                   pl.BlockSpec((1, FOX_HEADS, tm), tcol)],
        out_shape=[jax.ShapeDtypeStruct((N, fox_w), BF16),
                   jax.ShapeDtypeStruct((N, 2 * fox_w), BF16),
                   jax.ShapeDtypeStruct((B, fox_w, S), BF16),
                   jax.ShapeDtypeStruct((N, MLA_PAD), BF16),
                   jax.ShapeDtypeStruct((N, MLA_PAD), BF16),
                   jax.ShapeDtypeStruct((B, vw, S), BF16),
                   jax.ShapeDtypeStruct((B, FOX_HEADS, S), F32)],
        scratch_shapes=[pltpu.VMEM((1, LANES), F32)],
        compiler_params=_params("arbitrary", "arbitrary"),
        name="l0_pre",
    )(x2, pos2, g, w_ext, bf_row, qn, wq, kvn, wk, wv, tri, place, invf, m_lo, m_hi)


def _attn_kernel(*refs, fox, tq, tk, tw):
    if fox:
        q_ref, k_ref, vt_ref, cq_ref, ones_ref, o_ref, m_ref, l_ref, acc_ref = refs
    else:
        q_ref, k_ref, vt_ref, o_ref, m_ref, l_ref, acc_ref = refs
    qi = pl.program_id(2)
    n_strips = tq // tw

    q = q_ref[0]
    if fox:
        lane = lax.broadcasted_iota(jnp.int32, (1, LANES), 1)
        zero = jnp.zeros_like(q)
        qa = (jnp.concatenate([jnp.where(lane < HEAD_DIM, q, zero), ones_ref[0]], axis=1),
              jnp.concatenate([jnp.where(lane < HEAD_DIM, zero, q), ones_ref[1]], axis=1))
    else:
        qa = (q[:, :LANES], q[:, LANES:])

    m_ref[...] = jnp.full_like(m_ref, MASK_VALUE)
    l_ref[...] = jnp.zeros_like(l_ref)
    acc_ref[...] = jnp.zeros_like(acc_ref)

    sub = LANES

    def chunk(start, strips):
        units = [(i, r, n_sub, mask_from) for i in range(2) for (r, n_sub, mask_from) in strips]
        kc = k_ref[0, pl.ds(start, tk), :]
        logits = []
        for i, r, n_sub, _ in units:
            kh = kc[:n_sub * sub] if fox else kc[:n_sub * sub, i * LANES:(i + 1) * LANES]
            logits.append(_dot_nt(kh, qa[i][r * tw:(r + 1) * tw]))
        stats = []
        for (i, r, n_sub, mask_from), s in zip(units, logits):
            tiles = []
            for c in range(n_sub):
                t = s[c * sub:(c + 1) * sub]
                if mask_from is not None and c >= mask_from:
                    keys = start + c * sub + lax.broadcasted_iota(jnp.int32, (sub, tw), 0)
                    qpos = qi * tq + r * tw + lax.broadcasted_iota(jnp.int32, (sub, tw), 1)
                    t = jnp.where(keys <= qpos, t, MASK_VALUE)
                tiles.append(t)
            col_max = tiles[0]
            for t in tiles[1:]:
                col_max = jnp.maximum(col_max, t)
            col_max = jnp.max(col_max, axis=0, keepdims=True)
            idx = i * n_strips + r
            m_prev = m_ref[idx]
            if fox:
                cq = cq_ref[0, 0, i:i + 1, r * tw:(r + 1) * tw]
                m_new = jnp.maximum(m_prev, col_max + cq)
                shift = m_new - cq
            else:
                m_new = jnp.maximum(m_prev, col_max)
                shift = m_new
            m_ref[idx] = m_new
            stats.append((tiles, jnp.exp2(m_prev - m_new), shift))
        for (i, r, n_sub, _), (tiles, alpha, shift) in zip(units, stats):
            idx = i * n_strips + r
            psum = None
            probs = []
            for t in tiles:
                p = jnp.exp2(t - shift)
                psum = p if psum is None else psum + p
                probs.append(p.astype(BF16))
            l_ref[idx] = alpha * l_ref[idx] + jnp.sum(psum, axis=0, keepdims=True)
            vt = vt_ref[0, i * HEAD_DIM:(i + 1) * HEAD_DIM, pl.ds(start, n_sub * sub)]
            rows = slice(i * HEAD_DIM, (i + 1) * HEAD_DIM)
            cols = slice(r * tw, (r + 1) * tw)
            acc_ref[rows, cols] = (acc_ref[rows, cols] * alpha
                                   + _dot(vt, jnp.concatenate(probs, axis=0)))

    n_sub_full = tk // sub

    def full_chunk(j, carry):
        chunk(pl.multiple_of(j * tk, tk), [(r, n_sub_full, None) for r in range(n_strips)])
        return carry

    lax.fori_loop(0, qi * (tq // tk), full_chunk, 0)
    per = tw // sub
    for d in range(tq // tk):
        strips = []
        for r in range(n_strips):
            n_sub = min(max((r + 1) * per - d * n_sub_full, 0), n_sub_full)
            if n_sub > 0:
                strips.append((r, n_sub, max(r * per - d * n_sub_full, 0)))
        chunk(pl.multiple_of(qi * tq + d * tk, tk), strips)

    heads = []
    for i in range(2):
        rows = slice(i * HEAD_DIM, (i + 1) * HEAD_DIM)
        heads.append(jnp.concatenate(
            [acc_ref[rows, r * tw:(r + 1) * tw] * (1.0 / l_ref[i * n_strips + r])
             for r in range(n_strips)], axis=1))
    o_ref[0] = jnp.concatenate(heads, axis=0).T.astype(o_ref.dtype)


def _attention(q, k, vt, ct, B, S, tq, tk, tw, fox):
    n_pairs = FOX_HEADS // 2
    nq = S // tq
    qw = LANES if fox else 2 * LANES
    in_specs = [pl.BlockSpec((1, tq, qw), lambda b, h, i: (b, i, h)),
                pl.BlockSpec((1, S, 2 * LANES), lambda b, h, i: (b, 0, h)),
                pl.BlockSpec((1, LANES, S), lambda b, h, i: (b, h, 0))]
    args = [q, k, vt]
    if fox:
        in_specs.append(pl.BlockSpec((1, 1, 2, tq), lambda b, h, i: (b, h, 0, i)))
        args.append(ct)
        lane = jnp.arange(LANES)
        ones = jnp.stack([lane < 3, (lane >= 3) & (lane < 6)]).astype(BF16)
        in_specs.append(pl.BlockSpec((2, tq, LANES), lambda b, h, i: (0, 0, 0)))
        args.append(jnp.broadcast_to(ones[:, None, :], (2, tq, LANES)))
    n_strips = tq // tw
    return pl.pallas_call(
        functools.partial(_attn_kernel, fox=fox, tq=tq, tk=tk, tw=tw),
        grid=(B, n_pairs, nq),
        in_specs=in_specs,
        out_specs=pl.BlockSpec((1, tq, LANES), lambda b, h, i: (b, i, h)),
        out_shape=jax.ShapeDtypeStruct((B, S, n_pairs * LANES), BF16),
        scratch_shapes=[pltpu.VMEM((2 * n_strips, 1, tw), F32),
                        pltpu.VMEM((2 * n_strips, 1, tw), F32),
                        pltpu.VMEM((LANES, tq), F32)],
        compiler_params=_params("arbitrary", "arbitrary", "arbitrary"),
        name="fox_attn" if fox else "mla_attn",
    )(*args)


def _ffn_kernel(h_ref, a_ref, b_ref, wa_ref, wb_ref, g_ref, wg_ref, wu_ref, wd_ref,
                o_ref, hres_ref, xn_ref, acc_ref):
    f = pl.program_id(1)

    @pl.when(f == 0)
    def _():
        hres = h_ref[...] + _dot(a_ref[...], wa_ref[...]) + _dot(b_ref[...], wb_ref[...])
        hres_ref[...] = hres
        xn_ref[...] = _rms(hres, g_ref[...]).astype(BF16)
        acc_ref[...] = jnp.zeros_like(acc_ref)

    xn = xn_ref[...]
    gate = _dot(xn, wg_ref[...])
    up = _dot(xn, wu_ref[...])
    act = gate * (1.0 / (1.0 + jnp.exp(-gate))) * up
    acc_ref[...] += _dot(act.astype(BF16), wd_ref[...])

    @pl.when(f == pl.num_programs(1) - 1)
    def _():
        o_ref[...] = hres_ref[...] + acc_ref[...]


def _ffn(h, a, b, wa, wb, g, w_gate, w_up, w_down, tm, tf):
    N = h.shape[0]
    row = lambda i, f: (i, 0)
    full = lambda x: pl.BlockSpec(x.shape, lambda i, f: (0, 0))
    half_w = a.shape[1]
    return pl.pallas_call(
        _ffn_kernel,
        grid=(N // tm, D_FF // tf),
        in_specs=[pl.BlockSpec((tm, D_MODEL), row), pl.BlockSpec((tm, half_w), row),
                  pl.BlockSpec((tm, half_w), row), full(wa), full(wb), full(g),
                  pl.BlockSpec((D_MODEL, tf), lambda i, f: (0, f)),
                  pl.BlockSpec((D_MODEL, tf), lambda i, f: (0, f)),
                  pl.BlockSpec((tf, D_MODEL), lambda i, f: (f, 0))],
        out_specs=pl.BlockSpec((tm, D_MODEL), row),
        out_shape=jax.ShapeDtypeStruct((N, D_MODEL), F32),
        scratch_shapes=[pltpu.VMEM((tm, D_MODEL), F32), pltpu.VMEM((tm, D_MODEL), BF16),
                        pltpu.VMEM((tm, D_MODEL), F32)],
        compiler_params=_params("arbitrary", "arbitrary"),
        name="dense_ffn",
    )(h, a, b, wa, wb, g, w_gate, w_up, w_down)


def _router_kernel(h_ref, a_ref, b_ref, wa_ref, wb_ref, g_ref, wr_ref, tri_ref,
                   hres_ref, xn_ref, slab_ref, meta_ref, cnt_ref):
    @pl.when(pl.program_id(0) == 0)
    def _():
        cnt_ref[...] = jnp.zeros_like(cnt_ref)

    hres = h_ref[...] + _dot(a_ref[...], wa_ref[...]) + _dot(b_ref[...], wb_ref[...])
    hres_ref[...] = hres
    xn = _rms(hres, g_ref[...])
    xn_ref[...] = xn
    xh = xn.astype(BF16)
    xl = (xn - xh.astype(F32)).astype(BF16)
    wr = wr_ref[...]
    wh = wr.astype(BF16)
    wl = (wr - wh.astype(F32)).astype(BF16)
    logits = _dot(xh, wh) + (_dot(xh, wl) + _dot(xl, wh))
    lane = lax.broadcasted_iota(jnp.int32, logits.shape, 1)
    logits = jnp.where(lane < N_EXPERTS, logits, MASK_VALUE)
    m1 = jnp.max(logits, axis=1, keepdims=True)
    i1 = jnp.min(jnp.where(logits == m1, lane, LANES), axis=1, keepdims=True)
    rest = jnp.where(lane == i1, MASK_VALUE, logits)
    m2 = jnp.max(rest, axis=1, keepdims=True)
    i2 = jnp.min(jnp.where(rest == m2, lane, LANES), axis=1, keepdims=True)
    e2 = jnp.exp(m2 - m1)
    w1 = 1.0 / (1.0 + e2)
    w2 = e2 / (1.0 + e2)
    hot1 = lane == i1
    hot2 = lane == i2
    onehot = jnp.where(hot1 | hot2, 1.0, 0.0)
    before = _dot(tri_ref[...], onehot.astype(BF16)) + cnt_ref[...]
    r1 = jnp.sum(jnp.where(hot1, before, 0.0), axis=1, keepdims=True)
    r2 = jnp.sum(jnp.where(hot2, before, 0.0), axis=1, keepdims=True)
    cnt_ref[...] += jnp.sum(onehot, axis=0, keepdims=True)
    slab = (jnp.where(lane == 0, i1.astype(F32), 0.0) + jnp.where(lane == 1, i2.astype(F32), 0.0)
            + jnp.where(lane == 2, r1, 0.0) + jnp.where(lane == 3, r2, 0.0)
            + jnp.where(lane == 4, w1, 0.0) + jnp.where(lane == 5, w2, 0.0))
    slab_ref[...] = slab
    meta_ref[...] = slab.T[:8, :]


def _router(h, a, b, wa, wb, g, w_router, tm):
    N = h.shape[0]
    tri = (jnp.arange(tm)[:, None] > jnp.arange(tm)[None, :]).astype(BF16)
    row = lambda i: (i, 0)
    full = lambda x: pl.BlockSpec(x.shape, lambda i: (0, 0))
    half_w = a.shape[1]
    return pl.pallas_call(
        _router_kernel,
        grid=(N // tm,),
        in_specs=[pl.BlockSpec((tm, D_MODEL), row), pl.BlockSpec((tm, half_w), row),
                  pl.BlockSpec((tm, half_w), row), full(wa), full(wb), full(g), full(w_router),
                  full(tri)],
        out_specs=[pl.BlockSpec((tm, D_MODEL), row), pl.BlockSpec((tm, D_MODEL), row),
                   pl.BlockSpec((tm, LANES), row), pl.BlockSpec((8, tm), lambda i: (0, i)),
                   pl.BlockSpec((1, LANES), lambda i: (0, 0))],
        out_shape=[jax.ShapeDtypeStruct((N, D_MODEL), F32), jax.ShapeDtypeStruct((N, D_MODEL), F32),
                   jax.ShapeDtypeStruct((N, LANES), F32), jax.ShapeDtypeStruct((8, N), F32),
                   jax.ShapeDtypeStruct((1, LANES), F32)],
        compiler_params=_params("arbitrary"),
        name="moe_router",
    )(h, a, b, wa, wb, g, w_router, tri)


def _row_copy(src, src_row, dst, dst_row, sem):
    return pltpu.make_async_copy(src.at[pl.ds(src_row, 1), :], dst.at[pl.ds(dst_row, 1), :], sem)


def _dispatch_kernel(pos_hbm, xn_ref, init_hbm, xs_hbm, pos_smem, idx_sem, row_sem, *, tm):
    del init_hbm
    i = pl.program_id(0)
    idx_copy = pltpu.make_async_copy(pos_hbm.at[i], pos_smem, idx_sem)
    idx_copy.start()
    idx_copy.wait()

    def issue(t, carry):
        for k in range(2):
            _row_copy(xn_ref, t, xs_hbm, pos_smem[k * tm + t], row_sem).start()
        return carry

    lax.fori_loop(0, tm, issue, 0, unroll=8)
    for k in range(2):
        pltpu.make_async_copy(xn_ref, xs_hbm.at[pl.ds(0, tm), :], row_sem).wait()


def _dispatch(pos_tiles, xn, n_rows, tm):
    N = xn.shape[0]
    init = jnp.zeros((n_rows, D_MODEL), F32)
    return pl.pallas_call(
        functools.partial(_dispatch_kernel, tm=tm),
        grid=(N // tm,),
        in_specs=[pl.BlockSpec(memory_space=pl.ANY),
                  pl.BlockSpec((tm, D_MODEL), lambda i: (i, 0)),
                  pl.BlockSpec(memory_space=pl.ANY)],
        out_specs=pl.BlockSpec(memory_space=pl.ANY),
        out_shape=jax.ShapeDtypeStruct((n_rows, D_MODEL), F32),
        scratch_shapes=[pltpu.SMEM((2 * tm,), jnp.int32), pltpu.SemaphoreType.DMA,
                        pltpu.SemaphoreType.DMA],
        input_output_aliases={2: 0},
        compiler_params=_params("arbitrary"),
        name="moe_dispatch",
    )(pos_tiles, xn, init)


def _expert_kernel(te_ref, tv_ref, x_ref, wg_ref, wu_ref, wd_ref, y_ref, xb_ref, acc_ref):
    del te_ref
    i = pl.program_id(0)
    f = pl.program_id(1)
    live = tv_ref[i] > 0

    @pl.when(live & (f == 0))
    def _():
        xb_ref[...] = x_ref[...].astype(BF16)
        acc_ref[...] = jnp.zeros_like(acc_ref)

    @pl.when(live)
    def _():
        xb = xb_ref[...]
        gate = _dot(xb, wg_ref[0])
        up = _dot(xb, wu_ref[0])
        act = gate * (1.0 / (1.0 + jnp.exp(-gate))) * up
        acc_ref[...] += _dot(act.astype(BF16), wd_ref[0])

    @pl.when(f == pl.num_programs(1) - 1)
    def _():
        y_ref[...] = jnp.where(live, acc_ref[...], 0.0)


def _experts(tile_expert, tile_valid, xs, w_gate, w_up, w_down, tr, tf):
    n_rows = xs.shape[0]
    nf = D_FF // tf
    fsel = lambda f, i, tv: jnp.where(tv[i] > 0, f, 0)
    grid_spec = pltpu.PrefetchScalarGridSpec(
        num_scalar_prefetch=2,
        grid=(n_rows // tr, nf),
        in_specs=[pl.BlockSpec((tr, D_MODEL), lambda i, f, te, tv: (i, 0)),
                  pl.BlockSpec((1, D_MODEL, tf), lambda i, f, te, tv: (te[i], 0, fsel(f, i, tv))),
                  pl.BlockSpec((1, D_MODEL, tf), lambda i, f, te, tv: (te[i], 0, fsel(f, i, tv))),
                  pl.BlockSpec((1, tf, D_MODEL), lambda i, f, te, tv: (te[i], fsel(f, i, tv), 0))],
        out_specs=pl.BlockSpec((tr, D_MODEL), lambda i, f, te, tv: (i, 0)),
        scratch_shapes=[pltpu.VMEM((tr, D_MODEL), BF16), pltpu.VMEM((tr, D_MODEL), F32)])
    return pl.pallas_call(
        _expert_kernel,
        grid_spec=grid_spec,
        out_shape=jax.ShapeDtypeStruct((n_rows, D_MODEL), F32),
        compiler_params=_params("arbitrary", "arbitrary"),
        name="moe_experts",
    )(tile_expert, tile_valid, xs, w_gate, w_up, w_down)


def _combine_kernel(pos_hbm, hres_ref, slab_ref, fn_ref, ys_hbm, o_ref, pos_smem, ybuf_ref,
                    idx_sem, row_sem, *, tm):
    i = pl.program_id(0)
    idx_copy = pltpu.make_async_copy(pos_hbm.at[i], pos_smem, idx_sem)
    idx_copy.start()
    idx_copy.wait()

    def issue(t, carry):
        for k in range(2):
            _row_copy(ys_hbm, pos_smem[k * tm + t], ybuf_ref.at[k], t, row_sem).start()
        return carry

    lax.fori_loop(0, tm, issue, 0, unroll=8)
    for k in range(2):
        pltpu.make_async_copy(ys_hbm.at[pl.ds(0, tm), :], ybuf_ref.at[k], row_sem).wait()

    slab = slab_ref[...]
    lane = lax.broadcasted_iota(jnp.int32, slab.shape, 1)
    w1 = jnp.sum(jnp.where(lane == 4, slab, 0.0), axis=1, keepdims=True)
    w2 = jnp.sum(jnp.where(lane == 5, slab, 0.0), axis=1, keepdims=True)
    out = hres_ref[...] + w1 * ybuf_ref[0] + w2 * ybuf_ref[1]
    o_ref[...] = _rms(out, fn_ref[...])


def _combine(pos_tiles, hres, slab, final_g, ys, tm):
    N = hres.shape[0]
    row = lambda i: (i, 0)
    return pl.pallas_call(
        functools.partial(_combine_kernel, tm=tm),
        grid=(N // tm,),
        in_specs=[pl.BlockSpec(memory_space=pl.ANY), pl.BlockSpec((tm, D_MODEL), row),
                  pl.BlockSpec((tm, LANES), row), pl.BlockSpec(final_g.shape, lambda i: (0, 0)),
                  pl.BlockSpec(memory_space=pl.ANY)],
        out_specs=pl.BlockSpec((tm, D_MODEL), row),
        out_shape=jax.ShapeDtypeStruct((N, D_MODEL), F32),
        scratch_shapes=[pltpu.SMEM((2 * tm,), jnp.int32), pltpu.VMEM((2, tm, D_MODEL), F32),
                        pltpu.SemaphoreType.DMA, pltpu.SemaphoreType.DMA],
        compiler_params=_params("arbitrary"),
        name="moe_combine",
    )(pos_tiles, hres, slab, final_g, ys)


def _moe(h, a, b, wa, wb, g, w_router, w_gate, w_up, w_down, final_g):
    N = h.shape[0]
    tm, tr = MOE_TM, MOE_TR
    hres, xn, slab, meta, cnt = _router(h, a, b, wa, wb, g, w_router, tm)
    e1, e2, r1, r2 = (meta[j].astype(jnp.int32) for j in range(4))
    counts = cnt[0, :N_EXPERTS].astype(jnp.int32)
    cap = (counts + tr - 1) // tr * tr
    ends = jnp.cumsum(cap)
    offs = ends - cap
    pos = jnp.stack([offs[e1] + r1, offs[e2] + r2])
    pos_tiles = pos.reshape(2, N // tm, tm).transpose(1, 0, 2).reshape(N // tm, 2 * tm)
    n_rows = 2 * N + N_EXPERTS * tr
    tile_start = jnp.arange(n_rows // tr, dtype=jnp.int32) * tr
    tile_expert = jnp.minimum(jnp.sum(tile_start[:, None] >= ends[None, :], axis=1),
                              N_EXPERTS - 1).astype(jnp.int32)
    tile_valid = jnp.clip(counts[tile_expert] - (tile_start - offs[tile_expert]), 0, tr)
    tile_valid = jnp.where(tile_start < ends[N_EXPERTS - 1], tile_valid, 0).astype(jnp.int32)
    xs = _dispatch(pos_tiles, xn, n_rows, tm)
    ys = _experts(tile_expert, tile_valid, xs, w_gate, w_up, w_down, tr, FFN_TF)
    return _combine(pos_tiles, hres, slab, final_g, ys, tm)


def _l1_pre_kernel(x_ref, pos_ref, g_ref, w_ref, cw_ref, invf_ref, mlo_ref, mhi_ref,
                   q_ref, kv_ref, oc_ref, tail_ref):
    @pl.when(pl.program_id(1) == 0)
    def _():
        tail_ref[...] = jnp.zeros_like(tail_ref)

    xn = _rms(x_ref[...], g_ref[...]).astype(BF16)
    y = _dot(xn, w_ref[...])
    ang = pos_ref[...].astype(F32) * invf_ref[...]
    cos = jnp.cos(ang)
    sin = jnp.sin(ang)
    sin_lo = sin * mlo_ref[...]
    sin_hi = sin * mhi_ref[...]
    half = HEAD_DIM // 2
    qw = SWA_HEADS * HEAD_DIM
    for j in range(qw // LANES):
        sl = slice(j * LANES, (j + 1) * LANES)
        q_ref[:, sl] = _rope(y[:, sl], cos, sin_lo, sin_hi, half).astype(BF16)
    kv_ref[:, :LANES] = _rope(y[:, qw:qw + LANES], cos, sin_lo, sin_hi, half).astype(BF16)
    kv_ref[:, LANES:] = y[:, qw + LANES:qw + 2 * LANES].astype(BF16)

    o = qw + 2 * LANES
    C = CONV_CHANNELS
    gate_b = y[:, o:o + C]
    u = y[:, o + C:o + 2 * C] * y[:, o + 2 * C:o + 3 * C]
    tm = u.shape[0]
    rows = lax.broadcasted_iota(jnp.int32, u.shape, 0)
    tail = tail_ref[...]
    u1 = jnp.where(rows == 0, tail[7:8, :], pltpu.roll(u, 1, 0))
    u2 = jnp.where(rows == 0, tail[6:7, :], jnp.where(rows == 1, tail[7:8, :], pltpu.roll(u, 2, 0)))
    cw = cw_ref[...]
    oc_ref[...] = (gate_b * (cw[0:1, :] * u2 + cw[1:2, :] * u1 + cw[2:3, :] * u)).astype(BF16)
    tail_ref[...] = u[tm - 8:tm, :]


def _l1_pre(x2, pos2, g, w_in, conv_w, B, S, tm):
    N = B * S
    ns = S // tm
    lane = jnp.arange(LANES)
    half = HEAD_DIM // 2
    freq = ROPE_THETA ** (-jnp.arange(0, HEAD_DIM, 2, dtype=F32) / HEAD_DIM)
    invf = freq[lane % half][None, :].astype(F32)
    m_lo = jnp.where(lane % HEAD_DIM < half, -1.0, 0.0)[None, :].astype(F32)
    m_hi = jnp.where(lane % HEAD_DIM >= half, 1.0, 0.0)[None, :].astype(F32)
    row = lambda b, s: (b * ns + s, 0)
    const = lambda b, s: (0, 0)
    full = lambda a: pl.BlockSpec(a.shape, const)
    return pl.pallas_call(
        _l1_pre_kernel,
        grid=(B, ns),
        in_specs=[pl.BlockSpec((tm, D_MODEL), row), pl.BlockSpec((tm, 1), row),
                  full(g), full(w_in), full(conv_w), full(invf), full(m_lo), full(m_hi)],
        out_specs=[pl.BlockSpec((tm, SWA_HEADS * HEAD_DIM), row),
                   pl.BlockSpec((tm, 2 * LANES), row),
                   pl.BlockSpec((tm, CONV_CHANNELS), row)],
        out_shape=[jax.ShapeDtypeStruct((N, SWA_HEADS * HEAD_DIM), BF16),
                   jax.ShapeDtypeStruct((N, 2 * LANES), BF16),
                   jax.ShapeDtypeStruct((N, CONV_CHANNELS), BF16)],
        scratch_shapes=[pltpu.VMEM((8, CONV_CHANNELS), F32)],
        compiler_params=_params("arbitrary", "arbitrary"),
        name="l1_pre",
    )(x2, pos2, g, w_in, conv_w, invf, m_lo, m_hi)


def _swa_kernel(sink_ref, q_ref, kv_ref, kvp_ref, o_ref, *, tq):
    s_idx = pl.program_id(1)
    first = lax.broadcasted_iota(jnp.int32, (1, LANES), 1) < HEAD_DIM
    G = SWA_HEADS // SWA_KV_HEADS
    W = WINDOW

    def dup(x):
        r = pltpu.roll(x, HEAD_DIM, 1)
        return (jnp.where(first, x, r), jnp.where(first, r, x))

    kv = kv_ref[0]
    kvp = kvp_ref[0]
    k_all = jnp.concatenate([kvp[:, :LANES], kv[:, :LANES]], axis=0)
    v_all = jnp.concatenate([kvp[:, LANES:], kv[:, LANES:]], axis=0)
    k_dup = dup(k_all)
    v_dup = dup(v_all)
    q = q_ref[0]
    zero = jnp.zeros((W, LANES), q.dtype)
    rel = (lax.broadcasted_iota(jnp.int32, (W, 2 * W), 0) + W
           - lax.broadcasted_iota(jnp.int32, (W, 2 * W), 1))
    in_window = (rel >= 0) & (rel < W)
    for r in range(tq // W):
        key_pos = (s_idx * tq + r * W - W
                   + lax.broadcasted_iota(jnp.int32, (W, 2 * W), 1))
        keep = in_window & (key_pos >= 0)
        keep4 = jnp.concatenate([keep] * (2 * (G // 2)), axis=0)
        for g in range(SWA_KV_HEADS):
            kb = k_dup[g][r * W:r * W + 2 * W]
            vb = v_dup[g][r * W:r * W + 2 * W]
            parts = []
            sinks = []
            for j in range(G // 2):
                blk = g * (G // 2) + j
                q2 = q[r * W:(r + 1) * W, blk * LANES:(blk + 1) * LANES]
                parts += [jnp.where(first, q2, zero), jnp.where(first, zero, q2)]
                sinks += [jnp.full((W, 1), sink_ref[2 * blk], F32),
                          jnp.full((W, 1), sink_ref[2 * blk + 1], F32)]
            qst = jnp.concatenate(parts, axis=0)
            sink = jnp.concatenate(sinks, axis=0)
            s = _dot_nt(qst, kb)
            s = jnp.where(keep4, s, MASK_VALUE)
            m = jnp.maximum(jnp.max(s, axis=1, keepdims=True), sink)
            e = jnp.exp(s - m)
            den = jnp.sum(e, axis=1, keepdims=True) + jnp.exp(sink - m)
            o = _dot((e * (1.0 / den)).astype(BF16), vb)
            for j in range(G // 2):
                blk = g * (G // 2) + j
                oa = o[(2 * j) * W:(2 * j + 1) * W]
                ob = o[(2 * j + 1) * W:(2 * j + 2) * W]
                o_ref[0, r * W:(r + 1) * W, blk * LANES:(blk + 1) * LANES] = (
                    jnp.where(first, oa, ob).astype(o_ref.dtype))


def _swa(sinks, q, kv, B, S, tq):
    ns = S // tq
    per = tq // WINDOW
    return pl.pallas_call(
        functools.partial(_swa_kernel, tq=tq),
        grid=(B, ns),
        in_specs=[pl.BlockSpec(memory_space=pltpu.SMEM),
                  pl.BlockSpec((1, tq, SWA_HEADS * HEAD_DIM), lambda b, s: (b, s, 0)),
                  pl.BlockSpec((1, tq, 2 * LANES), lambda b, s: (b, s, 0)),
                  pl.BlockSpec((1, WINDOW, 2 * LANES),
                               lambda b, s: (b, jnp.maximum(s * per - 1, 0), 0))],
        out_specs=pl.BlockSpec((1, tq, SWA_HEADS * HEAD_DIM), lambda b, s: (b, s, 0)),
        out_shape=jax.ShapeDtypeStruct((B, S, SWA_HEADS * HEAD_DIM), BF16),
        compiler_params=_params("arbitrary", "arbitrary"),
        name="swa_attn",
    )(sinks, q, kv, kv)


def _pad_heads(w, n_heads, width, offset=0):
    K = w.shape[0]
    w3 = w.reshape(K, n_heads, width)
    out = jnp.zeros((K, n_heads, LANES), w.dtype).at[:, :, offset:offset + width].set(w3)
    return out.reshape(K, n_heads * LANES)


def kernel(x, positions, l0_norm_mix, l0_w_in, l0_b_forget, l0_q_norm, l0_w_q_up, l0_kv_norm,
           l0_w_kv_up, l0_w_out, l0_norm_ffn, l0_w_gate, l0_w_up, l0_w_down,
           l1_norm_mix, l1_w_in, l1_sinks, l1_conv_w, l1_w_out, l1_norm_ffn, l1_w_router,
           l1_w_gate, l1_w_up, l1_w_down, final_norm):
    B, S, D = x.shape
    assert D == D_MODEL and S % 512 == 0
    N = B * S
    tm = 512
    x2 = x.reshape(N, D)
    pos2 = positions.reshape(N, 1)
    rowv = lambda v: v.reshape(1, -1).astype(F32)

    fox_w = FOX_HEADS * HEAD_DIM
    o = 3 * fox_w
    w_fq = l0_w_in[:, :fox_w] * (HEAD_DIM ** -0.5 * LOG2E)
    w_f = l0_w_in[:, o:o + FOX_HEADS]
    o += FOX_HEADS
    w_cq = l0_w_in[:, o:o + MLA_Q_RANK]
    o += MLA_Q_RANK
    w_ckv = l0_w_in[:, o:o + MLA_KV_RANK]
    o += MLA_KV_RANK
    w_kr = l0_w_in[:, o:o + MLA_ROPE_DIM]
    w_ext = jnp.concatenate(
        [w_fq, l0_w_in[:, fox_w:3 * fox_w], w_cq, w_ckv,
         _pad_heads(w_kr, 1, MLA_ROPE_DIM, MLA_NOPE_DIM), _pad_heads(w_f, 1, FOX_HEADS)],
        axis=1).astype(BF16)
    bf_row = jnp.zeros((1, LANES), F32).at[0, :FOX_HEADS].set(l0_b_forget.astype(F32))
    qk_dim = MLA_NOPE_DIM + MLA_ROPE_DIM
    wq = _pad_heads(l0_w_q_up * (qk_dim ** -0.5 * LOG2E), MLA_HEADS, qk_dim).astype(BF16)
    kv3 = l0_w_kv_up.reshape(MLA_KV_RANK, MLA_HEADS, MLA_NOPE_DIM + MLA_V_DIM)
    wk = _pad_heads(kv3[:, :, :MLA_NOPE_DIM].reshape(MLA_KV_RANK, -1), MLA_HEADS,
                    MLA_NOPE_DIM).astype(BF16)
    wv = kv3[:, :, MLA_NOPE_DIM:].reshape(MLA_KV_RANK, -1).astype(BF16)

    q_f, k_f, vt_f, q_m, k_m, vt_m, ct = _l0_pre(
        x2, pos2, rowv(l0_norm_mix), w_ext, bf_row, rowv(l0_q_norm), wq, rowv(l0_kv_norm),
        wk, wv, B, S, tm)

    tq, tk, tw = ATTN_TQ, ATTN_TK, ATTN_TW
    o_fox = _attention(q_f.reshape(B, S, fox_w), k_f.reshape(B, S, 2 * fox_w), vt_f,
                       ct.reshape(B, FOX_HEADS // 2, 2, S), B, S, tq, tk, tw, True)
    o_mla = _attention(q_m.reshape(B, S, MLA_PAD), k_m.reshape(B, S, MLA_PAD), vt_m, None,
                       B, S, tq, tk, tw, False)

    w_out0 = l0_w_out.astype(BF16)
    h1 = _ffn(x2, o_fox.reshape(N, -1), o_mla.reshape(N, -1), w_out0[:fox_w], w_out0[fox_w:],
              rowv(l0_norm_ffn), l0_w_gate.astype(BF16), l0_w_up.astype(BF16),
              l0_w_down.astype(BF16), FFN_TM, FFN_TF)

    swa_w = SWA_HEADS * HEAD_DIM
    w_in1 = jnp.concatenate([l1_w_in[:, :swa_w] * (HEAD_DIM ** -0.5), l1_w_in[:, swa_w:]],
                            axis=1).astype(BF16)
    q_s, kv_s, o_conv = _l1_pre(h1, pos2, rowv(l1_norm_mix), w_in1, l1_conv_w.astype(F32),
                                B, S, tm)
    o_swa = _swa(l1_sinks.astype(F32), q_s.reshape(B, S, swa_w), kv_s.reshape(B, S, 2 * LANES),
                 B, S, 512)
    w_out1 = l1_w_out.astype(BF16)
    w_router = jnp.zeros((D_MODEL, LANES), F32).at[:, :N_EXPERTS].set(l1_w_router.astype(F32))
    out = _moe(h1, o_swa.reshape(N, -1), o_conv, w_out1[:swa_w], w_out1[swa_w:],
               rowv(l1_norm_ffn), w_router, l1_w_gate.astype(BF16), l1_w_up.astype(BF16),
               l1_w_down.astype(BF16), rowv(final_norm))
    return out.reshape(B, S, D)
```

```python
import functools

import jax
import jax.numpy as jnp
from jax import lax
from jax.experimental import pallas as pl
from jax.experimental.pallas import tpu as pltpu

F32 = jnp.float32
BF16 = jnp.bfloat16

D_MODEL = 1024
HEAD_DIM = 64
RMS_EPS = 1e-6
ROPE_THETA = 10000.0
FOX_HEADS = 8
MLA_HEADS = 8
MLA_Q_RANK = 384
MLA_KV_RANK = 256
MLA_NOPE_DIM = 64
MLA_ROPE_DIM = 32
MLA_V_DIM = 64
SWA_HEADS = 8
SWA_KV_HEADS = 2
WINDOW = 128
CONV_CHANNELS = 512
CONV_WIDTH = 3
D_FF = 3584
N_EXPERTS = 8

LANES = 128
MASK_VALUE = -1e30
LOG2E = 1.4426950408889634
VMEM_LIMIT = 56 * 1024 * 1024

QKV_W = 3 * FOX_HEADS * HEAD_DIM
MLA_PAD = MLA_HEADS * LANES
L0_COLS = QKV_W + MLA_Q_RANK + MLA_KV_RANK + 2 * LANES
ATTN_TQ = 1024
ATTN_TK = 512
ATTN_TW = 256
ATTN_GROUP = 2
MOE_TM = 512
MOE_TR = 512
FFN_TM = 1024
FFN_TF = 512
L1_COLS = SWA_HEADS * HEAD_DIM + 2 * SWA_KV_HEADS * HEAD_DIM + 3 * CONV_CHANNELS


def _params(*sem):
    return pltpu.CompilerParams(dimension_semantics=sem, vmem_limit_bytes=VMEM_LIMIT)


def _rms(x, g):
    return x * lax.rsqrt(jnp.mean(x * x, axis=-1, keepdims=True) + RMS_EPS) * g


def _dot(a, b):
    return jnp.dot(a, b, preferred_element_type=F32)


def _dot_nt(a, b):
    return lax.dot_general(a, b, (((1,), (1,)), ((), ())), preferred_element_type=F32)


def _rope(x, cos, sin_lo, sin_hi, half):
    return (x * cos + pltpu.roll(x, half, 1) * sin_hi
            + pltpu.roll(x, LANES - half, 1) * sin_lo)


def _l0_pre_kernel(x_ref, pos_ref, g_ref, w_ref, bf_ref, qn_ref, wq_ref, kvn_ref, wk_ref,
                   wv_ref, tri_ref, place_ref, invf_ref, mlo_ref, mhi_ref,
                   qf_ref, kf_ref, vtf_ref, q_ref, k_ref, vt_ref, ct_ref, carry_ref):
    @pl.when(pl.program_id(1) == 0)
    def _():
        carry_ref[...] = jnp.zeros_like(carry_ref)

    fox_w = FOX_HEADS * HEAD_DIM
    xn = _rms(x_ref[...], g_ref[...]).astype(BF16)
    y = _dot(xn, w_ref[...])
    qf_ref[...] = y[:, :fox_w].astype(BF16)
    vtf_ref[0] = y[:, 2 * fox_w:3 * fox_w].T.astype(BF16)
    o = QKV_W
    cq = y[:, o:o + MLA_Q_RANK]
    o += MLA_Q_RANK
    ckv = y[:, o:o + MLA_KV_RANK]
    o += MLA_KV_RANK
    kr = y[:, o:o + LANES]
    fl = y[:, o + LANES:o + 2 * LANES]

    q = _dot(_rms(cq, qn_ref[...]).astype(BF16), wq_ref[...])
    ckn = _rms(ckv, kvn_ref[...]).astype(BF16)
    kk = _dot(ckn, wk_ref[...])
    vt_ref[0] = _dot(ckn, wv_ref[...]).T.astype(BF16)

    ang = pos_ref[...].astype(F32) * invf_ref[...]
    cos = jnp.cos(ang)
    sin = jnp.sin(ang)
    sin_lo = sin * mlo_ref[...]
    sin_hi = sin * mhi_ref[...]
    half = MLA_ROPE_DIM // 2
    kpe = _rope(kr, cos, sin_lo, sin_hi, half)
    for h in range(MLA_HEADS):
        sl = slice(h * LANES, (h + 1) * LANES)
        q_ref[:, sl] = _rope(q[:, sl], cos, sin_lo, sin_hi, half).astype(BF16)
        k_ref[:, sl] = (kk[:, sl] + kpe).astype(BF16)

    z = fl + bf_ref[...]
    ls = jnp.minimum(z, 0.0) - jnp.log1p(jnp.exp(-jnp.abs(z)))
    tri = tri_ref[...]
    c = carry_ref[...]
    for part in _split3(ls):
        c = c + _dot(tri, part)
    tm = c.shape[0]
    carry_ref[...] = c[tm - 1:tm, :]
    c2 = c * LOG2E
    ct_ref[0] = c2.T[:FOX_HEADS, :]
    bias = None
    for i, part in enumerate(_split3(c2)):
        d = _dot(part, place_ref[i])
        bias = d if bias is None else bias + d
    for hp in range(FOX_HEADS // 2):
        kf_ref[:, 2 * hp * LANES:(2 * hp + 1) * LANES] = (
            y[:, fox_w + hp * LANES:fox_w + (hp + 1) * LANES].astype(BF16))
        kf_ref[:, (2 * hp + 1) * LANES:(2 * hp + 2) * LANES] = (
            bias[:, hp * LANES:(hp + 1) * LANES].astype(BF16))


def _split3(x):
    hi = x.astype(BF16)
    r1 = x - hi.astype(F32)
    mid = r1.astype(BF16)
    lo = (r1 - mid.astype(F32)).astype(BF16)
    return hi, mid, lo


def _l0_pre(x2, pos2, g, w_ext, bf_row, qn, wq, kvn, wk, wv, B, S, tm):
    N = B * S
    ns = S // tm
    tri = (jnp.arange(tm)[:, None] >= jnp.arange(tm)[None, :]).astype(BF16)
    n_pairs = FOX_HEADS // 2
    place = jnp.zeros((3, LANES, n_pairs * LANES), F32)
    for hp in range(n_pairs):
        for i in range(3):
            place = place.at[i, 2 * hp, hp * LANES + i].set(-1.0)
            place = place.at[i, 2 * hp + 1, hp * LANES + 3 + i].set(-1.0)
    place = place.astype(BF16)
    lane = jnp.arange(LANES)
    half = MLA_ROPE_DIM // 2
    in_pe = (lane >= MLA_NOPE_DIM) & (lane < MLA_NOPE_DIM + MLA_ROPE_DIM)
    freq = ROPE_THETA ** (-jnp.arange(0, MLA_ROPE_DIM, 2, dtype=F32) / MLA_ROPE_DIM)
    invf = jnp.where(in_pe, freq[(lane - MLA_NOPE_DIM) % half], 0.0)[None, :].astype(F32)
    m_lo = jnp.where(in_pe & (lane < MLA_NOPE_DIM + half), -1.0, 0.0)[None, :].astype(F32)
    m_hi = jnp.where(in_pe & (lane >= MLA_NOPE_DIM + half), 1.0, 0.0)[None, :].astype(F32)

    row = lambda b, s: (b * ns + s, 0)
    tcol = lambda b, s: (b, 0, s)
    full = lambda a: pl.BlockSpec(a.shape, lambda b, s: (0,) * a.ndim)
    fox_w = FOX_HEADS * HEAD_DIM
    vw = MLA_HEADS * MLA_V_DIM
    return pl.pallas_call(
        _l0_pre_kernel,
        grid=(B, ns),
        in_specs=[pl.BlockSpec((tm, D_MODEL), row), pl.BlockSpec((tm, 1), row),
                  full(g), full(w_ext), full(bf_row), full(qn), full(wq), full(kvn), full(wk),
                  full(wv), full(tri), full(place), full(invf), full(m_lo), full(m_hi)],
        out_specs=[pl.BlockSpec((tm, fox_w), row), pl.BlockSpec((tm, 2 * fox_w), row),
                   pl.BlockSpec((1, fox_w, tm), tcol),
                   pl.BlockSpec((tm, MLA_PAD), row), pl.BlockSpec((tm, MLA_PAD), row),
                   pl.BlockSpec((1, vw, tm), tcol),
                   pl.BlockSpec((1, FOX_HEADS, tm), tcol)],
        out_shape=[jax.ShapeDtypeStruct((N, fox_w), BF16),
                   jax.ShapeDtypeStruct((N, 2 * fox_w), BF16),
                   jax.ShapeDtypeStruct((B, fox_w, S), BF16),
                   jax.ShapeDtypeStruct((N, MLA_PAD), BF16),
                   jax.ShapeDtypeStruct((N, MLA_PAD), BF16),
                   jax.ShapeDtypeStruct((B, vw, S), BF16),
                   jax.ShapeDtypeStruct((B, FOX_HEADS, S), F32)],
        scratch_shapes=[pltpu.VMEM((1, LANES), F32)],
        compiler_params=_params("arbitrary", "arbitrary"),
        name="l0_pre",
    )(x2, pos2, g, w_ext, bf_row, qn, wq, kvn, wk, wv, tri, place, invf, m_lo, m_hi)


def _attn_kernel(*refs, fox, tq, tk, tw):
    if fox:
        q_ref, k_ref, vt_ref, cq_ref, ones_ref, o_ref, m_ref, l_ref, acc_ref = refs
    else:
        q_ref, k_ref, vt_ref, o_ref, m_ref, l_ref, acc_ref = refs
    qi = pl.program_id(2)
    n_strips = tq // tw

    q = q_ref[0]
    if fox:
        lane = lax.broadcasted_iota(jnp.int32, (1, LANES), 1)
        zero = jnp.zeros_like(q)
        qa = (jnp.concatenate([jnp.where(lane < HEAD_DIM, q, zero), ones_ref[0]], axis=1),
              jnp.concatenate([jnp.where(lane < HEAD_DIM, zero, q), ones_ref[1]], axis=1))
    else:
        qa = (q[:, :LANES], q[:, LANES:])

    m_ref[...] = jnp.full_like(m_ref, MASK_VALUE)
    l_ref[...] = jnp.zeros_like(l_ref)
    acc_ref[...] = jnp.zeros_like(acc_ref)

    sub = LANES

    def group(chunks):
        work = []
        for start, strips in chunks:
            kc = k_ref[0, pl.ds(start, tk), :]
            for i in range(2):
                for r, n_sub, mask_from in strips:
                    kh = kc[:n_sub * sub] if fox else kc[:n_sub * sub, i * LANES:(i + 1) * LANES]
                    s = _dot_nt(kh, qa[i][r * tw:(r + 1) * tw])
                    work.append((start, i, r, n_sub, mask_from, s))
        for start, i, r, n_sub, mask_from, s in work:
            tiles = []
            for c in range(n_sub):
                t = s[c * sub:(c + 1) * sub]
                if mask_from is not None and c >= mask_from:
                    keys = start + c * sub + lax.broadcasted_iota(jnp.int32, (sub, tw), 0)
                    qpos = qi * tq + r * tw + lax.broadcasted_iota(jnp.int32, (sub, tw), 1)
                    t = jnp.where(keys <= qpos, t, MASK_VALUE)
                tiles.append(t)
            col_max = tiles[0]
            for t in tiles[1:]:
                col_max = jnp.maximum(col_max, t)
            col_max = jnp.max(col_max, axis=0, keepdims=True)
            idx = i * n_strips + r
            m_prev = m_ref[idx]
            if fox:
                cq = cq_ref[0, 0, i:i + 1, r * tw:(r + 1) * tw]
                m_new = jnp.maximum(m_prev, col_max + cq)
                shift = m_new - cq
            else:
                m_new = jnp.maximum(m_prev, col_max)
                shift = m_new
            m_ref[idx] = m_new
            alpha = jnp.exp2(m_prev - m_new)
            psum = None
            probs = []
            for t in tiles:
                p = jnp.exp2(t - shift)
                psum = p if psum is None else psum + p
                probs.append(p.astype(BF16))
            l_ref[idx] = alpha * l_ref[idx] + jnp.sum(psum, axis=0, keepdims=True)
            vt = vt_ref[0, i * HEAD_DIM:(i + 1) * HEAD_DIM, pl.ds(start, n_sub * sub)]
            rows = slice(i * HEAD_DIM, (i + 1) * HEAD_DIM)
            cols = slice(r * tw, (r + 1) * tw)
            acc_ref[rows, cols] = (acc_ref[rows, cols] * alpha
                                   + _dot(vt, jnp.concatenate(probs, axis=0)))

    n_sub_full = tk // sub
    full_strips = [(r, n_sub_full, None) for r in range(n_strips)]
    per = tw // sub
    diag = []
    for d in range(tq // tk):
        strips = []
        for r in range(n_strips):
            n_sub = min(max((r + 1) * per - d * n_sub_full, 0), n_sub_full)
            if n_sub > 0:
                strips.append((r, n_sub, max(r * per - d * n_sub_full, 0)))
        diag.append((pl.multiple_of(qi * tq + d * tk, tk), strips))

    n_full = qi * (tq // tk)

    def full_group(j, carry):
        group([(pl.multiple_of((j * ATTN_GROUP + g) * tk, tk), full_strips)
               for g in range(ATTN_GROUP)])
        return carry

    lax.fori_loop(0, n_full // ATTN_GROUP, full_group, 0)
    for rem in range(ATTN_GROUP):
        @pl.when(n_full % ATTN_GROUP == rem)
        def _():
            base = n_full - rem
            group([(pl.multiple_of((base + g) * tk, tk), full_strips) for g in range(rem)] + diag)

    heads = []
    for i in range(2):
        rows = slice(i * HEAD_DIM, (i + 1) * HEAD_DIM)
        heads.append(jnp.concatenate(
            [acc_ref[rows, r * tw:(r + 1) * tw] * (1.0 / l_ref[i * n_strips + r])
             for r in range(n_strips)], axis=1))
    o_ref[0] = jnp.concatenate(heads, axis=0).T.astype(o_ref.dtype)


def _attention(q, k, vt, ct, B, S, tq, tk, tw, fox):
    n_pairs = FOX_HEADS // 2
    nq = S // tq
    qw = LANES if fox else 2 * LANES
    in_specs = [pl.BlockSpec((1, tq, qw), lambda b, h, i: (b, i, h)),
                pl.BlockSpec((1, S, 2 * LANES), lambda b, h, i: (b, 0, h)),
                pl.BlockSpec((1, LANES, S), lambda b, h, i: (b, h, 0))]
    args = [q, k, vt]
    if fox:
        in_specs.append(pl.BlockSpec((1, 1, 2, tq), lambda b, h, i: (b, h, 0, i)))
        args.append(ct)
        lane = jnp.arange(LANES)
        ones = jnp.stack([lane < 3, (lane >= 3) & (lane < 6)]).astype(BF16)
        in_specs.append(pl.BlockSpec((2, tq, LANES), lambda b, h, i: (0, 0, 0)))
        args.append(jnp.broadcast_to(ones[:, None, :], (2, tq, LANES)))
    n_strips = tq // tw
    return pl.pallas_call(
        functools.partial(_attn_kernel, fox=fox, tq=tq, tk=tk, tw=tw),
        grid=(B, n_pairs, nq),
        in_specs=in_specs,
        out_specs=pl.BlockSpec((1, tq, LANES), lambda b, h, i: (b, i, h)),
        out_shape=jax.ShapeDtypeStruct((B, S, n_pairs * LANES), BF16),
        scratch_shapes=[pltpu.VMEM((2 * n_strips, 1, tw), F32),
                        pltpu.VMEM((2 * n_strips, 1, tw), F32),
                        pltpu.VMEM((LANES, tq), F32)],
        compiler_params=_params("arbitrary", "arbitrary", "arbitrary"),
        name="fox_attn" if fox else "mla_attn",
    )(*args)


def _ffn_kernel(h_ref, a_ref, b_ref, wa_ref, wb_ref, g_ref, wg_ref, wu_ref, wd_ref,
                o_ref, hres_ref, xn_ref, acc_ref):
    f = pl.program_id(1)

    @pl.when(f == 0)
    def _():
        hres = h_ref[...] + _dot(a_ref[...], wa_ref[...]) + _dot(b_ref[...], wb_ref[...])
        hres_ref[...] = hres
        xn_ref[...] = _rms(hres, g_ref[...]).astype(BF16)
        acc_ref[...] = jnp.zeros_like(acc_ref)

    xn = xn_ref[...]
    gate = _dot(xn, wg_ref[...])
    up = _dot(xn, wu_ref[...])
    act = gate * (1.0 / (1.0 + jnp.exp(-gate))) * up
    acc_ref[...] += _dot(act.astype(BF16), wd_ref[...])

    @pl.when(f == pl.num_programs(1) - 1)
    def _():
        o_ref[...] = hres_ref[...] + acc_ref[...]


def _ffn(h, a, b, wa, wb, g, w_gate, w_up, w_down, tm, tf):
    N = h.shape[0]
    row = lambda i, f: (i, 0)
    full = lambda x: pl.BlockSpec(x.shape, lambda i, f: (0, 0))
    half_w = a.shape[1]
    return pl.pallas_call(
        _ffn_kernel,
        grid=(N // tm, D_FF // tf),
        in_specs=[pl.BlockSpec((tm, D_MODEL), row), pl.BlockSpec((tm, half_w), row),
                  pl.BlockSpec((tm, half_w), row), full(wa), full(wb), full(g),
                  pl.BlockSpec((D_MODEL, tf), lambda i, f: (0, f)),
                  pl.BlockSpec((D_MODEL, tf), lambda i, f: (0, f)),
                  pl.BlockSpec((tf, D_MODEL), lambda i, f: (f, 0))],
        out_specs=pl.BlockSpec((tm, D_MODEL), row),
        out_shape=jax.ShapeDtypeStruct((N, D_MODEL), F32),
        scratch_shapes=[pltpu.VMEM((tm, D_MODEL), F32), pltpu.VMEM((tm, D_MODEL), BF16),
                        pltpu.VMEM((tm, D_MODEL), F32)],
        compiler_params=_params("arbitrary", "arbitrary"),
        name="dense_ffn",
    )(h, a, b, wa, wb, g, w_gate, w_up, w_down)


def _router_kernel(h_ref, a_ref, b_ref, wa_ref, wb_ref, g_ref, wr_ref, tri_ref,
                   hres_ref, xn_ref, slab_ref, meta_ref, cnt_ref):
    @pl.when(pl.program_id(0) == 0)
    def _():
        cnt_ref[...] = jnp.zeros_like(cnt_ref)

    hres = h_ref[...] + _dot(a_ref[...], wa_ref[...]) + _dot(b_ref[...], wb_ref[...])
    hres_ref[...] = hres
    xn = _rms(hres, g_ref[...])
    xn_ref[...] = xn
    xh = xn.astype(BF16)
    xl = (xn - xh.astype(F32)).astype(BF16)
    wr = wr_ref[...]
    wh = wr.astype(BF16)
    wl = (wr - wh.astype(F32)).astype(BF16)
    logits = _dot(xh, wh) + (_dot(xh, wl) + _dot(xl, wh))
    lane = lax.broadcasted_iota(jnp.int32, logits.shape, 1)
    logits = jnp.where(lane < N_EXPERTS, logits, MASK_VALUE)
    m1 = jnp.max(logits, axis=1, keepdims=True)
    i1 = jnp.min(jnp.where(logits == m1, lane, LANES), axis=1, keepdims=True)
    rest = jnp.where(lane == i1, MASK_VALUE, logits)
    m2 = jnp.max(rest, axis=1, keepdims=True)
    i2 = jnp.min(jnp.where(rest == m2, lane, LANES), axis=1, keepdims=True)
    e2 = jnp.exp(m2 - m1)
    w1 = 1.0 / (1.0 + e2)
    w2 = e2 / (1.0 + e2)
    hot1 = lane == i1
    hot2 = lane == i2
    onehot = jnp.where(hot1 | hot2, 1.0, 0.0)
    before = _dot(tri_ref[...], onehot.astype(BF16)) + cnt_ref[...]
    r1 = jnp.sum(jnp.where(hot1, before, 0.0), axis=1, keepdims=True)
    r2 = jnp.sum(jnp.where(hot2, before, 0.0), axis=1, keepdims=True)
    cnt_ref[...] += jnp.sum(onehot, axis=0, keepdims=True)
    slab = (jnp.where(lane == 0, i1.astype(F32), 0.0) + jnp.where(lane == 1, i2.astype(F32), 0.0)
            + jnp.where(lane == 2, r1, 0.0) + jnp.where(lane == 3, r2, 0.0)
            + jnp.where(lane == 4, w1, 0.0) + jnp.where(lane == 5, w2, 0.0))
    slab_ref[...] = slab
    meta_ref[...] = slab.T[:8, :]


def _router(h, a, b, wa, wb, g, w_router, tm):
    N = h.shape[0]
    tri = (jnp.arange(tm)[:, None] > jnp.arange(tm)[None, :]).astype(BF16)
    row = lambda i: (i, 0)
    full = lambda x: pl.BlockSpec(x.shape, lambda i: (0, 0))
    half_w = a.shape[1]
    return pl.pallas_call(
        _router_kernel,
        grid=(N // tm,),
        in_specs=[pl.BlockSpec((tm, D_MODEL), row), pl.BlockSpec((tm, half_w), row),
                  pl.BlockSpec((tm, half_w), row), full(wa), full(wb), full(g), full(w_router),
                  full(tri)],
        out_specs=[pl.BlockSpec((tm, D_MODEL), row), pl.BlockSpec((tm, D_MODEL), row),
                   pl.BlockSpec((tm, LANES), row), pl.BlockSpec((8, tm), lambda i: (0, i)),
                   pl.BlockSpec((1, LANES), lambda i: (0, 0))],
        out_shape=[jax.ShapeDtypeStruct((N, D_MODEL), F32), jax.ShapeDtypeStruct((N, D_MODEL), F32),
                   jax.ShapeDtypeStruct((N, LANES), F32), jax.ShapeDtypeStruct((8, N), F32),
                   jax.ShapeDtypeStruct((1, LANES), F32)],
        compiler_params=_params("arbitrary"),
        name="moe_router",
    )(h, a, b, wa, wb, g, w_router, tri)


def _row_copy(src, src_row, dst, dst_row, sem):
    return pltpu.make_async_copy(src.at[pl.ds(src_row, 1), :], dst.at[pl.ds(dst_row, 1), :], sem)


def _dispatch_kernel(pos_hbm, xn_ref, init_hbm, xs_hbm, pos_smem, idx_sem, row_sem, *, tm):
    del init_hbm
    i = pl.program_id(0)
    idx_copy = pltpu.make_async_copy(pos_hbm.at[i], pos_smem, idx_sem)
    idx_copy.start()
    idx_copy.wait()

    def issue(t, carry):
        for k in range(2):
            _row_copy(xn_ref, t, xs_hbm, pos_smem[k * tm + t], row_sem).start()
        return carry

    lax.fori_loop(0, tm, issue, 0, unroll=8)
    for k in range(2):
        pltpu.make_async_copy(xn_ref, xs_hbm.at[pl.ds(0, tm), :], row_sem).wait()


def _dispatch(pos_tiles, xn, n_rows, tm):
    N = xn.shape[0]
    init = jnp.zeros((n_rows, D_MODEL), F32)
    return pl.pallas_call(
        functools.partial(_dispatch_kernel, tm=tm),
        grid=(N // tm,),
        in_specs=[pl.BlockSpec(memory_space=pl.ANY),
                  pl.BlockSpec((tm, D_MODEL), lambda i: (i, 0)),
                  pl.BlockSpec(memory_space=pl.ANY)],
        out_specs=pl.BlockSpec(memory_space=pl.ANY),
        out_shape=jax.ShapeDtypeStruct((n_rows, D_MODEL), F32),
        scratch_shapes=[pltpu.SMEM((2 * tm,), jnp.int32), pltpu.SemaphoreType.DMA,
                        pltpu.SemaphoreType.DMA],
        input_output_aliases={2: 0},
        compiler_params=_params("arbitrary"),
        name="moe_dispatch",
    )(pos_tiles, xn, init)


def _expert_kernel(te_ref, tv_ref, x_ref, wg_ref, wu_ref, wd_ref, y_ref, xb_ref, acc_ref):
    del te_ref
    i = pl.program_id(0)
    f = pl.program_id(1)
    live = tv_ref[i] > 0

    @pl.when(live & (f == 0))
    def _():
        xb_ref[...] = x_ref[...].astype(BF16)
        acc_ref[...] = jnp.zeros_like(acc_ref)

    @pl.when(live)
    def _():
        xb = xb_ref[...]
        gate = _dot(xb, wg_ref[0])
        up = _dot(xb, wu_ref[0])
        act = gate * (1.0 / (1.0 + jnp.exp(-gate))) * up
        acc_ref[...] += _dot(act.astype(BF16), wd_ref[0])

    @pl.when(f == pl.num_programs(1) - 1)
    def _():
        y_ref[...] = jnp.where(live, acc_ref[...], 0.0)


def _experts(tile_expert, tile_valid, xs, w_gate, w_up, w_down, tr, tf):
    n_rows = xs.shape[0]
    nf = D_FF // tf
    fsel = lambda f, i, tv: jnp.where(tv[i] > 0, f, 0)
    grid_spec = pltpu.PrefetchScalarGridSpec(
        num_scalar_prefetch=2,
        grid=(n_rows // tr, nf),
        in_specs=[pl.BlockSpec((tr, D_MODEL), lambda i, f, te, tv: (i, 0)),
                  pl.BlockSpec((1, D_MODEL, tf), lambda i, f, te, tv: (te[i], 0, fsel(f, i, tv))),
                  pl.BlockSpec((1, D_MODEL, tf), lambda i, f, te, tv: (te[i], 0, fsel(f, i, tv))),
                  pl.BlockSpec((1, tf, D_MODEL), lambda i, f, te, tv: (te[i], fsel(f, i, tv), 0))],
        out_specs=pl.BlockSpec((tr, D_MODEL), lambda i, f, te, tv: (i, 0)),
        scratch_shapes=[pltpu.VMEM((tr, D_MODEL), BF16), pltpu.VMEM((tr, D_MODEL), F32)])
    return pl.pallas_call(
        _expert_kernel,
        grid_spec=grid_spec,
        out_shape=jax.ShapeDtypeStruct((n_rows, D_MODEL), F32),
        compiler_params=_params("arbitrary", "arbitrary"),
        name="moe_experts",
    )(tile_expert, tile_valid, xs, w_gate, w_up, w_down)


def _combine_kernel(pos_hbm, hres_ref, slab_ref, fn_ref, ys_hbm, o_ref, pos_smem, ybuf_ref,
                    idx_sem, row_sem, *, tm):
    i = pl.program_id(0)
    idx_copy = pltpu.make_async_copy(pos_hbm.at[i], pos_smem, idx_sem)
    idx_copy.start()
    idx_copy.wait()

    def issue(t, carry):
        for k in range(2):
            _row_copy(ys_hbm, pos_smem[k * tm + t], ybuf_ref.at[k], t, row_sem).start()
        return carry

    lax.fori_loop(0, tm, issue, 0, unroll=8)
    for k in range(2):
        pltpu.make_async_copy(ys_hbm.at[pl.ds(0, tm), :], ybuf_ref.at[k], row_sem).wait()

    slab = slab_ref[...]
    lane = lax.broadcasted_iota(jnp.int32, slab.shape, 1)
    w1 = jnp.sum(jnp.where(lane == 4, slab, 0.0), axis=1, keepdims=True)
    w2 = jnp.sum(jnp.where(lane == 5, slab, 0.0), axis=1, keepdims=True)
    out = hres_ref[...] + w1 * ybuf_ref[0] + w2 * ybuf_ref[1]
    o_ref[...] = _rms(out, fn_ref[...])


def _combine(pos_tiles, hres, slab, final_g, ys, tm):
    N = hres.shape[0]
    row = lambda i: (i, 0)
    return pl.pallas_call(
        functools.partial(_combine_kernel, tm=tm),
        grid=(N // tm,),
        in_specs=[pl.BlockSpec(memory_space=pl.ANY), pl.BlockSpec((tm, D_MODEL), row),
                  pl.BlockSpec((tm, LANES), row), pl.BlockSpec(final_g.shape, lambda i: (0, 0)),
                  pl.BlockSpec(memory_space=pl.ANY)],
        out_specs=pl.BlockSpec((tm, D_MODEL), row),
        out_shape=jax.ShapeDtypeStruct((N, D_MODEL), F32),
        scratch_shapes=[pltpu.SMEM((2 * tm,), jnp.int32), pltpu.VMEM((2, tm, D_MODEL), F32),
                        pltpu.SemaphoreType.DMA, pltpu.SemaphoreType.DMA],
        compiler_params=_params("arbitrary"),
        name="moe_combine",
    )(pos_tiles, hres, slab, final_g, ys)


def _moe(h, a, b, wa, wb, g, w_router, w_gate, w_up, w_down, final_g):
    N = h.shape[0]
    tm, tr = MOE_TM, MOE_TR
    hres, xn, slab, meta, cnt = _router(h, a, b, wa, wb, g, w_router, tm)
    e1, e2, r1, r2 = (meta[j].astype(jnp.int32) for j in range(4))
    counts = cnt[0, :N_EXPERTS].astype(jnp.int32)
    cap = (counts + tr - 1) // tr * tr
    ends = jnp.cumsum(cap)
    offs = ends - cap
    pos = jnp.stack([offs[e1] + r1, offs[e2] + r2])
    pos_tiles = pos.reshape(2, N // tm, tm).transpose(1, 0, 2).reshape(N // tm, 2 * tm)
    n_rows = 2 * N + N_EXPERTS * tr
    tile_start = jnp.arange(n_rows // tr, dtype=jnp.int32) * tr
    tile_expert = jnp.minimum(jnp.sum(tile_start[:, None] >= ends[None, :], axis=1),
                              N_EXPERTS - 1).astype(jnp.int32)
    tile_valid = jnp.clip(counts[tile_expert] - (tile_start - offs[tile_expert]), 0, tr)
    tile_valid = jnp.where(tile_start < ends[N_EXPERTS - 1], tile_valid, 0).astype(jnp.int32)
    xs = _dispatch(pos_tiles, xn, n_rows, tm)
    ys = _experts(tile_expert, tile_valid, xs, w_gate, w_up, w_down, tr, FFN_TF)
    return _combine(pos_tiles, hres, slab, final_g, ys, tm)


def _l1_pre_kernel(x_ref, pos_ref, g_ref, w_ref, cw_ref, invf_ref, mlo_ref, mhi_ref,
                   q_ref, kv_ref, oc_ref, tail_ref):
    @pl.when(pl.program_id(1) == 0)
    def _():
        tail_ref[...] = jnp.zeros_like(tail_ref)

    xn = _rms(x_ref[...], g_ref[...]).astype(BF16)
    y = _dot(xn, w_ref[...])
    ang = pos_ref[...].astype(F32) * invf_ref[...]
    cos = jnp.cos(ang)
    sin = jnp.sin(ang)
    sin_lo = sin * mlo_ref[...]
    sin_hi = sin * mhi_ref[...]
    half = HEAD_DIM // 2
    qw = SWA_HEADS * HEAD_DIM
    for j in range(qw // LANES):
        sl = slice(j * LANES, (j + 1) * LANES)
        q_ref[:, sl] = _rope(y[:, sl], cos, sin_lo, sin_hi, half).astype(BF16)
    kv_ref[:, :LANES] = _rope(y[:, qw:qw + LANES], cos, sin_lo, sin_hi, half).astype(BF16)
    kv_ref[:, LANES:] = y[:, qw + LANES:qw + 2 * LANES].astype(BF16)

    o = qw + 2 * LANES
    C = CONV_CHANNELS
    gate_b = y[:, o:o + C]
    u = y[:, o + C:o + 2 * C] * y[:, o + 2 * C:o + 3 * C]
    tm = u.shape[0]
    rows = lax.broadcasted_iota(jnp.int32, u.shape, 0)
    tail = tail_ref[...]
    u1 = jnp.where(rows == 0, tail[7:8, :], pltpu.roll(u, 1, 0))
    u2 = jnp.where(rows == 0, tail[6:7, :], jnp.where(rows == 1, tail[7:8, :], pltpu.roll(u, 2, 0)))
    cw = cw_ref[...]
    oc_ref[...] = (gate_b * (cw[0:1, :] * u2 + cw[1:2, :] * u1 + cw[2:3, :] * u)).astype(BF16)
    tail_ref[...] = u[tm - 8:tm, :]


def _l1_pre(x2, pos2, g, w_in, conv_w, B, S, tm):
    N = B * S
    ns = S // tm
    lane = jnp.arange(LANES)
    half = HEAD_DIM // 2
    freq = ROPE_THETA ** (-jnp.arange(0, HEAD_DIM, 2, dtype=F32) / HEAD_DIM)
    invf = freq[lane % half][None, :].astype(F32)
    m_lo = jnp.where(lane % HEAD_DIM < half, -1.0, 0.0)[None, :].astype(F32)
    m_hi = jnp.where(lane % HEAD_DIM >= half, 1.0, 0.0)[None, :].astype(F32)
    row = lambda b, s: (b * ns + s, 0)
    const = lambda b, s: (0, 0)
    full = lambda a: pl.BlockSpec(a.shape, const)
    return pl.pallas_call(
        _l1_pre_kernel,
        grid=(B, ns),
        in_specs=[pl.BlockSpec((tm, D_MODEL), row), pl.BlockSpec((tm, 1), row),
                  full(g), full(w_in), full(conv_w), full(invf), full(m_lo), full(m_hi)],
        out_specs=[pl.BlockSpec((tm, SWA_HEADS * HEAD_DIM), row),
                   pl.BlockSpec((tm, 2 * LANES), row),
                   pl.BlockSpec((tm, CONV_CHANNELS), row)],
        out_shape=[jax.ShapeDtypeStruct((N, SWA_HEADS * HEAD_DIM), BF16),
                   jax.ShapeDtypeStruct((N, 2 * LANES), BF16),
                   jax.ShapeDtypeStruct((N, CONV_CHANNELS), BF16)],
        scratch_shapes=[pltpu.VMEM((8, CONV_CHANNELS), F32)],
        compiler_params=_params("arbitrary", "arbitrary"),
        name="l1_pre",
    )(x2, pos2, g, w_in, conv_w, invf, m_lo, m_hi)


def _swa_kernel(sink_ref, q_ref, kv_ref, kvp_ref, o_ref, *, tq):
    s_idx = pl.program_id(1)
    first = lax.broadcasted_iota(jnp.int32, (1, LANES), 1) < HEAD_DIM
    G = SWA_HEADS // SWA_KV_HEADS
    W = WINDOW

    def dup(x):
        r = pltpu.roll(x, HEAD_DIM, 1)
        return (jnp.where(first, x, r), jnp.where(first, r, x))

    kv = kv_ref[0]
    kvp = kvp_ref[0]
    k_all = jnp.concatenate([kvp[:, :LANES], kv[:, :LANES]], axis=0)
    v_all = jnp.concatenate([kvp[:, LANES:], kv[:, LANES:]], axis=0)
    k_dup = dup(k_all)
    v_dup = dup(v_all)
    q = q_ref[0]
    zero = jnp.zeros((W, LANES), q.dtype)
    rel = (lax.broadcasted_iota(jnp.int32, (W, 2 * W), 0) + W
           - lax.broadcasted_iota(jnp.int32, (W, 2 * W), 1))
    in_window = (rel >= 0) & (rel < W)
    for r in range(tq // W):
        key_pos = (s_idx * tq + r * W - W
                   + lax.broadcasted_iota(jnp.int32, (W, 2 * W), 1))
        keep = in_window & (key_pos >= 0)
        keep4 = jnp.concatenate([keep] * (2 * (G // 2)), axis=0)
        for g in range(SWA_KV_HEADS):
            kb = k_dup[g][r * W:r * W + 2 * W]
            vb = v_dup[g][r * W:r * W + 2 * W]
            parts = []
            sinks = []
            for j in range(G // 2):
                blk = g * (G // 2) + j
                q2 = q[r * W:(r + 1) * W, blk * LANES:(blk + 1) * LANES]
                parts += [jnp.where(first, q2, zero), jnp.where(first, zero, q2)]
                sinks += [jnp.full((W, 1), sink_ref[2 * blk], F32),
                          jnp.full((W, 1), sink_ref[2 * blk + 1], F32)]
            qst = jnp.concatenate(parts, axis=0)
            sink = jnp.concatenate(sinks, axis=0)
            s = _dot_nt(qst, kb)
            s = jnp.where(keep4, s, MASK_VALUE)
            m = jnp.maximum(jnp.max(s, axis=1, keepdims=True), sink)
            e = jnp.exp(s - m)
            den = jnp.sum(e, axis=1, keepdims=True) + jnp.exp(sink - m)
            o = _dot((e * (1.0 / den)).astype(BF16), vb)
            for j in range(G // 2):
                blk = g * (G // 2) + j
                oa = o[(2 * j) * W:(2 * j + 1) * W]
                ob = o[(2 * j + 1) * W:(2 * j + 2) * W]
                o_ref[0, r * W:(r + 1) * W, blk * LANES:(blk + 1) * LANES] = (
                    jnp.where(first, oa, ob).astype(o_ref.dtype))


def _swa(sinks, q, kv, B, S, tq):
    ns = S // tq
    per = tq // WINDOW
    return pl.pallas_call(
        functools.partial(_swa_kernel, tq=tq),
        grid=(B, ns),
        in_specs=[pl.BlockSpec(memory_space=pltpu.SMEM),
                  pl.BlockSpec((1, tq, SWA_HEADS * HEAD_DIM), lambda b, s: (b, s, 0)),
                  pl.BlockSpec((1, tq, 2 * LANES), lambda b, s: (b, s, 0)),
                  pl.BlockSpec((1, WINDOW, 2 * LANES),
                               lambda b, s: (b, jnp.maximum(s * per - 1, 0), 0))],
        out_specs=pl.BlockSpec((1, tq, SWA_HEADS * HEAD_DIM), lambda b, s: (b, s, 0)),
        out_shape=jax.ShapeDtypeStruct((B, S, SWA_HEADS * HEAD_DIM), BF16),
        compiler_params=_params("arbitrary", "arbitrary"),
        name="swa_attn",
    )(sinks, q, kv, kv)


def _pad_heads(w, n_heads, width, offset=0):
    K = w.shape[0]
    w3 = w.reshape(K, n_heads, width)
    out = jnp.zeros((K, n_heads, LANES), w.dtype).at[:, :, offset:offset + width].set(w3)
    return out.reshape(K, n_heads * LANES)


def kernel(x, positions, l0_norm_mix, l0_w_in, l0_b_forget, l0_q_norm, l0_w_q_up, l0_kv_norm,
           l0_w_kv_up, l0_w_out, l0_norm_ffn, l0_w_gate, l0_w_up, l0_w_down,
           l1_norm_mix, l1_w_in, l1_sinks, l1_conv_w, l1_w_out, l1_norm_ffn, l1_w_router,
           l1_w_gate, l1_w_up, l1_w_down, final_norm):
    B, S, D = x.shape
    assert D == D_MODEL and S % 512 == 0
    N = B * S
    tm = 512
    x2 = x.reshape(N, D)
    pos2 = positions.reshape(N, 1)
    rowv = lambda v: v.reshape(1, -1).astype(F32)

    fox_w = FOX_HEADS * HEAD_DIM
    o = 3 * fox_w
    w_fq = l0_w_in[:, :fox_w] * (HEAD_DIM ** -0.5 * LOG2E)
    w_f = l0_w_in[:, o:o + FOX_HEADS]
    o += FOX_HEADS
    w_cq = l0_w_in[:, o:o + MLA_Q_RANK]
    o += MLA_Q_RANK
    w_ckv = l0_w_in[:, o:o + MLA_KV_RANK]
    o += MLA_KV_RANK
    w_kr = l0_w_in[:, o:o + MLA_ROPE_DIM]
    w_ext = jnp.concatenate(
        [w_fq, l0_w_in[:, fox_w:3 * fox_w], w_cq, w_ckv,
         _pad_heads(w_kr, 1, MLA_ROPE_DIM, MLA_NOPE_DIM), _pad_heads(w_f, 1, FOX_HEADS)],
        axis=1).astype(BF16)
    bf_row = jnp.zeros((1, LANES), F32).at[0, :FOX_HEADS].set(l0_b_forget.astype(F32))
    qk_dim = MLA_NOPE_DIM + MLA_ROPE_DIM
    wq = _pad_heads(l0_w_q_up * (qk_dim ** -0.5 * LOG2E), MLA_HEADS, qk_dim).astype(BF16)
    kv3 = l0_w_kv_up.reshape(MLA_KV_RANK, MLA_HEADS, MLA_NOPE_DIM + MLA_V_DIM)
    wk = _pad_heads(kv3[:, :, :MLA_NOPE_DIM].reshape(MLA_KV_RANK, -1), MLA_HEADS,
                    MLA_NOPE_DIM).astype(BF16)
    wv = kv3[:, :, MLA_NOPE_DIM:].reshape(MLA_KV_RANK, -1).astype(BF16)

    q_f, k_f, vt_f, q_m, k_m, vt_m, ct = _l0_pre(
        x2, pos2, rowv(l0_norm_mix), w_ext, bf_row, rowv(l0_q_norm), wq, rowv(l0_kv_norm),
        wk, wv, B, S, tm)

    tq, tk, tw = ATTN_TQ, ATTN_TK, ATTN_TW
    o_fox = _attention(q_f.reshape(B, S, fox_w), k_f.reshape(B, S, 2 * fox_w), vt_f,
                       ct.reshape(B, FOX_HEADS // 2, 2, S), B, S, tq, tk, tw, True)
    o_mla = _attention(q_m.reshape(B, S, MLA_PAD), k_m.reshape(B, S, MLA_PAD), vt_m, None,
                       B, S, tq, tk, tw, False)

    w_out0 = l0_w_out.astype(BF16)
    h1 = _ffn(x2, o_fox.reshape(N, -1), o_mla.reshape(N, -1), w_out0[:fox_w], w_out0[fox_w:],
              rowv(l0_norm_ffn), l0_w_gate.astype(BF16), l0_w_up.astype(BF16),
              l0_w_down.astype(BF16), FFN_TM, FFN_TF)

    swa_w = SWA_HEADS * HEAD_DIM
    w_in1 = jnp.concatenate([l1_w_in[:, :swa_w] * (HEAD_DIM ** -0.5), l1_w_in[:, swa_w:]],
                            axis=1).astype(BF16)
    q_s, kv_s, o_conv = _l1_pre(h1, pos2, rowv(l1_norm_mix), w_in1, l1_conv_w.astype(F32),
                                B, S, tm)
    o_swa = _swa(l1_sinks.astype(F32), q_s.reshape(B, S, swa_w), kv_s.reshape(B, S, 2 * LANES),
                 B, S, 512)
    w_out1 = l1_w_out.astype(BF16)
    w_router = jnp.zeros((D_MODEL, LANES), F32).at[:, :N_EXPERTS].set(l1_w_router.astype(F32))
    out = _moe(h1, o_swa.reshape(N, -1), o_conv, w_out1[:swa_w], w_out1[swa_w:],
               rowv(l1_norm_ffn), w_router, l1_w_gate.astype(BF16), l1_w_up.astype(BF16),
               l1_w_down.astype(BF16), rowv(final_norm))
    return out.reshape(B, S, D)
```

```python
import functools

import jax
import jax.numpy as jnp
from jax import lax
from jax.experimental import pallas as pl
from jax.experimental.pallas import tpu as pltpu

F32 = jnp.float32
BF16 = jnp.bfloat16

D_MODEL = 1024
HEAD_DIM = 64
RMS_EPS = 1e-6
ROPE_THETA = 10000.0
FOX_HEADS = 8
MLA_HEADS = 8
MLA_Q_RANK = 384
MLA_KV_RANK = 256
MLA_NOPE_DIM = 64
MLA_ROPE_DIM = 32
MLA_V_DIM = 64
SWA_HEADS = 8
SWA_KV_HEADS = 2
WINDOW = 128
CONV_CHANNELS = 512
CONV_WIDTH = 3
D_FF = 3584
N_EXPERTS = 8

LANES = 128
MASK_VALUE = -1e30
LOG2E = 1.4426950408889634
VMEM_LIMIT = 56 * 1024 * 1024

QKV_W = 3 * FOX_HEADS * HEAD_DIM
MLA_PAD = MLA_HEADS * LANES
L0_COLS = QKV_W + MLA_Q_RANK + MLA_KV_RANK + 2 * LANES
ATTN_TQ = 1024
ATTN_TK = 512
ATTN_TW = 256
ATTN_GROUP = 2
MOE_TM = 512
MOE_TR = 512
MOE_TF = 896
FFN_TM = 1024
FFN_TF = 512
L1_COLS = SWA_HEADS * HEAD_DIM + 2 * SWA_KV_HEADS * HEAD_DIM + 3 * CONV_CHANNELS


def _params(*sem):
    return pltpu.CompilerParams(dimension_semantics=sem, vmem_limit_bytes=VMEM_LIMIT)


def _rms(x, g):
    return x * lax.rsqrt(jnp.mean(x * x, axis=-1, keepdims=True) + RMS_EPS) * g


def _dot(a, b):
    return jnp.dot(a, b, preferred_element_type=F32)


def _dot_nt(a, b):
    return lax.dot_general(a, b, (((1,), (1,)), ((), ())), preferred_element_type=F32)


def _rope(x, cos, sin_lo, sin_hi, half):
    return (x * cos + pltpu.roll(x, half, 1) * sin_hi
            + pltpu.roll(x, LANES - half, 1) * sin_lo)


def _l0_pre_kernel(x_ref, pos_ref, g_ref, w_ref, bf_ref, qn_ref, wq_ref, kvn_ref, wk_ref,
                   wv_ref, tri_ref, place_ref, invf_ref, mlo_ref, mhi_ref,
                   qf_ref, kf_ref, vtf_ref, q_ref, k_ref, vt_ref, ct_ref, carry_ref):
    @pl.when(pl.program_id(1) == 0)
    def _():
        carry_ref[...] = jnp.zeros_like(carry_ref)

    fox_w = FOX_HEADS * HEAD_DIM
    xn = _rms(x_ref[...], g_ref[...]).astype(BF16)
    y = _dot(xn, w_ref[...])
    qf_ref[...] = y[:, :fox_w].astype(BF16)
    vtf_ref[0] = y[:, 2 * fox_w:3 * fox_w].T.astype(BF16)
    o = QKV_W
    cq = y[:, o:o + MLA_Q_RANK]
    o += MLA_Q_RANK
    ckv = y[:, o:o + MLA_KV_RANK]
    o += MLA_KV_RANK
    kr = y[:, o:o + LANES]
    fl = y[:, o + LANES:o + 2 * LANES]

    q = _dot(_rms(cq, qn_ref[...]).astype(BF16), wq_ref[...])
    ckn = _rms(ckv, kvn_ref[...]).astype(BF16)
    kk = _dot(ckn, wk_ref[...])
    vt_ref[0] = _dot(ckn, wv_ref[...]).T.astype(BF16)

    ang = pos_ref[...].astype(F32) * invf_ref[...]
    cos = jnp.cos(ang)
    sin = jnp.sin(ang)
    sin_lo = sin * mlo_ref[...]
    sin_hi = sin * mhi_ref[...]
    half = MLA_ROPE_DIM // 2
    kpe = _rope(kr, cos, sin_lo, sin_hi, half)
    for h in range(MLA_HEADS):
        sl = slice(h * LANES, (h + 1) * LANES)
        q_ref[:, sl] = _rope(q[:, sl], cos, sin_lo, sin_hi, half).astype(BF16)
        k_ref[:, sl] = (kk[:, sl] + kpe).astype(BF16)

    z = fl + bf_ref[...]
    ls = jnp.minimum(z, 0.0) - jnp.log1p(jnp.exp(-jnp.abs(z)))
    tri = tri_ref[...]
    c = carry_ref[...]
    for part in _split3(ls):
        c = c + _dot(tri, part)
    tm = c.shape[0]
    carry_ref[...] = c[tm - 1:tm, :]
    c2 = c * LOG2E
    ct_ref[0] = c2.T[:FOX_HEADS, :]
    bias = None
    for i, part in enumerate(_split3(c2)):
        d = _dot(part, place_ref[i])
        bias = d if bias is None else bias + d
    for hp in range(FOX_HEADS // 2):
        kf_ref[:, 2 * hp * LANES:(2 * hp + 1) * LANES] = (
            y[:, fox_w + hp * LANES:fox_w + (hp + 1) * LANES].astype(BF16))
        kf_ref[:, (2 * hp + 1) * LANES:(2 * hp + 2) * LANES] = (
            bias[:, hp * LANES:(hp + 1) * LANES].astype(BF16))


def _split3(x):
    hi = x.astype(BF16)
    r1 = x - hi.astype(F32)
    mid = r1.astype(BF16)
    lo = (r1 - mid.astype(F32)).astype(BF16)
    return hi, mid, lo


def _l0_pre(x2, pos2, g, w_ext, bf_row, qn, wq, kvn, wk, wv, B, S, tm):
    N = B * S
    ns = S // tm
    tri = (jnp.arange(tm)[:, None] >= jnp.arange(tm)[None, :]).astype(BF16)
    n_pairs = FOX_HEADS // 2
    place = jnp.zeros((3, LANES, n_pairs * LANES), F32)
    for hp in range(n_pairs):
        for i in range(3):
            place = place.at[i, 2 * hp, hp * LANES + i].set(-1.0)
            place = place.at[i, 2 * hp + 1, hp * LANES + 3 + i].set(-1.0)
    place = place.astype(BF16)
    lane = jnp.arange(LANES)
    half = MLA_ROPE_DIM // 2
    in_pe = (lane >= MLA_NOPE_DIM) & (lane < MLA_NOPE_DIM + MLA_ROPE_DIM)
    freq = ROPE_THETA ** (-jnp.arange(0, MLA_ROPE_DIM, 2, dtype=F32) / MLA_ROPE_DIM)
    invf = jnp.where(in_pe, freq[(lane - MLA_NOPE_DIM) % half], 0.0)[None, :].astype(F32)
    m_lo = jnp.where(in_pe & (lane < MLA_NOPE_DIM + half), -1.0, 0.0)[None, :].astype(F32)
    m_hi = jnp.where(in_pe & (lane >= MLA_NOPE_DIM + half), 1.0, 0.0)[None, :].astype(F32)

    row = lambda b, s: (b * ns + s, 0)
    tcol = lambda b, s: (b, 0, s)
    full = lambda a: pl.BlockSpec(a.shape, lambda b, s: (0,) * a.ndim)
    fox_w = FOX_HEADS * HEAD_DIM
    vw = MLA_HEADS * MLA_V_DIM
    return pl.pallas_call(
        _l0_pre_kernel,
        grid=(B, ns),
        in_specs=[pl.BlockSpec((tm, D_MODEL), row), pl.BlockSpec((tm, 1), row),
                  full(g), full(w_ext), full(bf_row), full(qn), full(wq), full(kvn), full(wk),
                  full(wv), full(tri), full(place), full(invf), full(m_lo), full(m_hi)],
        out_specs=[pl.BlockSpec((tm, fox_w), row), pl.BlockSpec((tm, 2 * fox_w), row),
                   pl.BlockSpec((1, fox_w, tm), tcol),
                   pl.BlockSpec((tm, MLA_PAD), row), pl.BlockSpec((tm, MLA_PAD), row),
                   pl.BlockSpec((1, vw, tm), tcol),
                   pl.BlockSpec((1, FOX_HEADS, tm), tcol)],
        out_shape=[jax.ShapeDtypeStruct((N, fox_w), BF16),
                   jax.ShapeDtypeStruct((N, 2 * fox_w), BF16),
                   jax.ShapeDtypeStruct((B, fox_w, S), BF16),
                   jax.ShapeDtypeStruct((N, MLA_PAD), BF16),
                   jax.ShapeDtypeStruct((N, MLA_PAD), BF16),
                   jax.ShapeDtypeStruct((B, vw, S), BF16),
                   jax.ShapeDtypeStruct((B, FOX_HEADS, S), F32)],
        scratch_shapes=[pltpu.VMEM((1, LANES), F32)],
        compiler_params=_params("arbitrary", "arbitrary"),
        name="l0_pre",
    )(x2, pos2, g, w_ext, bf_row, qn, wq, kvn, wk, wv, tri, place, invf, m_lo, m_hi)


def _attn_kernel(*refs, fox, tq, tk, tw):
    if fox:
        q_ref, k_ref, vt_ref, cq_ref, ones_ref, o_ref, m_ref, l_ref, acc_ref = refs
    else:
        q_ref, k_ref, vt_ref, o_ref, m_ref, l_ref, acc_ref = refs
    qi = pl.program_id(2)
    n_strips = tq // tw

    q = q_ref[0]
    if fox:
        lane = lax.broadcasted_iota(jnp.int32, (1, LANES), 1)
        zero = jnp.zeros_like(q)
        qa = (jnp.concatenate([jnp.where(lane < HEAD_DIM, q, zero), ones_ref[0]], axis=1),
              jnp.concatenate([jnp.where(lane < HEAD_DIM, zero, q), ones_ref[1]], axis=1))
    else:
        qa = (q[:, :LANES], q[:, LANES:])

    m_ref[...] = jnp.full_like(m_ref, MASK_VALUE)
    l_ref[...] = jnp.zeros_like(l_ref)
    acc_ref[...] = jnp.zeros_like(acc_ref)

    sub = LANES

    def group(chunks):
        work = []
        for start, strips in chunks:
            kc = k_ref[0, pl.ds(start, tk), :]
            for i in range(2):
                for r, n_sub, mask_from in strips:
                    kh = kc[:n_sub * sub] if fox else kc[:n_sub * sub, i * LANES:(i + 1) * LANES]
                    s = _dot_nt(kh, qa[i][r * tw:(r + 1) * tw])
                    work.append((start, i, r, n_sub, mask_from, s))
        for start, i, r, n_sub, mask_from, s in work:
            tiles = []
            for c in range(n_sub):
                t = s[c * sub:(c + 1) * sub]
                if mask_from is not None and c >= mask_from:
                    keys = start + c * sub + lax.broadcasted_iota(jnp.int32, (sub, tw), 0)
                    qpos = qi * tq + r * tw + lax.broadcasted_iota(jnp.int32, (sub, tw), 1)
                    t = jnp.where(keys <= qpos, t, MASK_VALUE)
                tiles.append(t)
            col_max = tiles[0]
            for t in tiles[1:]:
                col_max = jnp.maximum(col_max, t)
            col_max = jnp.max(col_max, axis=0, keepdims=True)
            idx = i * n_strips + r
            m_prev = m_ref[idx]
            if fox:
                cq = cq_ref[0, 0, i:i + 1, r * tw:(r + 1) * tw]
                m_new = jnp.maximum(m_prev, col_max + cq)
                shift = m_new - cq
            else:
                m_new = jnp.maximum(m_prev, col_max)
                shift = m_new
            m_ref[idx] = m_new
            alpha = jnp.exp2(m_prev - m_new)
            psum = None
            probs = []
            for t in tiles:
                p = jnp.exp2(t - shift)
                psum = p if psum is None else psum + p
                probs.append(p.astype(BF16))
            l_ref[idx] = alpha * l_ref[idx] + jnp.sum(psum, axis=0, keepdims=True)
            vt = vt_ref[0, i * HEAD_DIM:(i + 1) * HEAD_DIM, pl.ds(start, n_sub * sub)]
            rows = slice(i * HEAD_DIM, (i + 1) * HEAD_DIM)
            cols = slice(r * tw, (r + 1) * tw)
            acc_ref[rows, cols] = (acc_ref[rows, cols] * alpha
                                   + _dot(vt, jnp.concatenate(probs, axis=0)))

    n_sub_full = tk // sub
    full_strips = [(r, n_sub_full, None) for r in range(n_strips)]
    per = tw // sub
    diag = []
    for d in range(tq // tk):
        strips = []
        for r in range(n_strips):
            n_sub = min(max((r + 1) * per - d * n_sub_full, 0), n_sub_full)
            if n_sub > 0:
                strips.append((r, n_sub, max(r * per - d * n_sub_full, 0)))
        diag.append((pl.multiple_of(qi * tq + d * tk, tk), strips))

    n_full = qi * (tq // tk)

    def full_group(j, carry):
        group([(pl.multiple_of((j * ATTN_GROUP + g) * tk, tk), full_strips)
               for g in range(ATTN_GROUP)])
        return carry

    lax.fori_loop(0, n_full // ATTN_GROUP, full_group, 0)
    for rem in range(ATTN_GROUP):
        @pl.when(n_full % ATTN_GROUP == rem)
        def _():
            base = n_full - rem
            group([(pl.multiple_of((base + g) * tk, tk), full_strips) for g in range(rem)] + diag)

    heads = []
    for i in range(2):
        rows = slice(i * HEAD_DIM, (i + 1) * HEAD_DIM)
        heads.append(jnp.concatenate(
            [acc_ref[rows, r * tw:(r + 1) * tw] * (1.0 / l_ref[i * n_strips + r])
             for r in range(n_strips)], axis=1))
    o_ref[0] = jnp.concatenate(heads, axis=0).T.astype(o_ref.dtype)


def _attention(q, k, vt, ct, B, S, tq, tk, tw, fox):
    n_pairs = FOX_HEADS // 2
    nq = S // tq
    qw = LANES if fox else 2 * LANES
    in_specs = [pl.BlockSpec((1, tq, qw), lambda b, h, i: (b, i, h)),
                pl.BlockSpec((1, S, 2 * LANES), lambda b, h, i: (b, 0, h)),
                pl.BlockSpec((1, LANES, S), lambda b, h, i: (b, h, 0))]
    args = [q, k, vt]
    if fox:
        in_specs.append(pl.BlockSpec((1, 1, 2, tq), lambda b, h, i: (b, h, 0, i)))
        args.append(ct)
        lane = jnp.arange(LANES)
        ones = jnp.stack([lane < 3, (lane >= 3) & (lane < 6)]).astype(BF16)
        in_specs.append(pl.BlockSpec((2, tq, LANES), lambda b, h, i: (0, 0, 0)))
        args.append(jnp.broadcast_to(ones[:, None, :], (2, tq, LANES)))
    n_strips = tq // tw
    return pl.pallas_call(
        functools.partial(_attn_kernel, fox=fox, tq=tq, tk=tk, tw=tw),
        grid=(B, n_pairs, nq),
        in_specs=in_specs,
        out_specs=pl.BlockSpec((1, tq, LANES), lambda b, h, i: (b, i, h)),
        out_shape=jax.ShapeDtypeStruct((B, S, n_pairs * LANES), BF16),
        scratch_shapes=[pltpu.VMEM((2 * n_strips, 1, tw), F32),
                        pltpu.VMEM((2 * n_strips, 1, tw), F32),
                        pltpu.VMEM((LANES, tq), F32)],
        compiler_params=_params("arbitrary", "arbitrary", "arbitrary"),
        name="fox_attn" if fox else "mla_attn",
    )(*args)


def _ffn_kernel(h_ref, a_ref, b_ref, wa_ref, wb_ref, g_ref, wg_ref, wu_ref, wd_ref,
                o_ref, hres_ref, xn_ref, acc_ref):
    f = pl.program_id(1)

    @pl.when(f == 0)
    def _():
        hres = h_ref[...] + _dot(a_ref[...], wa_ref[...]) + _dot(b_ref[...], wb_ref[...])
        hres_ref[...] = hres
        xn_ref[...] = _rms(hres, g_ref[...]).astype(BF16)
        acc_ref[...] = jnp.zeros_like(acc_ref)

    xn = xn_ref[...]
    gate = _dot(xn, wg_ref[...])
    up = _dot(xn, wu_ref[...])
    act = gate * (1.0 / (1.0 + jnp.exp(-gate))) * up
    acc_ref[...] += _dot(act.astype(BF16), wd_ref[...])

    @pl.when(f == pl.num_programs(1) - 1)
    def _():
        o_ref[...] = hres_ref[...] + acc_ref[...]


def _ffn(h, a, b, wa, wb, g, w_gate, w_up, w_down, tm, tf):
    N = h.shape[0]
    row = lambda i, f: (i, 0)
    full = lambda x: pl.BlockSpec(x.shape, lambda i, f: (0, 0))
    half_w = a.shape[1]
    return pl.pallas_call(
        _ffn_kernel,
        grid=(N // tm, D_FF // tf),
        in_specs=[pl.BlockSpec((tm, D_MODEL), row), pl.BlockSpec((tm, half_w), row),
                  pl.BlockSpec((tm, half_w), row), full(wa), full(wb), full(g),
                  pl.BlockSpec((D_MODEL, tf), lambda i, f: (0, f)),
                  pl.BlockSpec((D_MODEL, tf), lambda i, f: (0, f)),
                  pl.BlockSpec((tf, D_MODEL), lambda i, f: (f, 0))],
        out_specs=pl.BlockSpec((tm, D_MODEL), row),
        out_shape=jax.ShapeDtypeStruct((N, D_MODEL), F32),
        scratch_shapes=[pltpu.VMEM((tm, D_MODEL), F32), pltpu.VMEM((tm, D_MODEL), BF16),
                        pltpu.VMEM((tm, D_MODEL), F32)],
        compiler_params=_params("arbitrary", "arbitrary"),
        name="dense_ffn",
    )(h, a, b, wa, wb, g, w_gate, w_up, w_down)


def _router_kernel(h_ref, a_ref, b_ref, wa_ref, wb_ref, g_ref, wr_ref, tri_ref,
                   hres_ref, xn_ref, slab_ref, meta_ref, cnt_ref):
    @pl.when(pl.program_id(0) == 0)
    def _():
        cnt_ref[...] = jnp.zeros_like(cnt_ref)

    hres = h_ref[...] + _dot(a_ref[...], wa_ref[...]) + _dot(b_ref[...], wb_ref[...])
    hres_ref[...] = hres
    xn = _rms(hres, g_ref[...])
    xn_ref[...] = xn
    xh = xn.astype(BF16)
    xl = (xn - xh.astype(F32)).astype(BF16)
    wr = wr_ref[...]
    wh = wr.astype(BF16)
    wl = (wr - wh.astype(F32)).astype(BF16)
    logits = _dot(xh, wh) + (_dot(xh, wl) + _dot(xl, wh))
    lane = lax.broadcasted_iota(jnp.int32, logits.shape, 1)
    logits = jnp.where(lane < N_EXPERTS, logits, MASK_VALUE)
    m1 = jnp.max(logits, axis=1, keepdims=True)
    i1 = jnp.min(jnp.where(logits == m1, lane, LANES), axis=1, keepdims=True)
    rest = jnp.where(lane == i1, MASK_VALUE, logits)
    m2 = jnp.max(rest, axis=1, keepdims=True)
    i2 = jnp.min(jnp.where(rest == m2, lane, LANES), axis=1, keepdims=True)
    e2 = jnp.exp(m2 - m1)
    w1 = 1.0 / (1.0 + e2)
    w2 = e2 / (1.0 + e2)
    hot1 = lane == i1
    hot2 = lane == i2
    onehot = jnp.where(hot1 | hot2, 1.0, 0.0)
    before = _dot(tri_ref[...], onehot.astype(BF16)) + cnt_ref[...]
    r1 = jnp.sum(jnp.where(hot1, before, 0.0), axis=1, keepdims=True)
    r2 = jnp.sum(jnp.where(hot2, before, 0.0), axis=1, keepdims=True)
    cnt_ref[...] += jnp.sum(onehot, axis=0, keepdims=True)
    slab = (jnp.where(lane == 0, i1.astype(F32), 0.0) + jnp.where(lane == 1, i2.astype(F32), 0.0)
            + jnp.where(lane == 2, r1, 0.0) + jnp.where(lane == 3, r2, 0.0)
            + jnp.where(lane == 4, w1, 0.0) + jnp.where(lane == 5, w2, 0.0))
    slab_ref[...] = slab
    meta_ref[...] = slab.T[:8, :]


def _router(h, a, b, wa, wb, g, w_router, tm):
    N = h.shape[0]
    tri = (jnp.arange(tm)[:, None] > jnp.arange(tm)[None, :]).astype(BF16)
    row = lambda i: (i, 0)
    full = lambda x: pl.BlockSpec(x.shape, lambda i: (0, 0))
    half_w = a.shape[1]
    return pl.pallas_call(
        _router_kernel,
        grid=(N // tm,),
        in_specs=[pl.BlockSpec((tm, D_MODEL), row), pl.BlockSpec((tm, half_w), row),
                  pl.BlockSpec((tm, half_w), row), full(wa), full(wb), full(g), full(w_router),
                  full(tri)],
        out_specs=[pl.BlockSpec((tm, D_MODEL), row), pl.BlockSpec((tm, D_MODEL), row),
                   pl.BlockSpec((tm, LANES), row), pl.BlockSpec((8, tm), lambda i: (0, i)),
                   pl.BlockSpec((1, LANES), lambda i: (0, 0))],
        out_shape=[jax.ShapeDtypeStruct((N, D_MODEL), F32), jax.ShapeDtypeStruct((N, D_MODEL), F32),
                   jax.ShapeDtypeStruct((N, LANES), F32), jax.ShapeDtypeStruct((8, N), F32),
                   jax.ShapeDtypeStruct((1, LANES), F32)],
        compiler_params=_params("arbitrary"),
        name="moe_router",
    )(h, a, b, wa, wb, g, w_router, tri)


def _row_copy(src, src_row, dst, dst_row, sem):
    return pltpu.make_async_copy(src.at[pl.ds(src_row, 1), :], dst.at[pl.ds(dst_row, 1), :], sem)


def _dispatch_kernel(pos_hbm, xn_ref, init_hbm, xs_hbm, pos_smem, idx_sem, row_sem, *, tm):
    del init_hbm
    i = pl.program_id(0)
    idx_copy = pltpu.make_async_copy(pos_hbm.at[i], pos_smem, idx_sem)
    idx_copy.start()
    idx_copy.wait()

    def issue(j, carry):
        base = pl.multiple_of(j * 8, 8)
        for u in range(8):
            for k in range(2):
                _row_copy(xn_ref, base + u, xs_hbm, pos_smem[k * tm + base + u], row_sem).start()
        return carry

    lax.fori_loop(0, tm // 8, issue, 0)
    for k in range(2):
        pltpu.make_async_copy(xn_ref, xs_hbm.at[pl.ds(0, tm), :], row_sem).wait()


def _dispatch(pos_tiles, xn, n_rows, tm):
    N = xn.shape[0]
    init = jnp.zeros((n_rows, D_MODEL), F32)
    return pl.pallas_call(
        functools.partial(_dispatch_kernel, tm=tm),
        grid=(N // tm,),
        in_specs=[pl.BlockSpec(memory_space=pl.ANY),
                  pl.BlockSpec((tm, D_MODEL), lambda i: (i, 0)),
                  pl.BlockSpec(memory_space=pl.ANY)],
        out_specs=pl.BlockSpec(memory_space=pl.ANY),
        out_shape=jax.ShapeDtypeStruct((n_rows, D_MODEL), F32),
        scratch_shapes=[pltpu.SMEM((2 * tm,), jnp.int32), pltpu.SemaphoreType.DMA,
                        pltpu.SemaphoreType.DMA],
        input_output_aliases={2: 0},
        compiler_params=_params("arbitrary"),
        name="moe_dispatch",
    )(pos_tiles, xn, init)


def _expert_kernel(te_ref, tv_ref, x_ref, wg_ref, wu_ref, wd_ref, y_ref, xb_ref, acc_ref):
    del te_ref
    i = pl.program_id(0)
    f = pl.program_id(1)
    live = tv_ref[i] > 0

    @pl.when(live & (f == 0))
    def _():
        xb_ref[...] = x_ref[...].astype(BF16)
        acc_ref[...] = jnp.zeros_like(acc_ref)

    @pl.when(live)
    def _():
        xb = xb_ref[...]
        gate = _dot(xb, wg_ref[0])
        up = _dot(xb, wu_ref[0])
        act = gate * (1.0 / (1.0 + jnp.exp(-gate))) * up
        acc_ref[...] += _dot(act.astype(BF16), wd_ref[0])

    @pl.when(f == pl.num_programs(1) - 1)
    def _():
        y_ref[...] = jnp.where(live, acc_ref[...], 0.0)


def _experts(tile_expert, tile_valid, xs, w_gate, w_up, w_down, tr, tf):
    n_rows = xs.shape[0]
    nf = D_FF // tf
    fsel = lambda f, i, tv: jnp.where(tv[i] > 0, f, 0)
    grid_spec = pltpu.PrefetchScalarGridSpec(
        num_scalar_prefetch=2,
        grid=(n_rows // tr, nf),
        in_specs=[pl.BlockSpec((tr, D_MODEL), lambda i, f, te, tv: (i, 0)),
                  pl.BlockSpec((1, D_MODEL, tf), lambda i, f, te, tv: (te[i], 0, fsel(f, i, tv))),
                  pl.BlockSpec((1, D_MODEL, tf), lambda i, f, te, tv: (te[i], 0, fsel(f, i, tv))),
                  pl.BlockSpec((1, tf, D_MODEL), lambda i, f, te, tv: (te[i], fsel(f, i, tv), 0))],
        out_specs=pl.BlockSpec((tr, D_MODEL), lambda i, f, te, tv: (i, 0)),
        scratch_shapes=[pltpu.VMEM((tr, D_MODEL), BF16), pltpu.VMEM((tr, D_MODEL), F32)])
    return pl.pallas_call(
        _expert_kernel,
        grid_spec=grid_spec,
        out_shape=jax.ShapeDtypeStruct((n_rows, D_MODEL), F32),
        compiler_params=_params("arbitrary", "arbitrary"),
        name="moe_experts",
    )(tile_expert, tile_valid, xs, w_gate, w_up, w_down)


def _combine_kernel(pos_hbm, hres_ref, slab_ref, fn_ref, ys_hbm, o_ref, pos_smem, ybuf_ref,
                    idx_sem, row_sem, *, tm):
    i = pl.program_id(0)
    idx_copy = pltpu.make_async_copy(pos_hbm.at[i], pos_smem, idx_sem)
    idx_copy.start()
    idx_copy.wait()

    def issue(j, carry):
        base = pl.multiple_of(j * 8, 8)
        for u in range(8):
            for k in range(2):
                _row_copy(ys_hbm, pos_smem[k * tm + base + u], ybuf_ref.at[k], base + u, row_sem).start()
        return carry

    lax.fori_loop(0, tm // 8, issue, 0)
    for k in range(2):
        pltpu.make_async_copy(ys_hbm.at[pl.ds(0, tm), :], ybuf_ref.at[k], row_sem).wait()

    slab = slab_ref[...]
    lane = lax.broadcasted_iota(jnp.int32, slab.shape, 1)
    w1 = jnp.sum(jnp.where(lane == 4, slab, 0.0), axis=1, keepdims=True)
    w2 = jnp.sum(jnp.where(lane == 5, slab, 0.0), axis=1, keepdims=True)
    out = hres_ref[...] + w1 * ybuf_ref[0] + w2 * ybuf_ref[1]
    o_ref[...] = _rms(out, fn_ref[...])


def _combine(pos_tiles, hres, slab, final_g, ys, tm):
    N = hres.shape[0]
    row = lambda i: (i, 0)
    return pl.pallas_call(
        functools.partial(_combine_kernel, tm=tm),
        grid=(N // tm,),
        in_specs=[pl.BlockSpec(memory_space=pl.ANY), pl.BlockSpec((tm, D_MODEL), row),
                  pl.BlockSpec((tm, LANES), row), pl.BlockSpec(final_g.shape, lambda i: (0, 0)),
                  pl.BlockSpec(memory_space=pl.ANY)],
        out_specs=pl.BlockSpec((tm, D_MODEL), row),
        out_shape=jax.ShapeDtypeStruct((N, D_MODEL), F32),
        scratch_shapes=[pltpu.SMEM((2 * tm,), jnp.int32), pltpu.VMEM((2, tm, D_MODEL), F32),
                        pltpu.SemaphoreType.DMA, pltpu.SemaphoreType.DMA],
        compiler_params=_params("arbitrary"),
        name="moe_combine",
    )(pos_tiles, hres, slab, final_g, ys)


def _moe(h, a, b, wa, wb, g, w_router, w_gate, w_up, w_down, final_g):
    N = h.shape[0]
    tm, tr = MOE_TM, MOE_TR
    hres, xn, slab, meta, cnt = _router(h, a, b, wa, wb, g, w_router, tm)
    e1, e2, r1, r2 = (meta[j].astype(jnp.int32) for j in range(4))
    counts = cnt[0, :N_EXPERTS].astype(jnp.int32)
    cap = (counts + tr - 1) // tr * tr
    ends = jnp.cumsum(cap)
    offs = ends - cap
    pos = jnp.stack([offs[e1] + r1, offs[e2] + r2])
    pos_tiles = pos.reshape(2, N // tm, tm).transpose(1, 0, 2).reshape(N // tm, 2 * tm)
    n_rows = 2 * N + N_EXPERTS * tr
    tile_start = jnp.arange(n_rows // tr, dtype=jnp.int32) * tr
    tile_expert = jnp.minimum(jnp.sum(tile_start[:, None] >= ends[None, :], axis=1),
                              N_EXPERTS - 1).astype(jnp.int32)
    tile_valid = jnp.clip(counts[tile_expert] - (tile_start - offs[tile_expert]), 0, tr)
    tile_valid = jnp.where(tile_start < ends[N_EXPERTS - 1], tile_valid, 0).astype(jnp.int32)
    xs = _dispatch(pos_tiles, xn, n_rows, tm)
    ys = _experts(tile_expert, tile_valid, xs, w_gate, w_up, w_down, tr, MOE_TF)
    return _combine(pos_tiles, hres, slab, final_g, ys, tm)


def _l1_pre_kernel(x_ref, pos_ref, g_ref, w_ref, cw_ref, invf_ref, mlo_ref, mhi_ref,
                   q_ref, kv_ref, oc_ref, tail_ref):
    @pl.when(pl.program_id(1) == 0)
    def _():
        tail_ref[...] = jnp.zeros_like(tail_ref)

    xn = _rms(x_ref[...], g_ref[...]).astype(BF16)
    y = _dot(xn, w_ref[...])
    ang = pos_ref[...].astype(F32) * invf_ref[...]
    cos = jnp.cos(ang)
    sin = jnp.sin(ang)
    sin_lo = sin * mlo_ref[...]
    sin_hi = sin * mhi_ref[...]
    half = HEAD_DIM // 2
    qw = SWA_HEADS * HEAD_DIM
    for j in range(qw // LANES):
        sl = slice(j * LANES, (j + 1) * LANES)
        q_ref[:, sl] = _rope(y[:, sl], cos, sin_lo, sin_hi, half).astype(BF16)
    kv_ref[:, :LANES] = _rope(y[:, qw:qw + LANES], cos, sin_lo, sin_hi, half).astype(BF16)
    kv_ref[:, LANES:] = y[:, qw + LANES:qw + 2 * LANES].astype(BF16)

    o = qw + 2 * LANES
    C = CONV_CHANNELS
    gate_b = y[:, o:o + C]
    u = y[:, o + C:o + 2 * C] * y[:, o + 2 * C:o + 3 * C]
    tm = u.shape[0]
    rows = lax.broadcasted_iota(jnp.int32, u.shape, 0)
    tail = tail_ref[...]
    u1 = jnp.where(rows == 0, tail[7:8, :], pltpu.roll(u, 1, 0))
    u2 = jnp.where(rows == 0, tail[6:7, :], jnp.where(rows == 1, tail[7:8, :], pltpu.roll(u, 2, 0)))
    cw = cw_ref[...]
    oc_ref[...] = (gate_b * (cw[0:1, :] * u2 + cw[1:2, :] * u1 + cw[2:3, :] * u)).astype(BF16)
    tail_ref[...] = u[tm - 8:tm, :]


def _l1_pre(x2, pos2, g, w_in, conv_w, B, S, tm):
    N = B * S
    ns = S // tm
    lane = jnp.arange(LANES)
    half = HEAD_DIM // 2
    freq = ROPE_THETA ** (-jnp.arange(0, HEAD_DIM, 2, dtype=F32) / HEAD_DIM)
    invf = freq[lane % half][None, :].astype(F32)
    m_lo = jnp.where(lane % HEAD_DIM < half, -1.0, 0.0)[None, :].astype(F32)
    m_hi = jnp.where(lane % HEAD_DIM >= half, 1.0, 0.0)[None, :].astype(F32)
    row = lambda b, s: (b * ns + s, 0)
    const = lambda b, s: (0, 0)
    full = lambda a: pl.BlockSpec(a.shape, const)
    return pl.pallas_call(
        _l1_pre_kernel,
        grid=(B, ns),
        in_specs=[pl.BlockSpec((tm, D_MODEL), row), pl.BlockSpec((tm, 1), row),
                  full(g), full(w_in), full(conv_w), full(invf), full(m_lo), full(m_hi)],
        out_specs=[pl.BlockSpec((tm, SWA_HEADS * HEAD_DIM), row),
                   pl.BlockSpec((tm, 2 * LANES), row),
                   pl.BlockSpec((tm, CONV_CHANNELS), row)],
        out_shape=[jax.ShapeDtypeStruct((N, SWA_HEADS * HEAD_DIM), BF16),
                   jax.ShapeDtypeStruct((N, 2 * LANES), BF16),
                   jax.ShapeDtypeStruct((N, CONV_CHANNELS), BF16)],
        scratch_shapes=[pltpu.VMEM((8, CONV_CHANNELS), F32)],
        compiler_params=_params("arbitrary", "arbitrary"),
        name="l1_pre",
    )(x2, pos2, g, w_in, conv_w, invf, m_lo, m_hi)


def _swa_kernel(sink_ref, q_ref, kv_ref, kvp_ref, o_ref, *, tq):
    s_idx = pl.program_id(1)
    first = lax.broadcasted_iota(jnp.int32, (1, LANES), 1) < HEAD_DIM
    G = SWA_HEADS // SWA_KV_HEADS
    W = WINDOW

    def dup(x):
        r = pltpu.roll(x, HEAD_DIM, 1)
        return (jnp.where(first, x, r), jnp.where(first, r, x))

    kv = kv_ref[0]
    kvp = kvp_ref[0]
    k_all = jnp.concatenate([kvp[:, :LANES], kv[:, :LANES]], axis=0)
    v_all = jnp.concatenate([kvp[:, LANES:], kv[:, LANES:]], axis=0)
    k_dup = dup(k_all)
    v_dup = dup(v_all)
    q = q_ref[0]
    zero = jnp.zeros((W, LANES), q.dtype)
    rel = (lax.broadcasted_iota(jnp.int32, (W, 2 * W), 0) + W
           - lax.broadcasted_iota(jnp.int32, (W, 2 * W), 1))
    in_window = (rel >= 0) & (rel < W)
    for r in range(tq // W):
        key_pos = (s_idx * tq + r * W - W
                   + lax.broadcasted_iota(jnp.int32, (W, 2 * W), 1))
        keep = in_window & (key_pos >= 0)
        keep4 = jnp.concatenate([keep] * (2 * (G // 2)), axis=0)
        for g in range(SWA_KV_HEADS):
            kb = k_dup[g][r * W:r * W + 2 * W]
            vb = v_dup[g][r * W:r * W + 2 * W]
            parts = []
            sinks = []
            for j in range(G // 2):
                blk = g * (G // 2) + j
                q2 = q[r * W:(r + 1) * W, blk * LANES:(blk + 1) * LANES]
                parts += [jnp.where(first, q2, zero), jnp.where(first, zero, q2)]
                sinks += [jnp.full((W, 1), sink_ref[2 * blk], F32),
                          jnp.full((W, 1), sink_ref[2 * blk + 1], F32)]
            qst = jnp.concatenate(parts, axis=0)
            sink = jnp.concatenate(sinks, axis=0)
            s = _dot_nt(qst, kb)
            s = jnp.where(keep4, s, MASK_VALUE)
            m = jnp.maximum(jnp.max(s, axis=1, keepdims=True), sink)
            e = jnp.exp(s - m)
            den = jnp.sum(e, axis=1, keepdims=True) + jnp.exp(sink - m)
            o = _dot((e * (1.0 / den)).astype(BF16), vb)
            for j in range(G // 2):
                blk = g * (G // 2) + j
                oa = o[(2 * j) * W:(2 * j + 1) * W]
                ob = o[(2 * j + 1) * W:(2 * j + 2) * W]
                o_ref[0, r * W:(r + 1) * W, blk * LANES:(blk + 1) * LANES] = (
                    jnp.where(first, oa, ob).astype(o_ref.dtype))


def _swa(sinks, q, kv, B, S, tq):
    ns = S // tq
    per = tq // WINDOW
    return pl.pallas_call(
        functools.partial(_swa_kernel, tq=tq),
        grid=(B, ns),
        in_specs=[pl.BlockSpec(memory_space=pltpu.SMEM),
                  pl.BlockSpec((1, tq, SWA_HEADS * HEAD_DIM), lambda b, s: (b, s, 0)),
                  pl.BlockSpec((1, tq, 2 * LANES), lambda b, s: (b, s, 0)),
                  pl.BlockSpec((1, WINDOW, 2 * LANES),
                               lambda b, s: (b, jnp.maximum(s * per - 1, 0), 0))],
        out_specs=pl.BlockSpec((1, tq, SWA_HEADS * HEAD_DIM), lambda b, s: (b, s, 0)),
        out_shape=jax.ShapeDtypeStruct((B, S, SWA_HEADS * HEAD_DIM), BF16),
        compiler_params=_params("arbitrary", "arbitrary"),
        name="swa_attn",
    )(sinks, q, kv, kv)


def _pad_heads(w, n_heads, width, offset=0):
    K = w.shape[0]
    w3 = w.reshape(K, n_heads, width)
    out = jnp.zeros((K, n_heads, LANES), w.dtype).at[:, :, offset:offset + width].set(w3)
    return out.reshape(K, n_heads * LANES)


def kernel(x, positions, l0_norm_mix, l0_w_in, l0_b_forget, l0_q_norm, l0_w_q_up, l0_kv_norm,
           l0_w_kv_up, l0_w_out, l0_norm_ffn, l0_w_gate, l0_w_up, l0_w_down,
           l1_norm_mix, l1_w_in, l1_sinks, l1_conv_w, l1_w_out, l1_norm_ffn, l1_w_router,
           l1_w_gate, l1_w_up, l1_w_down, final_norm):
    B, S, D = x.shape
    assert D == D_MODEL and S % 512 == 0
    N = B * S
    tm = 512
    x2 = x.reshape(N, D)
    pos2 = positions.reshape(N, 1)
    rowv = lambda v: v.reshape(1, -1).astype(F32)

    fox_w = FOX_HEADS * HEAD_DIM
    o = 3 * fox_w
    w_fq = l0_w_in[:, :fox_w] * (HEAD_DIM ** -0.5 * LOG2E)
    w_f = l0_w_in[:, o:o + FOX_HEADS]
    o += FOX_HEADS
    w_cq = l0_w_in[:, o:o + MLA_Q_RANK]
    o += MLA_Q_RANK
    w_ckv = l0_w_in[:, o:o + MLA_KV_RANK]
    o += MLA_KV_RANK
    w_kr = l0_w_in[:, o:o + MLA_ROPE_DIM]
    w_ext = jnp.concatenate(
        [w_fq, l0_w_in[:, fox_w:3 * fox_w], w_cq, w_ckv,
         _pad_heads(w_kr, 1, MLA_ROPE_DIM, MLA_NOPE_DIM), _pad_heads(w_f, 1, FOX_HEADS)],
        axis=1).astype(BF16)
    bf_row = jnp.zeros((1, LANES), F32).at[0, :FOX_HEADS].set(l0_b_forget.astype(F32))
    qk_dim = MLA_NOPE_DIM + MLA_ROPE_DIM
    wq = _pad_heads(l0_w_q_up * (qk_dim ** -0.5 * LOG2E), MLA_HEADS, qk_dim).astype(BF16)
    kv3 = l0_w_kv_up.reshape(MLA_KV_RANK, MLA_HEADS, MLA_NOPE_DIM + MLA_V_DIM)
    wk = _pad_heads(kv3[:, :, :MLA_NOPE_DIM].reshape(MLA_KV_RANK, -1), MLA_HEADS,
                    MLA_NOPE_DIM).astype(BF16)
    wv = kv3[:, :, MLA_NOPE_DIM:].reshape(MLA_KV_RANK, -1).astype(BF16)

    q_f, k_f, vt_f, q_m, k_m, vt_m, ct = _l0_pre(
        x2, pos2, rowv(l0_norm_mix), w_ext, bf_row, rowv(l0_q_norm), wq, rowv(l0_kv_norm),
        wk, wv, B, S, tm)

    tq, tk, tw = ATTN_TQ, ATTN_TK, ATTN_TW
    o_fox = _attention(q_f.reshape(B, S, fox_w), k_f.reshape(B, S, 2 * fox_w), vt_f,
                       ct.reshape(B, FOX_HEADS // 2, 2, S), B, S, tq, tk, tw, True)
    o_mla = _attention(q_m.reshape(B, S, MLA_PAD), k_m.reshape(B, S, MLA_PAD), vt_m, None,
                       B, S, tq, tk, tw, False)

    w_out0 = l0_w_out.astype(BF16)
    h1 = _ffn(x2, o_fox.reshape(N, -1), o_mla.reshape(N, -1), w_out0[:fox_w], w_out0[fox_w:],
              rowv(l0_norm_ffn), l0_w_gate.astype(BF16), l0_w_up.astype(BF16),
              l0_w_down.astype(BF16), FFN_TM, FFN_TF)

    swa_w = SWA_HEADS * HEAD_DIM
    w_in1 = jnp.concatenate([l1_w_in[:, :swa_w] * (HEAD_DIM ** -0.5), l1_w_in[:, swa_w:]],
                            axis=1).astype(BF16)
    q_s, kv_s, o_conv = _l1_pre(h1, pos2, rowv(l1_norm_mix), w_in1, l1_conv_w.astype(F32),
                                B, S, tm)
    o_swa = _swa(l1_sinks.astype(F32), q_s.reshape(B, S, swa_w), kv_s.reshape(B, S, 2 * LANES),
                 B, S, 512)
    w_out1 = l1_w_out.astype(BF16)
    w_router = jnp.zeros((D_MODEL, LANES), F32).at[:, :N_EXPERTS].set(l1_w_router.astype(F32))
    out = _moe(h1, o_swa.reshape(N, -1), o_conv, w_out1[:swa_w], w_out1[swa_w:],
               rowv(l1_norm_ffn), w_router, l1_w_gate.astype(BF16), l1_w_up.astype(BF16),
               l1_w_down.astype(BF16), rowv(final_norm))
    return out.reshape(B, S, D)
```

```python
import functools

import jax
import jax.numpy as jnp
from jax import lax
from jax.experimental import pallas as pl
from jax.experimental.pallas import tpu as pltpu

F32 = jnp.float32
BF16 = jnp.bfloat16

D_MODEL = 1024
HEAD_DIM = 64
RMS_EPS = 1e-6
ROPE_THETA = 10000.0
FOX_HEADS = 8
MLA_HEADS = 8
MLA_Q_RANK = 384
MLA_KV_RANK = 256
MLA_NOPE_DIM = 64
MLA_ROPE_DIM = 32
MLA_V_DIM = 64
SWA_HEADS = 8
SWA_KV_HEADS = 2
WINDOW = 128
CONV_CHANNELS = 512
CONV_WIDTH = 3
D_FF = 3584
N_EXPERTS = 8

LANES = 128
MASK_VALUE = -1e30
LOG2E = 1.4426950408889634
VMEM_LIMIT = 56 * 1024 * 1024

QKV_W = 3 * FOX_HEADS * HEAD_DIM
MLA_PAD = MLA_HEADS * LANES
L0_COLS = QKV_W + MLA_Q_RANK + MLA_KV_RANK + 2 * LANES
ATTN_TQ = 1024
ATTN_TK = 512
ATTN_TW = 256
ATTN_GROUP = 2
MOE_TM = 512
MOE_TR = 1024
MOE_TF = 896
FFN_TM = 1024
FFN_TF = 896
L1_COLS = SWA_HEADS * HEAD_DIM + 2 * SWA_KV_HEADS * HEAD_DIM + 3 * CONV_CHANNELS


def _params(*sem):
    return pltpu.CompilerParams(dimension_semantics=sem, vmem_limit_bytes=VMEM_LIMIT)


def _rms(x, g):
    return x * lax.rsqrt(jnp.mean(x * x, axis=-1, keepdims=True) + RMS_EPS) * g


def _dot(a, b):
    return jnp.dot(a, b, preferred_element_type=F32)


def _dot_nt(a, b):
    return lax.dot_general(a, b, (((1,), (1,)), ((), ())), preferred_element_type=F32)


def _rope(x, cos, sin_lo, sin_hi, half):
    return (x * cos + pltpu.roll(x, half, 1) * sin_hi
            + pltpu.roll(x, LANES - half, 1) * sin_lo)


def _l0_pre_kernel(x_ref, pos_ref, g_ref, w_ref, bf_ref, qn_ref, wq_ref, kvn_ref, wk_ref,
                   wv_ref, tri_ref, place_ref, invf_ref, mlo_ref, mhi_ref,
                   qf_ref, kf_ref, vtf_ref, q_ref, k_ref, vt_ref, ct_ref, carry_ref):
    @pl.when(pl.program_id(1) == 0)
    def _():
        carry_ref[...] = jnp.zeros_like(carry_ref)

    fox_w = FOX_HEADS * HEAD_DIM
    xn = _rms(x_ref[...], g_ref[...]).astype(BF16)
    y = _dot(xn, w_ref[...])
    qf_ref[...] = y[:, :fox_w].astype(BF16)
    vtf_ref[0] = y[:, 2 * fox_w:3 * fox_w].T.astype(BF16)
    o = QKV_W
    cq = y[:, o:o + MLA_Q_RANK]
    o += MLA_Q_RANK
    ckv = y[:, o:o + MLA_KV_RANK]
    o += MLA_KV_RANK
    kr = y[:, o:o + LANES]
    fl = y[:, o + LANES:o + 2 * LANES]

    q = _dot(_rms(cq, qn_ref[...]).astype(BF16), wq_ref[...])
    ckn = _rms(ckv, kvn_ref[...]).astype(BF16)
    kk = _dot(ckn, wk_ref[...])
    vt_ref[0] = _dot(ckn, wv_ref[...]).T.astype(BF16)

    ang = pos_ref[...].astype(F32) * invf_ref[...]
    cos = jnp.cos(ang)
    sin = jnp.sin(ang)
    sin_lo = sin * mlo_ref[...]
    sin_hi = sin * mhi_ref[...]
    half = MLA_ROPE_DIM // 2
    kpe = _rope(kr, cos, sin_lo, sin_hi, half)
    for h in range(MLA_HEADS):
        sl = slice(h * LANES, (h + 1) * LANES)
        q_ref[:, sl] = _rope(q[:, sl], cos, sin_lo, sin_hi, half).astype(BF16)
        k_ref[:, sl] = (kk[:, sl] + kpe).astype(BF16)

    z = fl + bf_ref[...]
    ls = jnp.minimum(z, 0.0) - jnp.log1p(jnp.exp(-jnp.abs(z)))
    tri = tri_ref[...]
    c = carry_ref[...]
    for part in _split3(ls):
        c = c + _dot(tri, part)
    tm = c.shape[0]
    carry_ref[...] = c[tm - 1:tm, :]
    c2 = c * LOG2E
    ct_ref[0] = c2.T[:FOX_HEADS, :]
    bias = None
    for i, part in enumerate(_split3(c2)):
        d = _dot(part, place_ref[i])
        bias = d if bias is None else bias + d
    for hp in range(FOX_HEADS // 2):
        kf_ref[:, 2 * hp * LANES:(2 * hp + 1) * LANES] = (
            y[:, fox_w + hp * LANES:fox_w + (hp + 1) * LANES].astype(BF16))
        kf_ref[:, (2 * hp + 1) * LANES:(2 * hp + 2) * LANES] = (
            bias[:, hp * LANES:(hp + 1) * LANES].astype(BF16))


def _split3(x):
    hi = x.astype(BF16)
    r1 = x - hi.astype(F32)
    mid = r1.astype(BF16)
    lo = (r1 - mid.astype(F32)).astype(BF16)
    return hi, mid, lo


def _l0_pre(x2, pos2, g, w_ext, bf_row, qn, wq, kvn, wk, wv, B, S, tm):
    N = B * S
    ns = S // tm
    tri = (jnp.arange(tm)[:, None] >= jnp.arange(tm)[None, :]).astype(BF16)
    n_pairs = FOX_HEADS // 2
    place = jnp.zeros((3, LANES, n_pairs * LANES), F32)
    for hp in range(n_pairs):
        for i in range(3):
            place = place.at[i, 2 * hp, hp * LANES + i].set(-1.0)
            place = place.at[i, 2 * hp + 1, hp * LANES + 3 + i].set(-1.0)
    place = place.astype(BF16)
    lane = jnp.arange(LANES)
    half = MLA_ROPE_DIM // 2
    in_pe = (lane >= MLA_NOPE_DIM) & (lane < MLA_NOPE_DIM + MLA_ROPE_DIM)
    freq = ROPE_THETA ** (-jnp.arange(0, MLA_ROPE_DIM, 2, dtype=F32) / MLA_ROPE_DIM)
    invf = jnp.where(in_pe, freq[(lane - MLA_NOPE_DIM) % half], 0.0)[None, :].astype(F32)
    m_lo = jnp.where(in_pe & (lane < MLA_NOPE_DIM + half), -1.0, 0.0)[None, :].astype(F32)
    m_hi = jnp.where(in_pe & (lane >= MLA_NOPE_DIM + half), 1.0, 0.0)[None, :].astype(F32)

    row = lambda b, s: (b * ns + s, 0)
    tcol = lambda b, s: (b, 0, s)
    full = lambda a: pl.BlockSpec(a.shape, lambda b, s: (0,) * a.ndim)
    fox_w = FOX_HEADS * HEAD_DIM
    vw = MLA_HEADS * MLA_V_DIM
    return pl.pallas_call(
        _l0_pre_kernel,
        grid=(B, ns),
        in_specs=[pl.BlockSpec((tm, D_MODEL), row), pl.BlockSpec((tm, 1), row),
                  full(g), full(w_ext), full(bf_row), full(qn), full(wq), full(kvn), full(wk),
                  full(wv), full(tri), full(place), full(invf), full(m_lo), full(m_hi)],
        out_specs=[pl.BlockSpec((tm, fox_w), row), pl.BlockSpec((tm, 2 * fox_w), row),
                   pl.BlockSpec((1, fox_w, tm), tcol),
                   pl.BlockSpec((tm, MLA_PAD), row), pl.BlockSpec((tm, MLA_PAD), row),
                   pl.BlockSpec((1, vw, tm), tcol),
                   pl.BlockSpec((1, FOX_HEADS, tm), tcol)],
        out_shape=[jax.ShapeDtypeStruct((N, fox_w), BF16),
                   jax.ShapeDtypeStruct((N, 2 * fox_w), BF16),
                   jax.ShapeDtypeStruct((B, fox_w, S), BF16),
                   jax.ShapeDtypeStruct((N, MLA_PAD), BF16),
                   jax.ShapeDtypeStruct((N, MLA_PAD), BF16),
                   jax.ShapeDtypeStruct((B, vw, S), BF16),
                   jax.ShapeDtypeStruct((B, FOX_HEADS, S), F32)],
        scratch_shapes=[pltpu.VMEM((1, LANES), F32)],
        compiler_params=_params("arbitrary", "arbitrary"),
        name="l0_pre",
    )(x2, pos2, g, w_ext, bf_row, qn, wq, kvn, wk, wv, tri, place, invf, m_lo, m_hi)


def _attn_kernel(*refs, fox, tq, tk, tw):
    if fox:
        q_ref, k_ref, vt_ref, cq_ref, ones_ref, o_ref, m_ref, l_ref, acc_ref = refs
    else:
        q_ref, k_ref, vt_ref, o_ref, m_ref, l_ref, acc_ref = refs
    qi = pl.program_id(2)
    n_strips = tq // tw

    q = q_ref[0]
    if fox:
        lane = lax.broadcasted_iota(jnp.int32, (1, LANES), 1)
        zero = jnp.zeros_like(q)
        qa = (jnp.concatenate([jnp.where(lane < HEAD_DIM, q, zero), ones_ref[0]], axis=1),
              jnp.concatenate([jnp.where(lane < HEAD_DIM, zero, q), ones_ref[1]], axis=1))
    else:
        qa = (q[:, :LANES], q[:, LANES:])

    m_ref[...] = jnp.full_like(m_ref, MASK_VALUE)
    l_ref[...] = jnp.zeros_like(l_ref)
    acc_ref[...] = jnp.zeros_like(acc_ref)

    sub = LANES

    def group(chunks):
        work = []
        for start, strips in chunks:
            kc = k_ref[0, pl.ds(start, tk), :]
            for i in range(2):
                for r, n_sub, mask_from in strips:
                    kh = kc[:n_sub * sub] if fox else kc[:n_sub * sub, i * LANES:(i + 1) * LANES]
                    s = _dot_nt(kh, qa[i][r * tw:(r + 1) * tw])
                    work.append((start, i, r, n_sub, mask_from, s))
        for start, i, r, n_sub, mask_from, s in work:
            tiles = []
            for c in range(n_sub):
                t = s[c * sub:(c + 1) * sub]
                if mask_from is not None and c >= mask_from:
                    keys = start + c * sub + lax.broadcasted_iota(jnp.int32, (sub, tw), 0)
                    qpos = qi * tq + r * tw + lax.broadcasted_iota(jnp.int32, (sub, tw), 1)
                    t = jnp.where(keys <= qpos, t, MASK_VALUE)
                tiles.append(t)
            col_max = tiles[0]
            for t in tiles[1:]:
                col_max = jnp.maximum(col_max, t)
            col_max = jnp.max(col_max, axis=0, keepdims=True)
            idx = i * n_strips + r
            m_prev = m_ref[idx]
            if fox:
                cq = cq_ref[0, 0, i:i + 1, r * tw:(r + 1) * tw]
                m_new = jnp.maximum(m_prev, col_max + cq)
                shift = m_new - cq
            else:
                m_new = jnp.maximum(m_prev, col_max)
                shift = m_new
            m_ref[idx] = m_new
            alpha = jnp.exp2(m_prev - m_new)
            psum = None
            probs = []
            for t in tiles:
                p = jnp.exp2(t - shift)
                psum = p if psum is None else psum + p
                probs.append(p.astype(BF16))
            l_ref[idx] = alpha * l_ref[idx] + jnp.sum(psum, axis=0, keepdims=True)
            vt = vt_ref[0, i * HEAD_DIM:(i + 1) * HEAD_DIM, pl.ds(start, n_sub * sub)]
            rows = slice(i * HEAD_DIM, (i + 1) * HEAD_DIM)
            cols = slice(r * tw, (r + 1) * tw)
            acc_ref[rows, cols] = (acc_ref[rows, cols] * alpha
                                   + _dot(vt, jnp.concatenate(probs, axis=0)))

    n_sub_full = tk // sub
    full_strips = [(r, n_sub_full, None) for r in range(n_strips)]
    per = tw // sub
    diag = []
    for d in range(tq // tk):
        strips = []
        for r in range(n_strips):
            n_sub = min(max((r + 1) * per - d * n_sub_full, 0), n_sub_full)
            if n_sub > 0:
                strips.append((r, n_sub, max(r * per - d * n_sub_full, 0)))
        diag.append((pl.multiple_of(qi * tq + d * tk, tk), strips))

    n_full = qi * (tq // tk)

    def full_group(j, carry):
        group([(pl.multiple_of((j * ATTN_GROUP + g) * tk, tk), full_strips)
               for g in range(ATTN_GROUP)])
        return carry

    lax.fori_loop(0, n_full // ATTN_GROUP, full_group, 0)
    for rem in range(ATTN_GROUP):
        @pl.when(n_full % ATTN_GROUP == rem)
        def _():
            base = n_full - rem
            group([(pl.multiple_of((base + g) * tk, tk), full_strips) for g in range(rem)] + diag)

    heads = []
    for i in range(2):
        rows = slice(i * HEAD_DIM, (i + 1) * HEAD_DIM)
        heads.append(jnp.concatenate(
            [acc_ref[rows, r * tw:(r + 1) * tw] * (1.0 / l_ref[i * n_strips + r])
             for r in range(n_strips)], axis=1))
    o_ref[0] = jnp.concatenate(heads, axis=0).T.astype(o_ref.dtype)


def _attention(q, k, vt, ct, B, S, tq, tk, tw, fox):
    n_pairs = FOX_HEADS // 2
    nq = S // tq
    qw = LANES if fox else 2 * LANES
    in_specs = [pl.BlockSpec((1, tq, qw), lambda b, h, i: (b, i, h)),
                pl.BlockSpec((1, S, 2 * LANES), lambda b, h, i: (b, 0, h)),
                pl.BlockSpec((1, LANES, S), lambda b, h, i: (b, h, 0))]
    args = [q, k, vt]
    if fox:
        in_specs.append(pl.BlockSpec((1, 1, 2, tq), lambda b, h, i: (b, h, 0, i)))
        args.append(ct)
        lane = jnp.arange(LANES)
        ones = jnp.stack([lane < 3, (lane >= 3) & (lane < 6)]).astype(BF16)
        in_specs.append(pl.BlockSpec((2, tq, LANES), lambda b, h, i: (0, 0, 0)))
        args.append(jnp.broadcast_to(ones[:, None, :], (2, tq, LANES)))
    n_strips = tq // tw
    return pl.pallas_call(
        functools.partial(_attn_kernel, fox=fox, tq=tq, tk=tk, tw=tw),
        grid=(B, n_pairs, nq),
        in_specs=in_specs,
        out_specs=pl.BlockSpec((1, tq, LANES), lambda b, h, i: (b, i, h)),
        out_shape=jax.ShapeDtypeStruct((B, S, n_pairs * LANES), BF16),
        scratch_shapes=[pltpu.VMEM((2 * n_strips, 1, tw), F32),
                        pltpu.VMEM((2 * n_strips, 1, tw), F32),
                        pltpu.VMEM((LANES, tq), F32)],
        compiler_params=_params("arbitrary", "arbitrary", "arbitrary"),
        name="fox_attn" if fox else "mla_attn",
    )(*args)


def _ffn_kernel(h_ref, a_ref, b_ref, wa_ref, wb_ref, g_ref, wg_ref, wu_ref, wd_ref,
                o_ref, hres_ref, xn_ref, acc_ref):
    f = pl.program_id(1)

    @pl.when(f == 0)
    def _():
        hres = h_ref[...] + _dot(a_ref[...], wa_ref[...]) + _dot(b_ref[...], wb_ref[...])
        hres_ref[...] = hres
        xn_ref[...] = _rms(hres, g_ref[...]).astype(BF16)
        acc_ref[...] = jnp.zeros_like(acc_ref)

    xn = xn_ref[...]
    gate = _dot(xn, wg_ref[...])
    up = _dot(xn, wu_ref[...])
    act = gate * (1.0 / (1.0 + jnp.exp(-gate))) * up
    acc_ref[...] += _dot(act.astype(BF16), wd_ref[...])

    @pl.when(f == pl.num_programs(1) - 1)
    def _():
        o_ref[...] = hres_ref[...] + acc_ref[...]


def _ffn(h, a, b, wa, wb, g, w_gate, w_up, w_down, tm, tf):
    N = h.shape[0]
    row = lambda i, f: (i, 0)
    full = lambda x: pl.BlockSpec(x.shape, lambda i, f: (0, 0))
    half_w = a.shape[1]
    return pl.pallas_call(
        _ffn_kernel,
        grid=(N // tm, D_FF // tf),
        in_specs=[pl.BlockSpec((tm, D_MODEL), row), pl.BlockSpec((tm, half_w), row),
                  pl.BlockSpec((tm, half_w), row), full(wa), full(wb), full(g),
                  pl.BlockSpec((D_MODEL, tf), lambda i, f: (0, f)),
                  pl.BlockSpec((D_MODEL, tf), lambda i, f: (0, f)),
                  pl.BlockSpec((tf, D_MODEL), lambda i, f: (f, 0))],
        out_specs=pl.BlockSpec((tm, D_MODEL), row),
        out_shape=jax.ShapeDtypeStruct((N, D_MODEL), F32),
        scratch_shapes=[pltpu.VMEM((tm, D_MODEL), F32), pltpu.VMEM((tm, D_MODEL), BF16),
                        pltpu.VMEM((tm, D_MODEL), F32)],
        compiler_params=_params("arbitrary", "arbitrary"),
        name="dense_ffn",
    )(h, a, b, wa, wb, g, w_gate, w_up, w_down)


def _router_kernel(h_ref, a_ref, b_ref, wa_ref, wb_ref, g_ref, wr_ref, tri_ref,
                   hres_ref, xn_ref, slab_ref, meta_ref, cnt_ref):
    @pl.when(pl.program_id(0) == 0)
    def _():
        cnt_ref[...] = jnp.zeros_like(cnt_ref)

    hres = h_ref[...] + _dot(a_ref[...], wa_ref[...]) + _dot(b_ref[...], wb_ref[...])
    hres_ref[...] = hres
    xn = _rms(hres, g_ref[...])
    xn_ref[...] = xn
    xh = xn.astype(BF16)
    xl = (xn - xh.astype(F32)).astype(BF16)
    wr = wr_ref[...]
    wh = wr.astype(BF16)
    wl = (wr - wh.astype(F32)).astype(BF16)
    logits = _dot(xh, wh) + (_dot(xh, wl) + _dot(xl, wh))
    lane = lax.broadcasted_iota(jnp.int32, logits.shape, 1)
    logits = jnp.where(lane < N_EXPERTS, logits, MASK_VALUE)
    m1 = jnp.max(logits, axis=1, keepdims=True)
    i1 = jnp.min(jnp.where(logits == m1, lane, LANES), axis=1, keepdims=True)
    rest = jnp.where(lane == i1, MASK_VALUE, logits)
    m2 = jnp.max(rest, axis=1, keepdims=True)
    i2 = jnp.min(jnp.where(rest == m2, lane, LANES), axis=1, keepdims=True)
    e2 = jnp.exp(m2 - m1)
    w1 = 1.0 / (1.0 + e2)
    w2 = e2 / (1.0 + e2)
    hot1 = lane == i1
    hot2 = lane == i2
    onehot = jnp.where(hot1 | hot2, 1.0, 0.0)
    before = _dot(tri_ref[...], onehot.astype(BF16)) + cnt_ref[...]
    r1 = jnp.sum(jnp.where(hot1, before, 0.0), axis=1, keepdims=True)
    r2 = jnp.sum(jnp.where(hot2, before, 0.0), axis=1, keepdims=True)
    cnt_ref[...] += jnp.sum(onehot, axis=0, keepdims=True)
    slab = (jnp.where(lane == 0, i1.astype(F32), 0.0) + jnp.where(lane == 1, i2.astype(F32), 0.0)
            + jnp.where(lane == 2, r1, 0.0) + jnp.where(lane == 3, r2, 0.0)
            + jnp.where(lane == 4, w1, 0.0) + jnp.where(lane == 5, w2, 0.0))
    slab_ref[...] = slab
    meta_ref[...] = slab.T[:8, :]


def _router(h, a, b, wa, wb, g, w_router, tm):
    N = h.shape[0]
    tri = (jnp.arange(tm)[:, None] > jnp.arange(tm)[None, :]).astype(BF16)
    row = lambda i: (i, 0)
    full = lambda x: pl.BlockSpec(x.shape, lambda i: (0, 0))
    half_w = a.shape[1]
    return pl.pallas_call(
        _router_kernel,
        grid=(N // tm,),
        in_specs=[pl.BlockSpec((tm, D_MODEL), row), pl.BlockSpec((tm, half_w), row),
                  pl.BlockSpec((tm, half_w), row), full(wa), full(wb), full(g), full(w_router),
                  full(tri)],
        out_specs=[pl.BlockSpec((tm, D_MODEL), row), pl.BlockSpec((tm, D_MODEL), row),
                   pl.BlockSpec((tm, LANES), row), pl.BlockSpec((8, tm), lambda i: (0, i)),
                   pl.BlockSpec((1, LANES), lambda i: (0, 0))],
        out_shape=[jax.ShapeDtypeStruct((N, D_MODEL), F32), jax.ShapeDtypeStruct((N, D_MODEL), F32),
                   jax.ShapeDtypeStruct((N, LANES), F32), jax.ShapeDtypeStruct((8, N), F32),
                   jax.ShapeDtypeStruct((1, LANES), F32)],
        compiler_params=_params("arbitrary"),
        name="moe_router",
    )(h, a, b, wa, wb, g, w_router, tri)


def _row_copy(src, src_row, dst, dst_row, sem):
    return pltpu.make_async_copy(src.at[pl.ds(src_row, 1), :], dst.at[pl.ds(dst_row, 1), :], sem)


def _dispatch_kernel(pos_hbm, xn_ref, init_hbm, xs_hbm, pos_smem, idx_sem, row_sem, *, tm):
    del init_hbm
    i = pl.program_id(0)
    idx_copy = pltpu.make_async_copy(pos_hbm.at[i], pos_smem, idx_sem)
    idx_copy.start()
    idx_copy.wait()

    def issue(j, carry):
        base = pl.multiple_of(j * 8, 8)
        for u in range(8):
            for k in range(2):
                _row_copy(xn_ref, base + u, xs_hbm, pos_smem[k * tm + base + u],
                          row_sem).start(priority=k)
        return carry

    lax.fori_loop(0, tm // 8, issue, 0)
    for k in range(2):
        pltpu.make_async_copy(xn_ref, xs_hbm.at[pl.ds(0, tm), :], row_sem).wait()


def _dispatch(pos_tiles, xn, n_rows, tm):
    N = xn.shape[0]
    init = jnp.zeros((n_rows, D_MODEL), F32)
    return pl.pallas_call(
        functools.partial(_dispatch_kernel, tm=tm),
        grid=(N // tm,),
        in_specs=[pl.BlockSpec(memory_space=pl.ANY),
                  pl.BlockSpec((tm, D_MODEL), lambda i: (i, 0)),
                  pl.BlockSpec(memory_space=pl.ANY)],
        out_specs=pl.BlockSpec(memory_space=pl.ANY),
        out_shape=jax.ShapeDtypeStruct((n_rows, D_MODEL), F32),
        scratch_shapes=[pltpu.SMEM((2 * tm,), jnp.int32), pltpu.SemaphoreType.DMA,
                        pltpu.SemaphoreType.DMA],
        input_output_aliases={2: 0},
        compiler_params=_params("arbitrary"),
        name="moe_dispatch",
    )(pos_tiles, xn, init)


def _expert_kernel(te_ref, tv_ref, x_ref, wg_ref, wu_ref, wd_ref, y_ref, xb_ref, acc_ref):
    del te_ref
    i = pl.program_id(0)
    f = pl.program_id(1)
    live = tv_ref[i] > 0

    @pl.when(live & (f == 0))
    def _():
        xb_ref[...] = x_ref[...].astype(BF16)
        acc_ref[...] = jnp.zeros_like(acc_ref)

    @pl.when(live)
    def _():
        xb = xb_ref[...]
        gate = _dot(xb, wg_ref[0])
        up = _dot(xb, wu_ref[0])
        act = gate * (1.0 / (1.0 + jnp.exp(-gate))) * up
        acc_ref[...] += _dot(act.astype(BF16), wd_ref[0])

    @pl.when(f == pl.num_programs(1) - 1)
    def _():
        y_ref[...] = jnp.where(live, acc_ref[...], 0.0)


def _experts(tile_expert, tile_valid, xs, w_gate, w_up, w_down, tr, tf):
    n_rows = xs.shape[0]
    nf = D_FF // tf
    fsel = lambda f, i, tv: jnp.where(tv[i] > 0, f, 0)
    grid_spec = pltpu.PrefetchScalarGridSpec(
        num_scalar_prefetch=2,
        grid=(n_rows // tr, nf),
        in_specs=[pl.BlockSpec((tr, D_MODEL), lambda i, f, te, tv: (i, 0)),
                  pl.BlockSpec((1, D_MODEL, tf), lambda i, f, te, tv: (te[i], 0, fsel(f, i, tv))),
                  pl.BlockSpec((1, D_MODEL, tf), lambda i, f, te, tv: (te[i], 0, fsel(f, i, tv))),
                  pl.BlockSpec((1, tf, D_MODEL), lambda i, f, te, tv: (te[i], fsel(f, i, tv), 0))],
        out_specs=pl.BlockSpec((tr, D_MODEL), lambda i, f, te, tv: (i, 0)),
        scratch_shapes=[pltpu.VMEM((tr, D_MODEL), BF16), pltpu.VMEM((tr, D_MODEL), F32)])
    return pl.pallas_call(
        _expert_kernel,
        grid_spec=grid_spec,
        out_shape=jax.ShapeDtypeStruct((n_rows, D_MODEL), F32),
        compiler_params=_params("arbitrary", "arbitrary"),
        name="moe_experts",
    )(tile_expert, tile_valid, xs, w_gate, w_up, w_down)


def _combine_kernel(pos_hbm, hres_ref, slab_ref, fn_ref, ys_hbm, o_ref, pos_smem, ybuf_ref,
                    idx_sem, row_sem, *, tm):
    i = pl.program_id(0)
    idx_copy = pltpu.make_async_copy(pos_hbm.at[i], pos_smem, idx_sem)
    idx_copy.start()
    idx_copy.wait()

    def issue(j, carry):
        base = pl.multiple_of(j * 8, 8)
        for u in range(8):
            for k in range(2):
                _row_copy(ys_hbm, pos_smem[k * tm + base + u], ybuf_ref.at[k], base + u,
                          row_sem).start(priority=k)
        return carry

    lax.fori_loop(0, tm // 8, issue, 0)
    for k in range(2):
        pltpu.make_async_copy(ys_hbm.at[pl.ds(0, tm), :], ybuf_ref.at[k], row_sem).wait()

    slab = slab_ref[...]
    lane = lax.broadcasted_iota(jnp.int32, slab.shape, 1)
    w1 = jnp.sum(jnp.where(lane == 4, slab, 0.0), axis=1, keepdims=True)
    w2 = jnp.sum(jnp.where(lane == 5, slab, 0.0), axis=1, keepdims=True)
    out = hres_ref[...] + w1 * ybuf_ref[0] + w2 * ybuf_ref[1]
    o_ref[...] = _rms(out, fn_ref[...])


def _combine(pos_tiles, hres, slab, final_g, ys, tm):
    N = hres.shape[0]
    row = lambda i: (i, 0)
    return pl.pallas_call(
        functools.partial(_combine_kernel, tm=tm),
        grid=(N // tm,),
        in_specs=[pl.BlockSpec(memory_space=pl.ANY), pl.BlockSpec((tm, D_MODEL), row),
                  pl.BlockSpec((tm, LANES), row), pl.BlockSpec(final_g.shape, lambda i: (0, 0)),
                  pl.BlockSpec(memory_space=pl.ANY)],
        out_specs=pl.BlockSpec((tm, D_MODEL), row),
        out_shape=jax.ShapeDtypeStruct((N, D_MODEL), F32),
        scratch_shapes=[pltpu.SMEM((2 * tm,), jnp.int32), pltpu.VMEM((2, tm, D_MODEL), F32),
                        pltpu.SemaphoreType.DMA, pltpu.SemaphoreType.DMA],
        compiler_params=_params("arbitrary"),
        name="moe_combine",
    )(pos_tiles, hres, slab, final_g, ys)


def _moe(h, a, b, wa, wb, g, w_router, w_gate, w_up, w_down, final_g):
    N = h.shape[0]
    tm, tr = MOE_TM, MOE_TR
    hres, xn, slab, meta, cnt = _router(h, a, b, wa, wb, g, w_router, tm)
    e1, e2, r1, r2 = (meta[j].astype(jnp.int32) for j in range(4))
    counts = cnt[0, :N_EXPERTS].astype(jnp.int32)
    cap = (counts + tr - 1) // tr * tr
    ends = jnp.cumsum(cap)
    offs = ends - cap
    pos = jnp.stack([offs[e1] + r1, offs[e2] + r2])
    pos_tiles = pos.reshape(2, N // tm, tm).transpose(1, 0, 2).reshape(N // tm, 2 * tm)
    n_rows = 2 * N + N_EXPERTS * tr
    tile_start = jnp.arange(n_rows // tr, dtype=jnp.int32) * tr
    tile_expert = jnp.minimum(jnp.sum(tile_start[:, None] >= ends[None, :], axis=1),
                              N_EXPERTS - 1).astype(jnp.int32)
    tile_valid = jnp.clip(counts[tile_expert] - (tile_start - offs[tile_expert]), 0, tr)
    tile_valid = jnp.where(tile_start < ends[N_EXPERTS - 1], tile_valid, 0).astype(jnp.int32)
    xs = _dispatch(pos_tiles, xn, n_rows, tm)
    ys = _experts(tile_expert, tile_valid, xs, w_gate, w_up, w_down, tr, MOE_TF)
    return _combine(pos_tiles, hres, slab, final_g, ys, tm)


def _l1_pre_kernel(x_ref, pos_ref, g_ref, w_ref, cw_ref, invf_ref, mlo_ref, mhi_ref,
                   q_ref, kv_ref, oc_ref, tail_ref):
    @pl.when(pl.program_id(1) == 0)
    def _():
        tail_ref[...] = jnp.zeros_like(tail_ref)

    xn = _rms(x_ref[...], g_ref[...]).astype(BF16)
    y = _dot(xn, w_ref[...])
    ang = pos_ref[...].astype(F32) * invf_ref[...]
    cos = jnp.cos(ang)
    sin = jnp.sin(ang)
    sin_lo = sin * mlo_ref[...]
    sin_hi = sin * mhi_ref[...]
    half = HEAD_DIM // 2
    qw = SWA_HEADS * HEAD_DIM
    for j in range(qw // LANES):
        sl = slice(j * LANES, (j + 1) * LANES)
        q_ref[:, sl] = _rope(y[:, sl], cos, sin_lo, sin_hi, half).astype(BF16)
    kv_ref[:, :LANES] = _rope(y[:, qw:qw + LANES], cos, sin_lo, sin_hi, half).astype(BF16)
    kv_ref[:, LANES:] = y[:, qw + LANES:qw + 2 * LANES].astype(BF16)

    o = qw + 2 * LANES
    C = CONV_CHANNELS
    gate_b = y[:, o:o + C]
    u = y[:, o + C:o + 2 * C] * y[:, o + 2 * C:o + 3 * C]
    tm = u.shape[0]
    rows = lax.broadcasted_iota(jnp.int32, u.shape, 0)
    tail = tail_ref[...]
    u1 = jnp.where(rows == 0, tail[7:8, :], pltpu.roll(u, 1, 0))
    u2 = jnp.where(rows == 0, tail[6:7, :], jnp.where(rows == 1, tail[7:8, :], pltpu.roll(u, 2, 0)))
    cw = cw_ref[...]
    oc_ref[...] = (gate_b * (cw[0:1, :] * u2 + cw[1:2, :] * u1 + cw[2:3, :] * u)).astype(BF16)
    tail_ref[...] = u[tm - 8:tm, :]


def _l1_pre(x2, pos2, g, w_in, conv_w, B, S, tm):
    N = B * S
    ns = S // tm
    lane = jnp.arange(LANES)
    half = HEAD_DIM // 2
    freq = ROPE_THETA ** (-jnp.arange(0, HEAD_DIM, 2, dtype=F32) / HEAD_DIM)
    invf = freq[lane % half][None, :].astype(F32)
    m_lo = jnp.where(lane % HEAD_DIM < half, -1.0, 0.0)[None, :].astype(F32)
    m_hi = jnp.where(lane % HEAD_DIM >= half, 1.0, 0.0)[None, :].astype(F32)
    row = lambda b, s: (b * ns + s, 0)
    const = lambda b, s: (0, 0)
    full = lambda a: pl.BlockSpec(a.shape, const)
    return pl.pallas_call(
        _l1_pre_kernel,
        grid=(B, ns),
        in_specs=[pl.BlockSpec((tm, D_MODEL), row), pl.BlockSpec((tm, 1), row),
                  full(g), full(w_in), full(conv_w), full(invf), full(m_lo), full(m_hi)],
        out_specs=[pl.BlockSpec((tm, SWA_HEADS * HEAD_DIM), row),
                   pl.BlockSpec((tm, 2 * LANES), row),
                   pl.BlockSpec((tm, CONV_CHANNELS), row)],
        out_shape=[jax.ShapeDtypeStruct((N, SWA_HEADS * HEAD_DIM), BF16),
                   jax.ShapeDtypeStruct((N, 2 * LANES), BF16),
                   jax.ShapeDtypeStruct((N, CONV_CHANNELS), BF16)],
        scratch_shapes=[pltpu.VMEM((8, CONV_CHANNELS), F32)],
        compiler_params=_params("arbitrary", "arbitrary"),
        name="l1_pre",
    )(x2, pos2, g, w_in, conv_w, invf, m_lo, m_hi)


def _swa_kernel(sink_ref, q_ref, kv_ref, kvp_ref, o_ref, *, tq):
    s_idx = pl.program_id(1)
    first = lax.broadcasted_iota(jnp.int32, (1, LANES), 1) < HEAD_DIM
    G = SWA_HEADS // SWA_KV_HEADS
    W = WINDOW

    def dup(x):
        r = pltpu.roll(x, HEAD_DIM, 1)
        return (jnp.where(first, x, r), jnp.where(first, r, x))

    kv = kv_ref[0]
    kvp = kvp_ref[0]
    k_all = jnp.concatenate([kvp[:, :LANES], kv[:, :LANES]], axis=0)
    v_all = jnp.concatenate([kvp[:, LANES:], kv[:, LANES:]], axis=0)
    k_dup = dup(k_all)
    v_dup = dup(v_all)
    q = q_ref[0]
    zero = jnp.zeros((W, LANES), q.dtype)
    rel = (lax.broadcasted_iota(jnp.int32, (W, 2 * W), 0) + W
           - lax.broadcasted_iota(jnp.int32, (W, 2 * W), 1))
    in_window = (rel >= 0) & (rel < W)
    for r in range(tq // W):
        key_pos = (s_idx * tq + r * W - W
                   + lax.broadcasted_iota(jnp.int32, (W, 2 * W), 1))
        keep = in_window & (key_pos >= 0)
        keep4 = jnp.concatenate([keep] * (2 * (G // 2)), axis=0)
        for g in range(SWA_KV_HEADS):
            kb = k_dup[g][r * W:r * W + 2 * W]
            vb = v_dup[g][r * W:r * W + 2 * W]
            parts = []
            sinks = []
            for j in range(G // 2):
                blk = g * (G // 2) + j
                q2 = q[r * W:(r + 1) * W, blk * LANES:(blk + 1) * LANES]
                parts += [jnp.where(first, q2, zero), jnp.where(first, zero, q2)]
                sinks += [jnp.full((W, 1), sink_ref[2 * blk], F32),
                          jnp.full((W, 1), sink_ref[2 * blk + 1], F32)]
            qst = jnp.concatenate(parts, axis=0)
            sink = jnp.concatenate(sinks, axis=0)
            s = _dot_nt(qst, kb)
            s = jnp.where(keep4, s, MASK_VALUE)
            m = jnp.maximum(jnp.max(s, axis=1, keepdims=True), sink)
            e = jnp.exp(s - m)
            den = jnp.sum(e, axis=1, keepdims=True) + jnp.exp(sink - m)
            o = _dot((e * (1.0 / den)).astype(BF16), vb)
            for j in range(G // 2):
                blk = g * (G // 2) + j
                oa = o[(2 * j) * W:(2 * j + 1) * W]
                ob = o[(2 * j + 1) * W:(2 * j + 2) * W]
                o_ref[0, r * W:(r + 1) * W, blk * LANES:(blk + 1) * LANES] = (
                    jnp.where(first, oa, ob).astype(o_ref.dtype))


def _swa(sinks, q, kv, B, S, tq):
    ns = S // tq
    per = tq // WINDOW
    return pl.pallas_call(
        functools.partial(_swa_kernel, tq=tq),
        grid=(B, ns),
        in_specs=[pl.BlockSpec(memory_space=pltpu.SMEM),
                  pl.BlockSpec((1, tq, SWA_HEADS * HEAD_DIM), lambda b, s: (b, s, 0)),
                  pl.BlockSpec((1, tq, 2 * LANES), lambda b, s: (b, s, 0)),
                  pl.BlockSpec((1, WINDOW, 2 * LANES),
                               lambda b, s: (b, jnp.maximum(s * per - 1, 0), 0))],
        out_specs=pl.BlockSpec((1, tq, SWA_HEADS * HEAD_DIM), lambda b, s: (b, s, 0)),
        out_shape=jax.ShapeDtypeStruct((B, S, SWA_HEADS * HEAD_DIM), BF16),
        compiler_params=_params("arbitrary", "arbitrary"),
        name="swa_attn",
    )(sinks, q, kv, kv)


def _pad_heads(w, n_heads, width, offset=0):
    K = w.shape[0]
    w3 = w.reshape(K, n_heads, width)
    out = jnp.zeros((K, n_heads, LANES), w.dtype).at[:, :, offset:offset + width].set(w3)
    return out.reshape(K, n_heads * LANES)


def kernel(x, positions, l0_norm_mix, l0_w_in, l0_b_forget, l0_q_norm, l0_w_q_up, l0_kv_norm,
           l0_w_kv_up, l0_w_out, l0_norm_ffn, l0_w_gate, l0_w_up, l0_w_down,
           l1_norm_mix, l1_w_in, l1_sinks, l1_conv_w, l1_w_out, l1_norm_ffn, l1_w_router,
           l1_w_gate, l1_w_up, l1_w_down, final_norm):
    B, S, D = x.shape
    assert D == D_MODEL and S % 512 == 0
    N = B * S
    tm = 512
    x2 = x.reshape(N, D)
    pos2 = positions.reshape(N, 1)
    rowv = lambda v: v.reshape(1, -1).astype(F32)

    fox_w = FOX_HEADS * HEAD_DIM
    o = 3 * fox_w
    w_fq = l0_w_in[:, :fox_w] * (HEAD_DIM ** -0.5 * LOG2E)
    w_f = l0_w_in[:, o:o + FOX_HEADS]
    o += FOX_HEADS
    w_cq = l0_w_in[:, o:o + MLA_Q_RANK]
    o += MLA_Q_RANK
    w_ckv = l0_w_in[:, o:o + MLA_KV_RANK]
    o += MLA_KV_RANK
    w_kr = l0_w_in[:, o:o + MLA_ROPE_DIM]
    w_ext = jnp.concatenate(
        [w_fq, l0_w_in[:, fox_w:3 * fox_w], w_cq, w_ckv,
         _pad_heads(w_kr, 1, MLA_ROPE_DIM, MLA_NOPE_DIM), _pad_heads(w_f, 1, FOX_HEADS)],
        axis=1).astype(BF16)
    bf_row = jnp.zeros((1, LANES), F32).at[0, :FOX_HEADS].set(l0_b_forget.astype(F32))
    qk_dim = MLA_NOPE_DIM + MLA_ROPE_DIM
    wq = _pad_heads(l0_w_q_up * (qk_dim ** -0.5 * LOG2E), MLA_HEADS, qk_dim).astype(BF16)
    kv3 = l0_w_kv_up.reshape(MLA_KV_RANK, MLA_HEADS, MLA_NOPE_DIM + MLA_V_DIM)
    wk = _pad_heads(kv3[:, :, :MLA_NOPE_DIM].reshape(MLA_KV_RANK, -1), MLA_HEADS,
                    MLA_NOPE_DIM).astype(BF16)
    wv = kv3[:, :, MLA_NOPE_DIM:].reshape(MLA_KV_RANK, -1).astype(BF16)

    q_f, k_f, vt_f, q_m, k_m, vt_m, ct = _l0_pre(
        x2, pos2, rowv(l0_norm_mix), w_ext, bf_row, rowv(l0_q_norm), wq, rowv(l0_kv_norm),
        wk, wv, B, S, tm)

    tq, tk, tw = ATTN_TQ, ATTN_TK, ATTN_TW
    o_fox = _attention(q_f.reshape(B, S, fox_w), k_f.reshape(B, S, 2 * fox_w), vt_f,
                       ct.reshape(B, FOX_HEADS // 2, 2, S), B, S, tq, tk, tw, True)
    o_mla = _attention(q_m.reshape(B, S, MLA_PAD), k_m.reshape(B, S, MLA_PAD), vt_m, None,
                       B, S, tq, tk, tw, False)

    w_out0 = l0_w_out.astype(BF16)
    h1 = _ffn(x2, o_fox.reshape(N, -1), o_mla.reshape(N, -1), w_out0[:fox_w], w_out0[fox_w:],
              rowv(l0_norm_ffn), l0_w_gate.astype(BF16), l0_w_up.astype(BF16),
              l0_w_down.astype(BF16), FFN_TM, FFN_TF)

    swa_w = SWA_HEADS * HEAD_DIM
    w_in1 = jnp.concatenate([l1_w_in[:, :swa_w] * (HEAD_DIM ** -0.5), l1_w_in[:, swa_w:]],
                            axis=1).astype(BF16)
    q_s, kv_s, o_conv = _l1_pre(h1, pos2, rowv(l1_norm_mix), w_in1, l1_conv_w.astype(F32),
                                B, S, tm)
    o_swa = _swa(l1_sinks.astype(F32), q_s.reshape(B, S, swa_w), kv_s.reshape(B, S, 2 * LANES),
                 B, S, 512)
    w_out1 = l1_w_out.astype(BF16)
    w_router = jnp.zeros((D_MODEL, LANES), F32).at[:, :N_EXPERTS].set(l1_w_router.astype(F32))
    out = _moe(h1, o_swa.reshape(N, -1), o_conv, w_out1[:swa_w], w_out1[swa_w:],
               rowv(l1_norm_ffn), w_router, l1_w_gate.astype(BF16), l1_w_up.astype(BF16),
               l1_w_down.astype(BF16), rowv(final_norm))
    return out.reshape(B, S, D)
```

```python
import functools

import jax
import jax.numpy as jnp
from jax import lax
from jax.experimental import pallas as pl
from jax.experimental.pallas import tpu as pltpu

F32 = jnp.float32
BF16 = jnp.bfloat16

D_MODEL = 1024
HEAD_DIM = 64
RMS_EPS = 1e-6
ROPE_THETA = 10000.0
FOX_HEADS = 8
MLA_HEADS = 8
MLA_Q_RANK = 384
MLA_KV_RANK = 256
MLA_NOPE_DIM = 64
MLA_ROPE_DIM = 32
MLA_V_DIM = 64
SWA_HEADS = 8
SWA_KV_HEADS = 2
WINDOW = 128
CONV_CHANNELS = 512
CONV_WIDTH = 3
D_FF = 3584
N_EXPERTS = 8

LANES = 128
MASK_VALUE = -1e30
LOG2E = 1.4426950408889634
VMEM_LIMIT = 56 * 1024 * 1024

QKV_W = 3 * FOX_HEADS * HEAD_DIM
MLA_PAD = MLA_HEADS * LANES
L0_COLS = QKV_W + MLA_Q_RANK + MLA_KV_RANK + 2 * LANES
ATTN_TQ = 1024
ATTN_TK = 512
ATTN_TW = 256
ATTN_GROUP = 2
MOE_TM = 512
MOE_TR = 1024
MOE_TF = 512
FFN_TM = 1024
FFN_TF = 512
L1_COLS = SWA_HEADS * HEAD_DIM + 2 * SWA_KV_HEADS * HEAD_DIM + 3 * CONV_CHANNELS


def _params(*sem):
    return pltpu.CompilerParams(dimension_semantics=sem, vmem_limit_bytes=VMEM_LIMIT)


def _rms(x, g):
    return x * lax.rsqrt(jnp.mean(x * x, axis=-1, keepdims=True) + RMS_EPS) * g


def _dot(a, b):
    return jnp.dot(a, b, preferred_element_type=F32)


def _dot_nt(a, b):
    return lax.dot_general(a, b, (((1,), (1,)), ((), ())), preferred_element_type=F32)


def _rope(x, cos, sin_lo, sin_hi, half):
    return (x * cos + pltpu.roll(x, half, 1) * sin_hi
            + pltpu.roll(x, LANES - half, 1) * sin_lo)


def _l0_pre_kernel(x_ref, pos_ref, g_ref, w_ref, bf_ref, qn_ref, wq_ref, kvn_ref, wk_ref,
                   wv_ref, tri_ref, place_ref, invf_ref, mlo_ref, mhi_ref,
                   qf_ref, kf_ref, vtf_ref, q_ref, k_ref, vt_ref, ct_ref, carry_ref):
    @pl.when(pl.program_id(1) == 0)
    def _():
        carry_ref[...] = jnp.zeros_like(carry_ref)

    fox_w = FOX_HEADS * HEAD_DIM
    xn = _rms(x_ref[...], g_ref[...]).astype(BF16)
    y = _dot(xn, w_ref[...])
    qf_ref[...] = y[:, :fox_w].astype(BF16)
    vtf_ref[0] = y[:, 2 * fox_w:3 * fox_w].T.astype(BF16)
    o = QKV_W
    cq = y[:, o:o + MLA_Q_RANK]
    o += MLA_Q_RANK
    ckv = y[:, o:o + MLA_KV_RANK]
    o += MLA_KV_RANK
    kr = y[:, o:o + LANES]
    fl = y[:, o + LANES:o + 2 * LANES]

    q = _dot(_rms(cq, qn_ref[...]).astype(BF16), wq_ref[...])
    ckn = _rms(ckv, kvn_ref[...]).astype(BF16)
    kk = _dot(ckn, wk_ref[...])
    vt_ref[0] = _dot(ckn, wv_ref[...]).T.astype(BF16)

    ang = pos_ref[...].astype(F32) * invf_ref[...]
    cos = jnp.cos(ang)
    sin = jnp.sin(ang)
    sin_lo = sin * mlo_ref[...]
    sin_hi = sin * mhi_ref[...]
    half = MLA_ROPE_DIM // 2
    kpe = _rope(kr, cos, sin_lo, sin_hi, half)
    for h in range(MLA_HEADS):
        sl = slice(h * LANES, (h + 1) * LANES)
        q_ref[:, sl] = _rope(q[:, sl], cos, sin_lo, sin_hi, half).astype(BF16)
        k_ref[:, sl] = (kk[:, sl] + kpe).astype(BF16)

    z = fl + bf_ref[...]
    ls = jnp.minimum(z, 0.0) - jnp.log1p(jnp.exp(-jnp.abs(z)))
    tri = tri_ref[...]
    c = carry_ref[...]
    for part in _split3(ls):
        c = c + _dot(tri, part)
    tm = c.shape[0]
    carry_ref[...] = c[tm - 1:tm, :]
    c2 = c * LOG2E
    ct_ref[0] = c2.T[:FOX_HEADS, :]
    bias = None
    for i, part in enumerate(_split3(c2)):
        d = _dot(part, place_ref[i])
        bias = d if bias is None else bias + d
    for hp in range(FOX_HEADS // 2):
        kf_ref[:, 2 * hp * LANES:(2 * hp + 1) * LANES] = (
            y[:, fox_w + hp * LANES:fox_w + (hp + 1) * LANES].astype(BF16))
        kf_ref[:, (2 * hp + 1) * LANES:(2 * hp + 2) * LANES] = (
            bias[:, hp * LANES:(hp + 1) * LANES].astype(BF16))


def _split3(x):
    hi = x.astype(BF16)
    r1 = x - hi.astype(F32)
    mid = r1.astype(BF16)
    lo = (r1 - mid.astype(F32)).astype(BF16)
    return hi, mid, lo


def _l0_pre(x2, pos2, g, w_ext, bf_row, qn, wq, kvn, wk, wv, B, S, tm):
    N = B * S
    ns = S // tm
    tri = (jnp.arange(tm)[:, None] >= jnp.arange(tm)[None, :]).astype(BF16)
    n_pairs = FOX_HEADS // 2
    place = jnp.zeros((3, LANES, n_pairs * LANES), F32)
    for hp in range(n_pairs):
        for i in range(3):
            place = place.at[i, 2 * hp, hp * LANES + i].set(-1.0)
            place = place.at[i, 2 * hp + 1, hp * LANES + 3 + i].set(-1.0)
    place = place.astype(BF16)
    lane = jnp.arange(LANES)
    half = MLA_ROPE_DIM // 2
    in_pe = (lane >= MLA_NOPE_DIM) & (lane < MLA_NOPE_DIM + MLA_ROPE_DIM)
    freq = ROPE_THETA ** (-jnp.arange(0, MLA_ROPE_DIM, 2, dtype=F32) / MLA_ROPE_DIM)
    invf = jnp.where(in_pe, freq[(lane - MLA_NOPE_DIM) % half], 0.0)[None, :].astype(F32)
    m_lo = jnp.where(in_pe & (lane < MLA_NOPE_DIM + half), -1.0, 0.0)[None, :].astype(F32)
    m_hi = jnp.where(in_pe & (lane >= MLA_NOPE_DIM + half), 1.0, 0.0)[None, :].astype(F32)

    row = lambda b, s: (b * ns + s, 0)
    tcol = lambda b, s: (b, 0, s)
    full = lambda a: pl.BlockSpec(a.shape, lambda b, s: (0,) * a.ndim)
    fox_w = FOX_HEADS * HEAD_DIM
    vw = MLA_HEADS * MLA_V_DIM
    return pl.pallas_call(
        _l0_pre_kernel,
        grid=(B, ns),
        in_specs=[pl.BlockSpec((tm, D_MODEL), row), pl.BlockSpec((tm, 1), row),
                  full(g), full(w_ext), full(bf_row), full(qn), full(wq), full(kvn), full(wk),
                  full(wv), full(tri), full(place), full(invf), full(m_lo), full(m_hi)],
        out_specs=[pl.BlockSpec((tm, fox_w), row), pl.BlockSpec((tm, 2 * fox_w), row),
                   pl.BlockSpec((1, fox_w, tm), tcol),
                   pl.BlockSpec((tm, MLA_PAD), row), pl.BlockSpec((tm, MLA_PAD), row),
                   pl.BlockSpec((1, vw, tm), tcol),
                   pl.BlockSpec((1, FOX_HEADS, tm), tcol)],
        out_shape=[jax.ShapeDtypeStruct((N, fox_w), BF16),
                   jax.ShapeDtypeStruct((N, 2 * fox_w), BF16),
                   jax.ShapeDtypeStruct((B, fox_w, S), BF16),
                   jax.ShapeDtypeStruct((N, MLA_PAD), BF16),
                   jax.ShapeDtypeStruct((N, MLA_PAD), BF16),
                   jax.ShapeDtypeStruct((B, vw, S), BF16),
                   jax.ShapeDtypeStruct((B, FOX_HEADS, S), F32)],
        scratch_shapes=[pltpu.VMEM((1, LANES), F32)],
        compiler_params=_params("arbitrary", "arbitrary"),
        name="l0_pre",
    )(x2, pos2, g, w_ext, bf_row, qn, wq, kvn, wk, wv, tri, place, invf, m_lo, m_hi)


def _attn_kernel(*refs, fox, tq, tk, tw):
    if fox:
        q_ref, k_ref, vt_ref, cq_ref, ones_ref, o_ref, m_ref, l_ref, acc_ref = refs
    else:
        q_ref, k_ref, vt_ref, o_ref, m_ref, l_ref, acc_ref = refs
    qi = pl.program_id(2)
    n_strips = tq // tw

    q = q_ref[0]
    if fox:
        lane = lax.broadcasted_iota(jnp.int32, (1, LANES), 1)
        zero = jnp.zeros_like(q)
        qa = (jnp.concatenate([jnp.where(lane < HEAD_DIM, q, zero), ones_ref[0]], axis=1),
              jnp.concatenate([jnp.where(lane < HEAD_DIM, zero, q), ones_ref[1]], axis=1))
    else:
        qa = (q[:, :LANES], q[:, LANES:])

    m_ref[...] = jnp.full_like(m_ref, MASK_VALUE)
    l_ref[...] = jnp.zeros_like(l_ref)
    acc_ref[...] = jnp.zeros_like(acc_ref)

    sub = LANES

    def group(chunks):
        work = []
        for start, strips in chunks:
            kc = k_ref[0, pl.ds(start, tk), :]
            for i in range(2):
                for r, n_sub, mask_from in strips:
                    kh = kc[:n_sub * sub] if fox else kc[:n_sub * sub, i * LANES:(i + 1) * LANES]
                    s = _dot_nt(kh, qa[i][r * tw:(r + 1) * tw])
                    work.append((start, i, r, n_sub, mask_from, s))
        for start, i, r, n_sub, mask_from, s in work:
            tiles = []
            for c in range(n_sub):
                t = s[c * sub:(c + 1) * sub]
                if mask_from is not None and c >= mask_from:
                    keys = start + c * sub + lax.broadcasted_iota(jnp.int32, (sub, tw), 0)
                    qpos = qi * tq + r * tw + lax.broadcasted_iota(jnp.int32, (sub, tw), 1)
                    t = jnp.where(keys <= qpos, t, MASK_VALUE)
                tiles.append(t)
            col_max = tiles[0]
            for t in tiles[1:]:
                col_max = jnp.maximum(col_max, t)
            col_max = jnp.max(col_max, axis=0, keepdims=True)
            idx = i * n_strips + r
            m_prev = m_ref[idx]
            if fox:
                cq = cq_ref[0, 0, i:i + 1, r * tw:(r + 1) * tw]
                m_new = jnp.maximum(m_prev, col_max + cq)
                shift = m_new - cq
            else:
                m_new = jnp.maximum(m_prev, col_max)
                shift = m_new
            m_ref[idx] = m_new
            alpha = jnp.exp2(m_prev - m_new)
            psum = None
            probs = []
            for t in tiles:
                p = jnp.exp2(t - shift)
                psum = p if psum is None else psum + p
                probs.append(p.astype(BF16))
            l_ref[idx] = alpha * l_ref[idx] + jnp.sum(psum, axis=0, keepdims=True)
            vt = vt_ref[0, i * HEAD_DIM:(i + 1) * HEAD_DIM, pl.ds(start, n_sub * sub)]
            rows = slice(i * HEAD_DIM, (i + 1) * HEAD_DIM)
            cols = slice(r * tw, (r + 1) * tw)
            acc_ref[rows, cols] = (acc_ref[rows, cols] * alpha
                                   + _dot(vt, jnp.concatenate(probs, axis=0)))

    n_sub_full = tk // sub
    full_strips = [(r, n_sub_full, None) for r in range(n_strips)]
    per = tw // sub
    diag = []
    for d in range(tq // tk):
        strips = []
        for r in range(n_strips):
            n_sub = min(max((r + 1) * per - d * n_sub_full, 0), n_sub_full)
            if n_sub > 0:
                strips.append((r, n_sub, max(r * per - d * n_sub_full, 0)))
        diag.append((pl.multiple_of(qi * tq + d * tk, tk), strips))

    n_full = qi * (tq // tk)

    def full_group(j, carry):
        group([(pl.multiple_of((j * ATTN_GROUP + g) * tk, tk), full_strips)
               for g in range(ATTN_GROUP)])
        return carry

    lax.fori_loop(0, n_full // ATTN_GROUP, full_group, 0)
    for rem in range(ATTN_GROUP):
        @pl.when(n_full % ATTN_GROUP == rem)
        def _():
            base = n_full - rem
            group([(pl.multiple_of((base + g) * tk, tk), full_strips) for g in range(rem)] + diag)

    heads = []
    for i in range(2):
        rows = slice(i * HEAD_DIM, (i + 1) * HEAD_DIM)
        heads.append(jnp.concatenate(
            [acc_ref[rows, r * tw:(r + 1) * tw] * (1.0 / l_ref[i * n_strips + r])
             for r in range(n_strips)], axis=1))
    o_ref[0] = jnp.concatenate(heads, axis=0).T.astype(o_ref.dtype)


def _attention(q, k, vt, ct, B, S, tq, tk, tw, fox):
    n_pairs = FOX_HEADS // 2
    nq = S // tq
    qw = LANES if fox else 2 * LANES
    in_specs = [pl.BlockSpec((1, tq, qw), lambda b, h, i: (b, i, h)),
                pl.BlockSpec((1, S, 2 * LANES), lambda b, h, i: (b, 0, h)),
                pl.BlockSpec((1, LANES, S), lambda b, h, i: (b, h, 0))]
    args = [q, k, vt]
    if fox:
        in_specs.append(pl.BlockSpec((1, 1, 2, tq), lambda b, h, i: (b, h, 0, i)))
        args.append(ct)
        lane = jnp.arange(LANES)
        ones = jnp.stack([lane < 3, (lane >= 3) & (lane < 6)]).astype(BF16)
        in_specs.append(pl.BlockSpec((2, tq, LANES), lambda b, h, i: (0, 0, 0)))
        args.append(jnp.broadcast_to(ones[:, None, :], (2, tq, LANES)))
    n_strips = tq // tw
    return pl.pallas_call(
        functools.partial(_attn_kernel, fox=fox, tq=tq, tk=tk, tw=tw),
        grid=(B, n_pairs, nq),
        in_specs=in_specs,
        out_specs=pl.BlockSpec((1, tq, LANES), lambda b, h, i: (b, i, h)),
        out_shape=jax.ShapeDtypeStruct((B, S, n_pairs * LANES), BF16),
        scratch_shapes=[pltpu.VMEM((2 * n_strips, 1, tw), F32),
                        pltpu.VMEM((2 * n_strips, 1, tw), F32),
                        pltpu.VMEM((LANES, tq), F32)],
        compiler_params=_params("arbitrary", "arbitrary", "arbitrary"),
        name="fox_attn" if fox else "mla_attn",
    )(*args)


def _ffn_kernel(h_ref, a_ref, b_ref, wa_ref, wb_ref, g_ref, wg_ref, wu_ref, wd_ref,
                o_ref, hres_ref, xn_ref, acc_ref):
    f = pl.program_id(1)

    @pl.when(f == 0)
    def _():
        hres = h_ref[...] + _dot(a_ref[...], wa_ref[...]) + _dot(b_ref[...], wb_ref[...])
        hres_ref[...] = hres
        xn_ref[...] = _rms(hres, g_ref[...]).astype(BF16)
        acc_ref[...] = jnp.zeros_like(acc_ref)

    xn = xn_ref[...]
    gate = _dot(xn, wg_ref[...])
    up = _dot(xn, wu_ref[...])
    act = gate * (1.0 / (1.0 + jnp.exp(-gate))) * up
    acc_ref[...] += _dot(act.astype(BF16), wd_ref[...])

    @pl.when(f == pl.num_programs(1) - 1)
    def _():
        o_ref[...] = hres_ref[...] + acc_ref[...]


def _ffn(h, a, b, wa, wb, g, w_gate, w_up, w_down, tm, tf):
    N = h.shape[0]
    row = lambda i, f: (i, 0)
    full = lambda x: pl.BlockSpec(x.shape, lambda i, f: (0, 0))
    half_w = a.shape[1]
    return pl.pallas_call(
        _ffn_kernel,
        grid=(N // tm, D_FF // tf),
        in_specs=[pl.BlockSpec((tm, D_MODEL), row), pl.BlockSpec((tm, half_w), row),
                  pl.BlockSpec((tm, half_w), row), full(wa), full(wb), full(g),
                  pl.BlockSpec((D_MODEL, tf), lambda i, f: (0, f)),
                  pl.BlockSpec((D_MODEL, tf), lambda i, f: (0, f)),
                  pl.BlockSpec((tf, D_MODEL), lambda i, f: (f, 0))],
        out_specs=pl.BlockSpec((tm, D_MODEL), row),
        out_shape=jax.ShapeDtypeStruct((N, D_MODEL), F32),
        scratch_shapes=[pltpu.VMEM((tm, D_MODEL), F32), pltpu.VMEM((tm, D_MODEL), BF16),
                        pltpu.VMEM((tm, D_MODEL), F32)],
        compiler_params=_params("arbitrary", "arbitrary"),
        name="dense_ffn",
    )(h, a, b, wa, wb, g, w_gate, w_up, w_down)


def _router_kernel(h_ref, a_ref, b_ref, wa_ref, wb_ref, g_ref, wr_ref, tri_ref,
                   hres_ref, xn_ref, slab_ref, meta_ref, cnt_ref):
    @pl.when(pl.program_id(0) == 0)
    def _():
        cnt_ref[...] = jnp.zeros_like(cnt_ref)

    hres = h_ref[...] + _dot(a_ref[...], wa_ref[...]) + _dot(b_ref[...], wb_ref[...])
    hres_ref[...] = hres
    xn = _rms(hres, g_ref[...])
    xn_ref[...] = xn
    xh = xn.astype(BF16)
    xl = (xn - xh.astype(F32)).astype(BF16)
    wr = wr_ref[...]
    wh = wr.astype(BF16)
    wl = (wr - wh.astype(F32)).astype(BF16)
    logits = _dot(xh, wh) + (_dot(xh, wl) + _dot(xl, wh))
    lane = lax.broadcasted_iota(jnp.int32, logits.shape, 1)
    logits = jnp.where(lane < N_EXPERTS, logits, MASK_VALUE)
    m1 = jnp.max(logits, axis=1, keepdims=True)
    i1 = jnp.min(jnp.where(logits == m1, lane, LANES), axis=1, keepdims=True)
    rest = jnp.where(lane == i1, MASK_VALUE, logits)
    m2 = jnp.max(rest, axis=1, keepdims=True)
    i2 = jnp.min(jnp.where(rest == m2, lane, LANES), axis=1, keepdims=True)
    e2 = jnp.exp(m2 - m1)
    w1 = 1.0 / (1.0 + e2)
    w2 = e2 / (1.0 + e2)
    hot1 = lane == i1
    hot2 = lane == i2
    onehot = jnp.where(hot1 | hot2, 1.0, 0.0)
    before = _dot(tri_ref[...], onehot.astype(BF16)) + cnt_ref[...]
    r1 = jnp.sum(jnp.where(hot1, before, 0.0), axis=1, keepdims=True)
    r2 = jnp.sum(jnp.where(hot2, before, 0.0), axis=1, keepdims=True)
    cnt_ref[...] += jnp.sum(onehot, axis=0, keepdims=True)
    slab = (jnp.where(lane == 0, i1.astype(F32), 0.0) + jnp.where(lane == 1, i2.astype(F32), 0.0)
            + jnp.where(lane == 2, r1, 0.0) + jnp.where(lane == 3, r2, 0.0)
            + jnp.where(lane == 4, w1, 0.0) + jnp.where(lane == 5, w2, 0.0))
    slab_ref[...] = slab
    meta_ref[...] = slab.T[:8, :]


def _router(h, a, b, wa, wb, g, w_router, tm):
    N = h.shape[0]
    tri = (jnp.arange(tm)[:, None] > jnp.arange(tm)[None, :]).astype(BF16)
    row = lambda i: (i, 0)
    full = lambda x: pl.BlockSpec(x.shape, lambda i: (0, 0))
    half_w = a.shape[1]
    return pl.pallas_call(
        _router_kernel,
        grid=(N // tm,),
        in_specs=[pl.BlockSpec((tm, D_MODEL), row), pl.BlockSpec((tm, half_w), row),
                  pl.BlockSpec((tm, half_w), row), full(wa), full(wb), full(g), full(w_router),
                  full(tri)],
        out_specs=[pl.BlockSpec((tm, D_MODEL), row), pl.BlockSpec((tm, D_MODEL), row),
                   pl.BlockSpec((tm, LANES), row), pl.BlockSpec((8, tm), lambda i: (0, i)),
                   pl.BlockSpec((1, LANES), lambda i: (0, 0))],
        out_shape=[jax.ShapeDtypeStruct((N, D_MODEL), F32), jax.ShapeDtypeStruct((N, D_MODEL), F32),
                   jax.ShapeDtypeStruct((N, LANES), F32), jax.ShapeDtypeStruct((8, N), F32),
                   jax.ShapeDtypeStruct((1, LANES), F32)],
        compiler_params=_params("arbitrary"),
        name="moe_router",
    )(h, a, b, wa, wb, g, w_router, tri)


def _row_copy(src, src_row, dst, dst_row, sem):
    return pltpu.make_async_copy(src.at[pl.ds(src_row, 1), :], dst.at[pl.ds(dst_row, 1), :], sem)


def _dispatch_kernel(pos_hbm, xn_ref, init_hbm, xs_hbm, pos_smem, idx_sem, row_sem, *, tm):
    del init_hbm
    i = pl.program_id(0)
    idx_copy = pltpu.make_async_copy(pos_hbm.at[i], pos_smem, idx_sem)
    idx_copy.start()
    idx_copy.wait()

    def issue(j, carry):
        base = pl.multiple_of(j * 8, 8)
        for u in range(8):
            for k in range(2):
                _row_copy(xn_ref, base + u, xs_hbm, pos_smem[k * tm + base + u], row_sem).start()
        return carry

    lax.fori_loop(0, tm // 8, issue, 0)
    for k in range(2):
        pltpu.make_async_copy(xn_ref, xs_hbm.at[pl.ds(0, tm), :], row_sem).wait()


def _dispatch(pos_tiles, xn, n_rows, tm):
    N = xn.shape[0]
    init = jnp.zeros((n_rows, D_MODEL), F32)
    return pl.pallas_call(
        functools.partial(_dispatch_kernel, tm=tm),
        grid=(N // tm,),
        in_specs=[pl.BlockSpec(memory_space=pl.ANY),
                  pl.BlockSpec((tm, D_MODEL), lambda i: (i, 0)),
                  pl.BlockSpec(memory_space=pl.ANY)],
        out_specs=pl.BlockSpec(memory_space=pl.ANY),
        out_shape=jax.ShapeDtypeStruct((n_rows, D_MODEL), F32),
        scratch_shapes=[pltpu.SMEM((2 * tm,), jnp.int32), pltpu.SemaphoreType.DMA,
                        pltpu.SemaphoreType.DMA],
        input_output_aliases={2: 0},
        compiler_params=_params("arbitrary"),
        name="moe_dispatch",
    )(pos_tiles, xn, init)


def _expert_kernel(te_ref, tv_ref, x_ref, wg_ref, wu_ref, wd_ref, y_ref, xb_ref, acc_ref):
    del te_ref
    i = pl.program_id(0)
    f = pl.program_id(1)
    live = tv_ref[i] > 0

    @pl.when(live & (f == 0))
    def _():
        xb_ref[...] = x_ref[...].astype(BF16)
        acc_ref[...] = jnp.zeros_like(acc_ref)

    @pl.when(live)
    def _():
        xb = xb_ref[...]
        gate = _dot(xb, wg_ref[0].astype(BF16))
        up = _dot(xb, wu_ref[0].astype(BF16))
        act = gate * (1.0 / (1.0 + jnp.exp(-gate))) * up
        acc_ref[...] += _dot(act.astype(BF16), wd_ref[0].astype(BF16))

    @pl.when(f == pl.num_programs(1) - 1)
    def _():
        y_ref[...] = jnp.where(live, acc_ref[...], 0.0)


def _experts(tile_expert, tile_valid, xs, w_gate, w_up, w_down, tr, tf):
    n_rows = xs.shape[0]
    nf = D_FF // tf
    fsel = lambda f, i, tv: jnp.where(tv[i] > 0, f, 0)
    grid_spec = pltpu.PrefetchScalarGridSpec(
        num_scalar_prefetch=2,
        grid=(n_rows // tr, nf),
        in_specs=[pl.BlockSpec((tr, D_MODEL), lambda i, f, te, tv: (i, 0)),
                  pl.BlockSpec((1, D_MODEL, tf), lambda i, f, te, tv: (te[i], 0, fsel(f, i, tv))),
                  pl.BlockSpec((1, D_MODEL, tf), lambda i, f, te, tv: (te[i], 0, fsel(f, i, tv))),
                  pl.BlockSpec((1, tf, D_MODEL), lambda i, f, te, tv: (te[i], fsel(f, i, tv), 0))],
        out_specs=pl.BlockSpec((tr, D_MODEL), lambda i, f, te, tv: (i, 0)),
        scratch_shapes=[pltpu.VMEM((tr, D_MODEL), BF16), pltpu.VMEM((tr, D_MODEL), F32)])
    return pl.pallas_call(
        _expert_kernel,
        grid_spec=grid_spec,
        out_shape=jax.ShapeDtypeStruct((n_rows, D_MODEL), F32),
        compiler_params=_params("arbitrary", "arbitrary"),
        name="moe_experts",
    )(tile_expert, tile_valid, xs, w_gate, w_up, w_down)


def _combine_kernel(pos_hbm, hres_ref, slab_ref, fn_ref, ys_hbm, o_ref, pos_smem, ybuf_ref,
                    idx_sem, row_sem, *, tm):
    i = pl.program_id(0)
    idx_copy = pltpu.make_async_copy(pos_hbm.at[i], pos_smem, idx_sem)
    idx_copy.start()
    idx_copy.wait()

    def issue(j, carry):
        base = pl.multiple_of(j * 8, 8)
        for u in range(8):
            for k in range(2):
                _row_copy(ys_hbm, pos_smem[k * tm + base + u], ybuf_ref.at[k], base + u, row_sem).start()
        return carry

    lax.fori_loop(0, tm // 8, issue, 0)
    for k in range(2):
        pltpu.make_async_copy(ys_hbm.at[pl.ds(0, tm), :], ybuf_ref.at[k], row_sem).wait()

    slab = slab_ref[...]
    lane = lax.broadcasted_iota(jnp.int32, slab.shape, 1)
    w1 = jnp.sum(jnp.where(lane == 4, slab, 0.0), axis=1, keepdims=True)
    w2 = jnp.sum(jnp.where(lane == 5, slab, 0.0), axis=1, keepdims=True)
    out = hres_ref[...] + w1 * ybuf_ref[0] + w2 * ybuf_ref[1]
    o_ref[...] = _rms(out, fn_ref[...])


def _combine(pos_tiles, hres, slab, final_g, ys, tm):
    N = hres.shape[0]
    row = lambda i: (i, 0)
    return pl.pallas_call(
        functools.partial(_combine_kernel, tm=tm),
        grid=(N // tm,),
        in_specs=[pl.BlockSpec(memory_space=pl.ANY), pl.BlockSpec((tm, D_MODEL), row),
                  pl.BlockSpec((tm, LANES), row), pl.BlockSpec(final_g.shape, lambda i: (0, 0)),
                  pl.BlockSpec(memory_space=pl.ANY)],
        out_specs=pl.BlockSpec((tm, D_MODEL), row),
        out_shape=jax.ShapeDtypeStruct((N, D_MODEL), F32),
        scratch_shapes=[pltpu.SMEM((2 * tm,), jnp.int32), pltpu.VMEM((2, tm, D_MODEL), F32),
                        pltpu.SemaphoreType.DMA, pltpu.SemaphoreType.DMA],
        compiler_params=_params("arbitrary"),
        name="moe_combine",
    )(pos_tiles, hres, slab, final_g, ys)


def _moe(h, a, b, wa, wb, g, w_router, w_gate, w_up, w_down, final_g):
    N = h.shape[0]
    tm, tr = MOE_TM, MOE_TR
    hres, xn, slab, meta, cnt = _router(h, a, b, wa, wb, g, w_router, tm)
    e1, e2, r1, r2 = (meta[j].astype(jnp.int32) for j in range(4))
    counts = cnt[0, :N_EXPERTS].astype(jnp.int32)
    cap = (counts + tr - 1) // tr * tr
    ends = jnp.cumsum(cap)
    offs = ends - cap
    pos = jnp.stack([offs[e1] + r1, offs[e2] + r2])
    pos_tiles = pos.reshape(2, N // tm, tm).transpose(1, 0, 2).reshape(N // tm, 2 * tm)
    n_rows = 2 * N + N_EXPERTS * tr
    tile_start = jnp.arange(n_rows // tr, dtype=jnp.int32) * tr
    tile_expert = jnp.minimum(jnp.sum(tile_start[:, None] >= ends[None, :], axis=1),
                              N_EXPERTS - 1).astype(jnp.int32)
    tile_valid = jnp.clip(counts[tile_expert] - (tile_start - offs[tile_expert]), 0, tr)
    tile_valid = jnp.where(tile_start < ends[N_EXPERTS - 1], tile_valid, 0).astype(jnp.int32)
    xs = _dispatch(pos_tiles, xn, n_rows, tm)
    ys = _experts(tile_expert, tile_valid, xs, w_gate, w_up, w_down, tr, MOE_TF)
    return _combine(pos_tiles, hres, slab, final_g, ys, tm)


def _l1_pre_kernel(x_ref, pos_ref, g_ref, w_ref, cw_ref, invf_ref, mlo_ref, mhi_ref,
                   q_ref, kv_ref, oc_ref, tail_ref):
    @pl.when(pl.program_id(1) == 0)
    def _():
        tail_ref[...] = jnp.zeros_like(tail_ref)

    xn = _rms(x_ref[...], g_ref[...]).astype(BF16)
    y = _dot(xn, w_ref[...])
    ang = pos_ref[...].astype(F32) * invf_ref[...]
    cos = jnp.cos(ang)
    sin = jnp.sin(ang)
    sin_lo = sin * mlo_ref[...]
    sin_hi = sin * mhi_ref[...]
    half = HEAD_DIM // 2
    qw = SWA_HEADS * HEAD_DIM
    for j in range(qw // LANES):
        sl = slice(j * LANES, (j + 1) * LANES)
        q_ref[:, sl] = _rope(y[:, sl], cos, sin_lo, sin_hi, half).astype(BF16)
    kv_ref[:, :LANES] = _rope(y[:, qw:qw + LANES], cos, sin_lo, sin_hi, half).astype(BF16)
    kv_ref[:, LANES:] = y[:, qw + LANES:qw + 2 * LANES].astype(BF16)

    o = qw + 2 * LANES
    C = CONV_CHANNELS
    gate_b = y[:, o:o + C]
    u = y[:, o + C:o + 2 * C] * y[:, o + 2 * C:o + 3 * C]
    tm = u.shape[0]
    rows = lax.broadcasted_iota(jnp.int32, u.shape, 0)
    tail = tail_ref[...]
    u1 = jnp.where(rows == 0, tail[7:8, :], pltpu.roll(u, 1, 0))
    u2 = jnp.where(rows == 0, tail[6:7, :], jnp.where(rows == 1, tail[7:8, :], pltpu.roll(u, 2, 0)))
    cw = cw_ref[...]
    oc_ref[...] = (gate_b * (cw[0:1, :] * u2 + cw[1:2, :] * u1 + cw[2:3, :] * u)).astype(BF16)
    tail_ref[...] = u[tm - 8:tm, :]


def _l1_pre(x2, pos2, g, w_in, conv_w, B, S, tm):
    N = B * S
    ns = S // tm
    lane = jnp.arange(LANES)
    half = HEAD_DIM // 2
    freq = ROPE_THETA ** (-jnp.arange(0, HEAD_DIM, 2, dtype=F32) / HEAD_DIM)
    invf = freq[lane % half][None, :].astype(F32)
    m_lo = jnp.where(lane % HEAD_DIM < half, -1.0, 0.0)[None, :].astype(F32)
    m_hi = jnp.where(lane % HEAD_DIM >= half, 1.0, 0.0)[None, :].astype(F32)
    row = lambda b, s: (b * ns + s, 0)
    const = lambda b, s: (0, 0)
    full = lambda a: pl.BlockSpec(a.shape, const)
    return pl.pallas_call(
        _l1_pre_kernel,
        grid=(B, ns),
        in_specs=[pl.BlockSpec((tm, D_MODEL), row), pl.BlockSpec((tm, 1), row),
                  full(g), full(w_in), full(conv_w), full(invf), full(m_lo), full(m_hi)],
        out_specs=[pl.BlockSpec((tm, SWA_HEADS * HEAD_DIM), row),
                   pl.BlockSpec((tm, 2 * LANES), row),
                   pl.BlockSpec((tm, CONV_CHANNELS), row)],
        out_shape=[jax.ShapeDtypeStruct((N, SWA_HEADS * HEAD_DIM), BF16),
                   jax.ShapeDtypeStruct((N, 2 * LANES), BF16),
                   jax.ShapeDtypeStruct((N, CONV_CHANNELS), BF16)],
        scratch_shapes=[pltpu.VMEM((8, CONV_CHANNELS), F32)],
        compiler_params=_params("arbitrary", "arbitrary"),
        name="l1_pre",
    )(x2, pos2, g, w_in, conv_w, invf, m_lo, m_hi)


def _swa_kernel(sink_ref, q_ref, kv_ref, kvp_ref, o_ref, *, tq):
    s_idx = pl.program_id(1)
    first = lax.broadcasted_iota(jnp.int32, (1, LANES), 1) < HEAD_DIM
    G = SWA_HEADS // SWA_KV_HEADS
    W = WINDOW

    def dup(x):
        r = pltpu.roll(x, HEAD_DIM, 1)
        return (jnp.where(first, x, r), jnp.where(first, r, x))

    kv = kv_ref[0]
    kvp = kvp_ref[0]
    k_all = jnp.concatenate([kvp[:, :LANES], kv[:, :LANES]], axis=0)
    v_all = jnp.concatenate([kvp[:, LANES:], kv[:, LANES:]], axis=0)
    k_dup = dup(k_all)
    v_dup = dup(v_all)
    q = q_ref[0]
    zero = jnp.zeros((W, LANES), q.dtype)
    rel = (lax.broadcasted_iota(jnp.int32, (W, 2 * W), 0) + W
           - lax.broadcasted_iota(jnp.int32, (W, 2 * W), 1))
    in_window = (rel >= 0) & (rel < W)
    for r in range(tq // W):
        key_pos = (s_idx * tq + r * W - W
                   + lax.broadcasted_iota(jnp.int32, (W, 2 * W), 1))
        keep = in_window & (key_pos >= 0)
        keep4 = jnp.concatenate([keep] * (2 * (G // 2)), axis=0)
        for g in range(SWA_KV_HEADS):
            kb = k_dup[g][r * W:r * W + 2 * W]
            vb = v_dup[g][r * W:r * W + 2 * W]
            parts = []
            sinks = []
            for j in range(G // 2):
                blk = g * (G // 2) + j
                q2 = q[r * W:(r + 1) * W, blk * LANES:(blk + 1) * LANES]
                parts += [jnp.where(first, q2, zero), jnp.where(first, zero, q2)]
                sinks += [jnp.full((W, 1), sink_ref[2 * blk], F32),
                          jnp.full((W, 1), sink_ref[2 * blk + 1], F32)]
            qst = jnp.concatenate(parts, axis=0)
            sink = jnp.concatenate(sinks, axis=0)
            s = _dot_nt(qst, kb)
            s = jnp.where(keep4, s, MASK_VALUE)
            m = jnp.maximum(jnp.max(s, axis=1, keepdims=True), sink)
            e = jnp.exp(s - m)
            den = jnp.sum(e, axis=1, keepdims=True) + jnp.exp(sink - m)
            o = _dot((e * (1.0 / den)).astype(BF16), vb)
            for j in range(G // 2):
                blk = g * (G // 2) + j
                oa = o[(2 * j) * W:(2 * j + 1) * W]
                ob = o[(2 * j + 1) * W:(2 * j + 2) * W]
                o_ref[0, r * W:(r + 1) * W, blk * LANES:(blk + 1) * LANES] = (
                    jnp.where(first, oa, ob).astype(o_ref.dtype))


def _swa(sinks, q, kv, B, S, tq):
    ns = S // tq
    per = tq // WINDOW
    return pl.pallas_call(
        functools.partial(_swa_kernel, tq=tq),
        grid=(B, ns),
        in_specs=[pl.BlockSpec(memory_space=pltpu.SMEM),
                  pl.BlockSpec((1, tq, SWA_HEADS * HEAD_DIM), lambda b, s: (b, s, 0)),
                  pl.BlockSpec((1, tq, 2 * LANES), lambda b, s: (b, s, 0)),
                  pl.BlockSpec((1, WINDOW, 2 * LANES),
                               lambda b, s: (b, jnp.maximum(s * per - 1, 0), 0))],
        out_specs=pl.BlockSpec((1, tq, SWA_HEADS * HEAD_DIM), lambda b, s: (b, s, 0)),
        out_shape=jax.ShapeDtypeStruct((B, S, SWA_HEADS * HEAD_DIM), BF16),
        compiler_params=_params("arbitrary", "arbitrary"),
        name="swa_attn",
    )(sinks, q, kv, kv)


def _pad_heads(w, n_heads, width, offset=0):
    K = w.shape[0]
    w3 = w.reshape(K, n_heads, width)
    out = jnp.zeros((K, n_heads, LANES), w.dtype).at[:, :, offset:offset + width].set(w3)
    return out.reshape(K, n_heads * LANES)


def kernel(x, positions, l0_norm_mix, l0_w_in, l0_b_forget, l0_q_norm, l0_w_q_up, l0_kv_norm,
           l0_w_kv_up, l0_w_out, l0_norm_ffn, l0_w_gate, l0_w_up, l0_w_down,
           l1_norm_mix, l1_w_in, l1_sinks, l1_conv_w, l1_w_out, l1_norm_ffn, l1_w_router,
           l1_w_gate, l1_w_up, l1_w_down, final_norm):
    B, S, D = x.shape
    assert D == D_MODEL and S % 512 == 0
    N = B * S
    tm = 512
    x2 = x.reshape(N, D)
    pos2 = positions.reshape(N, 1)
    rowv = lambda v: v.reshape(1, -1).astype(F32)

    fox_w = FOX_HEADS * HEAD_DIM
    o = 3 * fox_w
    w_fq = l0_w_in[:, :fox_w] * (HEAD_DIM ** -0.5 * LOG2E)
    w_f = l0_w_in[:, o:o + FOX_HEADS]
    o += FOX_HEADS
    w_cq = l0_w_in[:, o:o + MLA_Q_RANK]
    o += MLA_Q_RANK
    w_ckv = l0_w_in[:, o:o + MLA_KV_RANK]
    o += MLA_KV_RANK
    w_kr = l0_w_in[:, o:o + MLA_ROPE_DIM]
    w_ext = jnp.concatenate(
        [w_fq, l0_w_in[:, fox_w:3 * fox_w], w_cq, w_ckv,
         _pad_heads(w_kr, 1, MLA_ROPE_DIM, MLA_NOPE_DIM), _pad_heads(w_f, 1, FOX_HEADS)],
        axis=1).astype(BF16)
    bf_row = jnp.zeros((1, LANES), F32).at[0, :FOX_HEADS].set(l0_b_forget.astype(F32))
    qk_dim = MLA_NOPE_DIM + MLA_ROPE_DIM
    wq = _pad_heads(l0_w_q_up * (qk_dim ** -0.5 * LOG2E), MLA_HEADS, qk_dim).astype(BF16)
    kv3 = l0_w_kv_up.reshape(MLA_KV_RANK, MLA_HEADS, MLA_NOPE_DIM + MLA_V_DIM)
    wk = _pad_heads(kv3[:, :, :MLA_NOPE_DIM].reshape(MLA_KV_RANK, -1), MLA_HEADS,
                    MLA_NOPE_DIM).astype(BF16)
    wv = kv3[:, :, MLA_NOPE_DIM:].reshape(MLA_KV_RANK, -1).astype(BF16)

    q_f, k_f, vt_f, q_m, k_m, vt_m, ct = _l0_pre(
        x2, pos2, rowv(l0_norm_mix), w_ext, bf_row, rowv(l0_q_norm), wq, rowv(l0_kv_norm),
        wk, wv, B, S, tm)

    tq, tk, tw = ATTN_TQ, ATTN_TK, ATTN_TW
    o_fox = _attention(q_f.reshape(B, S, fox_w), k_f.reshape(B, S, 2 * fox_w), vt_f,
                       ct.reshape(B, FOX_HEADS // 2, 2, S), B, S, tq, tk, tw, True)
    o_mla = _attention(q_m.reshape(B, S, MLA_PAD), k_m.reshape(B, S, MLA_PAD), vt_m, None,
                       B, S, tq, tk, tw, False)

    w_out0 = l0_w_out.astype(BF16)
    h1 = _ffn(x2, o_fox.reshape(N, -1), o_mla.reshape(N, -1), w_out0[:fox_w], w_out0[fox_w:],
              rowv(l0_norm_ffn), l0_w_gate.astype(BF16), l0_w_up.astype(BF16),
              l0_w_down.astype(BF16), FFN_TM, FFN_TF)

    swa_w = SWA_HEADS * HEAD_DIM
    w_in1 = jnp.concatenate([l1_w_in[:, :swa_w] * (HEAD_DIM ** -0.5), l1_w_in[:, swa_w:]],
                            axis=1).astype(BF16)
    q_s, kv_s, o_conv = _l1_pre(h1, pos2, rowv(l1_norm_mix), w_in1, l1_conv_w.astype(F32),
                                B, S, tm)
    o_swa = _swa(l1_sinks.astype(F32), q_s.reshape(B, S, swa_w), kv_s.reshape(B, S, 2 * LANES),
                 B, S, 512)
    w_out1 = l1_w_out.astype(BF16)
    w_router = jnp.zeros((D_MODEL, LANES), F32).at[:, :N_EXPERTS].set(l1_w_router.astype(F32))
    out = _moe(h1, o_swa.reshape(N, -1), o_conv, w_out1[:swa_w], w_out1[swa_w:],
               rowv(l1_norm_ffn), w_router, l1_w_gate, l1_w_up, l1_w_down, rowv(final_norm))
    return out.reshape(B, S, D)
```

```python
import functools

import jax
import jax.numpy as jnp
from jax import lax
from jax.experimental import pallas as pl
from jax.experimental.pallas import tpu as pltpu

F32 = jnp.float32
BF16 = jnp.bfloat16

D_MODEL = 1024
HEAD_DIM = 64
RMS_EPS = 1e-6
ROPE_THETA = 10000.0
FOX_HEADS = 8
MLA_HEADS = 8
MLA_Q_RANK = 384
MLA_KV_RANK = 256
MLA_NOPE_DIM = 64
MLA_ROPE_DIM = 32
MLA_V_DIM = 64
SWA_HEADS = 8
SWA_KV_HEADS = 2
WINDOW = 128
CONV_CHANNELS = 512
CONV_WIDTH = 3
D_FF = 3584
N_EXPERTS = 8

LANES = 128
MASK_VALUE = -1e30
LOG2E = 1.4426950408889634
VMEM_LIMIT = 56 * 1024 * 1024

QKV_W = 3 * FOX_HEADS * HEAD_DIM
MLA_PAD = MLA_HEADS * LANES
L0_COLS = QKV_W + MLA_Q_RANK + MLA_KV_RANK + 2 * LANES
ATTN_TQ = 1024
ATTN_TK = 512
ATTN_TW = 256
ATTN_GROUP = 2
MOE_TM = 512
MOE_TR = 1024
MOE_TF = 512
FFN_TM = 1024
FFN_TF = 512
L1_COLS = SWA_HEADS * HEAD_DIM + 2 * SWA_KV_HEADS * HEAD_DIM + 3 * CONV_CHANNELS


def _params(*sem):
    return pltpu.CompilerParams(dimension_semantics=sem, vmem_limit_bytes=VMEM_LIMIT)


def _rms(x, g):
    return x * lax.rsqrt(jnp.mean(x * x, axis=-1, keepdims=True) + RMS_EPS) * g


def _dot(a, b):
    return jnp.dot(a, b, preferred_element_type=F32)


def _dot_nt(a, b):
    return lax.dot_general(a, b, (((1,), (1,)), ((), ())), preferred_element_type=F32)


def _rope(x, cos, sin_lo, sin_hi, half):
    return (x * cos + pltpu.roll(x, half, 1) * sin_hi
            + pltpu.roll(x, LANES - half, 1) * sin_lo)


def _l0_pre_kernel(x_ref, pos_ref, g_ref, w_ref, bf_ref, qn_ref, wq_ref, kvn_ref, wk_ref,
                   wv_ref, tri_ref, place_ref, invf_ref, mlo_ref, mhi_ref,
                   qf_ref, kf_ref, vtf_ref, q_ref, k_ref, vt_ref, ct_ref, carry_ref):
    @pl.when(pl.program_id(1) == 0)
    def _():
        carry_ref[...] = jnp.zeros_like(carry_ref)

    fox_w = FOX_HEADS * HEAD_DIM
    xn = _rms(x_ref[...], g_ref[...]).astype(BF16)
    y = _dot(xn, w_ref[...])
    qf_ref[...] = y[:, :fox_w].astype(BF16)
    vtf_ref[0] = y[:, 2 * fox_w:3 * fox_w].T.astype(BF16)
    o = QKV_W
    cq = y[:, o:o + MLA_Q_RANK]
    o += MLA_Q_RANK
    ckv = y[:, o:o + MLA_KV_RANK]
    o += MLA_KV_RANK
    kr = y[:, o:o + LANES]
    fl = y[:, o + LANES:o + 2 * LANES]

    q = _dot(_rms(cq, qn_ref[...]).astype(BF16), wq_ref[...])
    ckn = _rms(ckv, kvn_ref[...]).astype(BF16)
    kk = _dot(ckn, wk_ref[...])
    vt_ref[0] = _dot(ckn, wv_ref[...]).T.astype(BF16)

    ang = pos_ref[...].astype(F32) * invf_ref[...]
    cos = jnp.cos(ang)
    sin = jnp.sin(ang)
    sin_lo = sin * mlo_ref[...]
    sin_hi = sin * mhi_ref[...]
    half = MLA_ROPE_DIM // 2
    kpe = _rope(kr, cos, sin_lo, sin_hi, half)
    for h in range(MLA_HEADS):
        sl = slice(h * LANES, (h + 1) * LANES)
        q_ref[:, sl] = _rope(q[:, sl], cos, sin_lo, sin_hi, half).astype(BF16)
        k_ref[:, sl] = (kk[:, sl] + kpe).astype(BF16)

    z = fl + bf_ref[...]
    ls = jnp.minimum(z, 0.0) - jnp.log1p(jnp.exp(-jnp.abs(z)))
    tri = tri_ref[...]
    c = carry_ref[...]
    for part in _split3(ls):
        c = c + _dot(tri, part)
    tm = c.shape[0]
    carry_ref[...] = c[tm - 1:tm, :]
    c2 = c * LOG2E
    ct_ref[0] = c2.T[:FOX_HEADS, :]
    bias = None
    for i, part in enumerate(_split3(c2)):
        d = _dot(part, place_ref[i])
        bias = d if bias is None else bias + d
    for hp in range(FOX_HEADS // 2):
        kf_ref[:, 2 * hp * LANES:(2 * hp + 1) * LANES] = (
            y[:, fox_w + hp * LANES:fox_w + (hp + 1) * LANES].astype(BF16))
        kf_ref[:, (2 * hp + 1) * LANES:(2 * hp + 2) * LANES] = (
            bias[:, hp * LANES:(hp + 1) * LANES].astype(BF16))


def _split3(x):
    hi = x.astype(BF16)
    r1 = x - hi.astype(F32)
    mid = r1.astype(BF16)
    lo = (r1 - mid.astype(F32)).astype(BF16)
    return hi, mid, lo


def _l0_pre(x2, pos2, g, w_ext, bf_row, qn, wq, kvn, wk, wv, B, S, tm):
    N = B * S
    ns = S // tm
    tri = (jnp.arange(tm)[:, None] >= jnp.arange(tm)[None, :]).astype(BF16)
    n_pairs = FOX_HEADS // 2
    place = jnp.zeros((3, LANES, n_pairs * LANES), F32)
    for hp in range(n_pairs):
        for i in range(3):
            place = place.at[i, 2 * hp, hp * LANES + i].set(-1.0)
            place = place.at[i, 2 * hp + 1, hp * LANES + 3 + i].set(-1.0)
    place = place.astype(BF16)
    lane = jnp.arange(LANES)
    half = MLA_ROPE_DIM // 2
    in_pe = (lane >= MLA_NOPE_DIM) & (lane < MLA_NOPE_DIM + MLA_ROPE_DIM)
    freq = ROPE_THETA ** (-jnp.arange(0, MLA_ROPE_DIM, 2, dtype=F32) / MLA_ROPE_DIM)
    invf = jnp.where(in_pe, freq[(lane - MLA_NOPE_DIM) % half], 0.0)[None, :].astype(F32)
    m_lo = jnp.where(in_pe & (lane < MLA_NOPE_DIM + half), -1.0, 0.0)[None, :].astype(F32)
    m_hi = jnp.where(in_pe & (lane >= MLA_NOPE_DIM + half), 1.0, 0.0)[None, :].astype(F32)

    row = lambda b, s: (b * ns + s, 0)
    tcol = lambda b, s: (b, 0, s)
    full = lambda a: pl.BlockSpec(a.shape, lambda b, s: (0,) * a.ndim)
    fox_w = FOX_HEADS * HEAD_DIM
    vw = MLA_HEADS * MLA_V_DIM
    return pl.pallas_call(
        _l0_pre_kernel,
        grid=(B, ns),
        in_specs=[pl.BlockSpec((tm, D_MODEL), row), pl.BlockSpec((tm, 1), row),
                  full(g), full(w_ext), full(bf_row), full(qn), full(wq), full(kvn), full(wk),
                  full(wv), full(tri), full(place), full(invf), full(m_lo), full(m_hi)],
        out_specs=[pl.BlockSpec((tm, fox_w), row), pl.BlockSpec((tm, 2 * fox_w), row),
                   pl.BlockSpec((1, fox_w, tm), tcol),
                   pl.BlockSpec((tm, MLA_PAD), row), pl.BlockSpec((tm, MLA_PAD), row),
                   pl.BlockSpec((1, vw, tm), tcol),
                   pl.BlockSpec((1, FOX_HEADS, tm), tcol)],
        out_shape=[jax.ShapeDtypeStruct((N, fox_w), BF16),
                   jax.ShapeDtypeStruct((N, 2 * fox_w), BF16),
                   jax.ShapeDtypeStruct((B, fox_w, S), BF16),
                   jax.ShapeDtypeStruct((N, MLA_PAD), BF16),
                   jax.ShapeDtypeStruct((N, MLA_PAD), BF16),
                   jax.ShapeDtypeStruct((B, vw, S), BF16),
                   jax.ShapeDtypeStruct((B, FOX_HEADS, S), F32)],
        scratch_shapes=[pltpu.VMEM((1, LANES), F32)],
        compiler_params=_params("arbitrary", "arbitrary"),
        name="l0_pre",
    )(x2, pos2, g, w_ext, bf_row, qn, wq, kvn, wk, wv, tri, place, invf, m_lo, m_hi)


def _attn_kernel(*refs, fox, tq, tk, tw):
    if fox:
        q_ref, k_ref, vt_ref, cq_ref, ones_ref, o_ref, m_ref, l_ref, acc_ref = refs
    else:
        q_ref, k_ref, vt_ref, o_ref, m_ref, l_ref, acc_ref = refs
    qi = pl.program_id(2)
    n_strips = tq // tw

    q = q_ref[0]
    if fox:
        lane = lax.broadcasted_iota(jnp.int32, (1, LANES), 1)
        zero = jnp.zeros_like(q)
        qa = (jnp.concatenate([jnp.where(lane < HEAD_DIM, q, zero), ones_ref[0]], axis=1),
              jnp.concatenate([jnp.where(lane < HEAD_DIM, zero, q), ones_ref[1]], axis=1))
    else:
        qa = (q[:, :LANES], q[:, LANES:])

    m_ref[...] = jnp.full_like(m_ref, MASK_VALUE)
    l_ref[...] = jnp.zeros_like(l_ref)
    acc_ref[...] = jnp.zeros_like(acc_ref)

    sub = LANES

    def group(chunks):
        work = []
        for start, strips in chunks:
            kc = k_ref[0, pl.ds(start, tk), :]
            for i in range(2):
                for r, n_sub, mask_from in strips:
                    kh = kc[:n_sub * sub] if fox else kc[:n_sub * sub, i * LANES:(i + 1) * LANES]
                    s = _dot_nt(kh, qa[i][r * tw:(r + 1) * tw])
                    work.append((start, i, r, n_sub, mask_from, s))
        for start, i, r, n_sub, mask_from, s in work:
            tiles = []
            for c in range(n_sub):
                t = s[c * sub:(c + 1) * sub]
                if mask_from is not None and c >= mask_from:
                    keys = start + c * sub + lax.broadcasted_iota(jnp.int32, (sub, tw), 0)
                    qpos = qi * tq + r * tw + lax.broadcasted_iota(jnp.int32, (sub, tw), 1)
                    t = jnp.where(keys <= qpos, t, MASK_VALUE)
                tiles.append(t)
            col_max = tiles[0]
            for t in tiles[1:]:
                col_max = jnp.maximum(col_max, t)
            col_max = jnp.max(col_max, axis=0, keepdims=True)
            idx = i * n_strips + r
            m_prev = m_ref[idx]
            if fox:
                cq = cq_ref[0, 0, i:i + 1, r * tw:(r + 1) * tw]
                m_new = jnp.maximum(m_prev, col_max + cq)
                shift = m_new - cq
            else:
                m_new = jnp.maximum(m_prev, col_max)
                shift = m_new
            m_ref[idx] = m_new
            alpha = jnp.exp2(m_prev - m_new)
            psum = None
            probs = []
            for t in tiles:
                p = jnp.exp2(t - shift)
                psum = p if psum is None else psum + p
                probs.append(p.astype(BF16))
            l_ref[idx] = alpha * l_ref[idx] + jnp.sum(psum, axis=0, keepdims=True)
            vt = vt_ref[0, i * HEAD_DIM:(i + 1) * HEAD_DIM, pl.ds(start, n_sub * sub)]
            rows = slice(i * HEAD_DIM, (i + 1) * HEAD_DIM)
            cols = slice(r * tw, (r + 1) * tw)
            acc_ref[rows, cols] = (acc_ref[rows, cols] * alpha
                                   + _dot(vt, jnp.concatenate(probs, axis=0)))

    n_sub_full = tk // sub
    full_strips = [(r, n_sub_full, None) for r in range(n_strips)]
    per = tw // sub
    diag = []
    for d in range(tq // tk):
        strips = []
        for r in range(n_strips):
            n_sub = min(max((r + 1) * per - d * n_sub_full, 0), n_sub_full)
            if n_sub > 0:
                strips.append((r, n_sub, max(r * per - d * n_sub_full, 0)))
        diag.append((pl.multiple_of(qi * tq + d * tk, tk), strips))

    n_full = qi * (tq // tk)

    def full_group(j, carry):
        group([(pl.multiple_of((j * ATTN_GROUP + g) * tk, tk), full_strips)
               for g in range(ATTN_GROUP)])
        return carry

    lax.fori_loop(0, n_full // ATTN_GROUP, full_group, 0)
    for rem in range(ATTN_GROUP):
        @pl.when(n_full % ATTN_GROUP == rem)
        def _():
            base = n_full - rem
            group([(pl.multiple_of((base + g) * tk, tk), full_strips) for g in range(rem)] + diag)

    heads = []
    for i in range(2):
        rows = slice(i * HEAD_DIM, (i + 1) * HEAD_DIM)
        heads.append(jnp.concatenate(
            [acc_ref[rows, r * tw:(r + 1) * tw] * (1.0 / l_ref[i * n_strips + r])
             for r in range(n_strips)], axis=1))
    o_ref[0] = jnp.concatenate(heads, axis=0).T.astype(o_ref.dtype)


def _attention(q, k, vt, ct, B, S, tq, tk, tw, fox):
    n_pairs = FOX_HEADS // 2
    nq = S // tq
    qw = LANES if fox else 2 * LANES
    in_specs = [pl.BlockSpec((1, tq, qw), lambda b, h, i: (b, i, h)),
                pl.BlockSpec((1, S, 2 * LANES), lambda b, h, i: (b, 0, h)),
                pl.BlockSpec((1, LANES, S), lambda b, h, i: (b, h, 0))]
    args = [q, k, vt]
    if fox:
        in_specs.append(pl.BlockSpec((1, 1, 2, tq), lambda b, h, i: (b, h, 0, i)))
        args.append(ct)
        lane = jnp.arange(LANES)
        ones = jnp.stack([lane < 3, (lane >= 3) & (lane < 6)]).astype(BF16)
        in_specs.append(pl.BlockSpec((2, tq, LANES), lambda b, h, i: (0, 0, 0)))
        args.append(jnp.broadcast_to(ones[:, None, :], (2, tq, LANES)))
    n_strips = tq // tw
    return pl.pallas_call(
        functools.partial(_attn_kernel, fox=fox, tq=tq, tk=tk, tw=tw),
        grid=(B, n_pairs, nq),
        in_specs=in_specs,
        out_specs=pl.BlockSpec((1, tq, LANES), lambda b, h, i: (b, i, h)),
        out_shape=jax.ShapeDtypeStruct((B, S, n_pairs * LANES), BF16),
        scratch_shapes=[pltpu.VMEM((2 * n_strips, 1, tw), F32),
                        pltpu.VMEM((2 * n_strips, 1, tw), F32),
                        pltpu.VMEM((LANES, tq), F32)],
        compiler_params=_params("arbitrary", "arbitrary", "arbitrary"),
        name="fox_attn" if fox else "mla_attn",
    )(*args)


def _ffn_kernel(h_ref, a_ref, b_ref, wa_ref, wb_ref, g_ref, wg_ref, wu_ref, wd_ref,
                o_ref, hres_ref, xn_ref, acc_ref):
    f = pl.program_id(1)

    @pl.when(f == 0)
    def _():
        hres = h_ref[...] + _dot(a_ref[...], wa_ref[...]) + _dot(b_ref[...], wb_ref[...])
        hres_ref[...] = hres
        xn_ref[...] = _rms(hres, g_ref[...]).astype(BF16)
        acc_ref[...] = jnp.zeros_like(acc_ref)

    xn = xn_ref[...]
    gate = _dot(xn, wg_ref[...])
    up = _dot(xn, wu_ref[...])
    act = gate * (1.0 / (1.0 + jnp.exp(-gate))) * up
    acc_ref[...] += _dot(act.astype(BF16), wd_ref[...])

    @pl.when(f == pl.num_programs(1) - 1)
    def _():
        o_ref[...] = hres_ref[...] + acc_ref[...]


def _ffn(h, a, b, wa, wb, g, w_gate, w_up, w_down, tm, tf):
    N = h.shape[0]
    row = lambda i, f: (i, 0)
    full = lambda x: pl.BlockSpec(x.shape, lambda i, f: (0, 0))
    half_w = a.shape[1]
    return pl.pallas_call(
        _ffn_kernel,
        grid=(N // tm, D_FF // tf),
        in_specs=[pl.BlockSpec((tm, D_MODEL), row), pl.BlockSpec((tm, half_w), row),
                  pl.BlockSpec((tm, half_w), row), full(wa), full(wb), full(g),
                  pl.BlockSpec((D_MODEL, tf), lambda i, f: (0, f)),
                  pl.BlockSpec((D_MODEL, tf), lambda i, f: (0, f)),
                  pl.BlockSpec((tf, D_MODEL), lambda i, f: (f, 0))],
        out_specs=pl.BlockSpec((tm, D_MODEL), row),
        out_shape=jax.ShapeDtypeStruct((N, D_MODEL), F32),
        scratch_shapes=[pltpu.VMEM((tm, D_MODEL), F32), pltpu.VMEM((tm, D_MODEL), BF16),
                        pltpu.VMEM((tm, D_MODEL), F32)],
        compiler_params=_params("arbitrary", "arbitrary"),
        name="dense_ffn",
    )(h, a, b, wa, wb, g, w_gate, w_up, w_down)


def _router_kernel(h_ref, a_ref, b_ref, wa_ref, wb_ref, g_ref, wr_ref, tri_ref,
                   hres_ref, xn_ref, slab_ref, meta_ref, cnt_ref):
    @pl.when(pl.program_id(0) == 0)
    def _():
        cnt_ref[...] = jnp.zeros_like(cnt_ref)

    hres = h_ref[...] + _dot(a_ref[...], wa_ref[...]) + _dot(b_ref[...], wb_ref[...])
    hres_ref[...] = hres
    xn = _rms(hres, g_ref[...])
    xn_ref[...] = xn
    xh = xn.astype(BF16)
    xl = (xn - xh.astype(F32)).astype(BF16)
    wr = wr_ref[...]
    wh = wr.astype(BF16)
    wl = (wr - wh.astype(F32)).astype(BF16)
    logits = _dot(xh, wh) + (_dot(xh, wl) + _dot(xl, wh))
    lane = lax.broadcasted_iota(jnp.int32, logits.shape, 1)
    logits = jnp.where(lane < N_EXPERTS, logits, MASK_VALUE)
    m1 = jnp.max(logits, axis=1, keepdims=True)
    i1 = jnp.min(jnp.where(logits == m1, lane, LANES), axis=1, keepdims=True)
    rest = jnp.where(lane == i1, MASK_VALUE, logits)
    m2 = jnp.max(rest, axis=1, keepdims=True)
    i2 = jnp.min(jnp.where(rest == m2, lane, LANES), axis=1, keepdims=True)
    e2 = jnp.exp(m2 - m1)
    w1 = 1.0 / (1.0 + e2)
    w2 = e2 / (1.0 + e2)
    hot1 = lane == i1
    hot2 = lane == i2
    onehot = jnp.where(hot1 | hot2, 1.0, 0.0)
    before = _dot(tri_ref[...], onehot.astype(BF16)) + cnt_ref[...]
    r1 = jnp.sum(jnp.where(hot1, before, 0.0), axis=1, keepdims=True)
    r2 = jnp.sum(jnp.where(hot2, before, 0.0), axis=1, keepdims=True)
    cnt_ref[...] += jnp.sum(onehot, axis=0, keepdims=True)
    slab = (jnp.where(lane == 0, i1.astype(F32), 0.0) + jnp.where(lane == 1, i2.astype(F32), 0.0)
            + jnp.where(lane == 2, r1, 0.0) + jnp.where(lane == 3, r2, 0.0)
            + jnp.where(lane == 4, w1, 0.0) + jnp.where(lane == 5, w2, 0.0))
    slab_ref[...] = slab
    meta_ref[...] = slab.T[:8, :]


def _router(h, a, b, wa, wb, g, w_router, tm):
    N = h.shape[0]
    tri = (jnp.arange(tm)[:, None] > jnp.arange(tm)[None, :]).astype(BF16)
    row = lambda i: (i, 0)
    full = lambda x: pl.BlockSpec(x.shape, lambda i: (0, 0))
    half_w = a.shape[1]
    return pl.pallas_call(
        _router_kernel,
        grid=(N // tm,),
        in_specs=[pl.BlockSpec((tm, D_MODEL), row), pl.BlockSpec((tm, half_w), row),
                  pl.BlockSpec((tm, half_w), row), full(wa), full(wb), full(g), full(w_router),
                  full(tri)],
        out_specs=[pl.BlockSpec((tm, D_MODEL), row), pl.BlockSpec((tm, D_MODEL), row),
                   pl.BlockSpec((tm, LANES), row), pl.BlockSpec((8, tm), lambda i: (0, i)),
                   pl.BlockSpec((1, LANES), lambda i: (0, 0))],
        out_shape=[jax.ShapeDtypeStruct((N, D_MODEL), F32), jax.ShapeDtypeStruct((N, D_MODEL), F32),
                   jax.ShapeDtypeStruct((N, LANES), F32), jax.ShapeDtypeStruct((8, N), F32),
                   jax.ShapeDtypeStruct((1, LANES), F32)],
        compiler_params=_params("arbitrary"),
        name="moe_router",
    )(h, a, b, wa, wb, g, w_router, tri)


def _row_copy(src, src_row, dst, dst_row, sem):
    return pltpu.make_async_copy(src.at[pl.ds(src_row, 1), :], dst.at[pl.ds(dst_row, 1), :], sem)


def _dispatch_kernel(ends_ref, cap_ref, pos_hbm, xn_ref, xs_hbm, pos_smem, zero_ref, idx_sem,
                     row_sem, zero_sem, *, tm, tr):
    i = pl.program_id(0)

    @pl.when(i == 0)
    def _():
        zero_ref[...] = jnp.zeros_like(zero_ref)

        def clear(start):
            copy = pltpu.make_async_copy(zero_ref, xs_hbm.at[pl.ds(start, tr), :], zero_sem)
            copy.start()
            copy.wait()

        for e in range(N_EXPERTS):
            @pl.when(cap_ref[e] > 0)
            def _():
                clear(pl.multiple_of(ends_ref[e] - tr, 8))
        n_rows = xs_hbm.shape[0]
        for t in range(N_EXPERTS):
            start = n_rows - (t + 1) * tr

            @pl.when(start >= ends_ref[N_EXPERTS - 1])
            def _():
                clear(start)

    idx_copy = pltpu.make_async_copy(pos_hbm.at[i], pos_smem, idx_sem)
    idx_copy.start()
    idx_copy.wait()

    def issue(j, carry):
        base = pl.multiple_of(j * 8, 8)
        for u in range(8):
            for k in range(2):
                _row_copy(xn_ref, base + u, xs_hbm, pos_smem[k * tm + base + u], row_sem).start()
        return carry

    lax.fori_loop(0, tm // 8, issue, 0)
    for k in range(2):
        pltpu.make_async_copy(xn_ref, xs_hbm.at[pl.ds(0, tm), :], row_sem).wait()


def _dispatch(ends, cap, pos_tiles, xn, n_rows, tm, tr):
    N = xn.shape[0]
    grid_spec = pltpu.PrefetchScalarGridSpec(
        num_scalar_prefetch=2,
        grid=(N // tm,),
        in_specs=[pl.BlockSpec(memory_space=pl.ANY),
                  pl.BlockSpec((tm, D_MODEL), lambda i, ends, cap: (i, 0))],
        out_specs=pl.BlockSpec(memory_space=pl.ANY),
        scratch_shapes=[pltpu.SMEM((2 * tm,), jnp.int32), pltpu.VMEM((tr, D_MODEL), F32),
                        pltpu.SemaphoreType.DMA, pltpu.SemaphoreType.DMA,
                        pltpu.SemaphoreType.DMA])
    return pl.pallas_call(
        functools.partial(_dispatch_kernel, tm=tm, tr=tr),
        grid_spec=grid_spec,
        out_shape=jax.ShapeDtypeStruct((n_rows, D_MODEL), F32),
        compiler_params=_params("arbitrary"),
        name="moe_dispatch",
    )(ends, cap, pos_tiles, xn)


def _expert_kernel(te_ref, tv_ref, x_ref, wg_ref, wu_ref, wd_ref, y_ref, xb_ref, acc_ref):
    del te_ref
    i = pl.program_id(0)
    f = pl.program_id(1)
    live = tv_ref[i] > 0

    @pl.when(live & (f == 0))
    def _():
        xb_ref[...] = x_ref[...].astype(BF16)
        acc_ref[...] = jnp.zeros_like(acc_ref)

    @pl.when(live)
    def _():
        xb = xb_ref[...]
        gate = _dot(xb, wg_ref[0].astype(BF16))
        up = _dot(xb, wu_ref[0].astype(BF16))
        act = gate * (1.0 / (1.0 + jnp.exp(-gate))) * up
        acc_ref[...] += _dot(act.astype(BF16), wd_ref[0].astype(BF16))

    @pl.when(f == pl.num_programs(1) - 1)
    def _():
        y_ref[...] = jnp.where(live, acc_ref[...], 0.0)


def _experts(tile_expert, tile_valid, xs, w_gate, w_up, w_down, tr, tf):
    n_rows = xs.shape[0]
    nf = D_FF // tf
    fsel = lambda f, i, tv: jnp.where(tv[i] > 0, f, 0)
    grid_spec = pltpu.PrefetchScalarGridSpec(
        num_scalar_prefetch=2,
        grid=(n_rows // tr, nf),
        in_specs=[pl.BlockSpec((tr, D_MODEL), lambda i, f, te, tv: (jnp.where(tv[i] > 0, i, 0), 0)),
                  pl.BlockSpec((1, D_MODEL, tf), lambda i, f, te, tv: (te[i], 0, fsel(f, i, tv))),
                  pl.BlockSpec((1, D_MODEL, tf), lambda i, f, te, tv: (te[i], 0, fsel(f, i, tv))),
                  pl.BlockSpec((1, tf, D_MODEL), lambda i, f, te, tv: (te[i], fsel(f, i, tv), 0))],
        out_specs=pl.BlockSpec((tr, D_MODEL), lambda i, f, te, tv: (i, 0)),
        scratch_shapes=[pltpu.VMEM((tr, D_MODEL), BF16), pltpu.VMEM((tr, D_MODEL), F32)])
    return pl.pallas_call(
        _expert_kernel,
        grid_spec=grid_spec,
        out_shape=jax.ShapeDtypeStruct((n_rows, D_MODEL), F32),
        compiler_params=_params("arbitrary", "arbitrary"),
        name="moe_experts",
    )(tile_expert, tile_valid, xs, w_gate, w_up, w_down)


def _combine_kernel(pos_hbm, hres_ref, slab_ref, fn_ref, ys_hbm, o_ref, pos_smem, ybuf_ref,
                    idx_sem, row_sem, *, tm):
    i = pl.program_id(0)
    idx_copy = pltpu.make_async_copy(pos_hbm.at[i], pos_smem, idx_sem)
    idx_copy.start()
    idx_copy.wait()

    def issue(j, carry):
        base = pl.multiple_of(j * 8, 8)
        for u in range(8):
            for k in range(2):
                _row_copy(ys_hbm, pos_smem[k * tm + base + u], ybuf_ref.at[k], base + u, row_sem).start()
        return carry

    lax.fori_loop(0, tm // 8, issue, 0)
    for k in range(2):
        pltpu.make_async_copy(ys_hbm.at[pl.ds(0, tm), :], ybuf_ref.at[k], row_sem).wait()

    slab = slab_ref[...]
    lane = lax.broadcasted_iota(jnp.int32, slab.shape, 1)
    w1 = jnp.sum(jnp.where(lane == 4, slab, 0.0), axis=1, keepdims=True)
    w2 = jnp.sum(jnp.where(lane == 5, slab, 0.0), axis=1, keepdims=True)
    out = hres_ref[...] + w1 * ybuf_ref[0] + w2 * ybuf_ref[1]
    o_ref[...] = _rms(out, fn_ref[...])


def _combine(pos_tiles, hres, slab, final_g, ys, tm):
    N = hres.shape[0]
    row = lambda i: (i, 0)
    return pl.pallas_call(
        functools.partial(_combine_kernel, tm=tm),
        grid=(N // tm,),
        in_specs=[pl.BlockSpec(memory_space=pl.ANY), pl.BlockSpec((tm, D_MODEL), row),
                  pl.BlockSpec((tm, LANES), row), pl.BlockSpec(final_g.shape, lambda i: (0, 0)),
                  pl.BlockSpec(memory_space=pl.ANY)],
        out_specs=pl.BlockSpec((tm, D_MODEL), row),
        out_shape=jax.ShapeDtypeStruct((N, D_MODEL), F32),
        scratch_shapes=[pltpu.SMEM((2 * tm,), jnp.int32), pltpu.VMEM((2, tm, D_MODEL), F32),
                        pltpu.SemaphoreType.DMA, pltpu.SemaphoreType.DMA],
        compiler_params=_params("arbitrary"),
        name="moe_combine",
    )(pos_tiles, hres, slab, final_g, ys)


def _moe(h, a, b, wa, wb, g, w_router, w_gate, w_up, w_down, final_g):
    N = h.shape[0]
    tm, tr = MOE_TM, MOE_TR
    hres, xn, slab, meta, cnt = _router(h, a, b, wa, wb, g, w_router, tm)
    e1, e2, r1, r2 = (meta[j].astype(jnp.int32) for j in range(4))
    counts = cnt[0, :N_EXPERTS].astype(jnp.int32)
    cap = (counts + tr - 1) // tr * tr
    ends = jnp.cumsum(cap)
    offs = ends - cap
    pos = jnp.stack([offs[e1] + r1, offs[e2] + r2])
    pos_tiles = pos.reshape(2, N // tm, tm).transpose(1, 0, 2).reshape(N // tm, 2 * tm)
    n_rows = 2 * N + N_EXPERTS * tr
    tile_start = jnp.arange(n_rows // tr, dtype=jnp.int32) * tr
    tile_expert = jnp.minimum(jnp.sum(tile_start[:, None] >= ends[None, :], axis=1),
                              N_EXPERTS - 1).astype(jnp.int32)
    tile_valid = jnp.clip(counts[tile_expert] - (tile_start - offs[tile_expert]), 0, tr)
    tile_valid = jnp.where(tile_start < ends[N_EXPERTS - 1], tile_valid, 0).astype(jnp.int32)
    xs = _dispatch(ends.astype(jnp.int32), cap.astype(jnp.int32), pos_tiles, xn, n_rows, tm, tr)
    ys = _experts(tile_expert, tile_valid, xs, w_gate, w_up, w_down, tr, MOE_TF)
    return _combine(pos_tiles, hres, slab, final_g, ys, tm)


def _l1_pre_kernel(x_ref, pos_ref, g_ref, w_ref, cw_ref, invf_ref, mlo_ref, mhi_ref,
                   q_ref, k_ref, vt_ref, oc_ref, tail_ref):
    @pl.when(pl.program_id(1) == 0)
    def _():
        tail_ref[...] = jnp.zeros_like(tail_ref)

    xn = _rms(x_ref[...], g_ref[...]).astype(BF16)
    y = _dot(xn, w_ref[...])
    ang = pos_ref[...].astype(F32) * invf_ref[...]
    cos = jnp.cos(ang)
    sin = jnp.sin(ang)
    sin_lo = sin * mlo_ref[...]
    sin_hi = sin * mhi_ref[...]
    half = HEAD_DIM // 2
    qw = SWA_HEADS * HEAD_DIM
    for j in range(qw // LANES):
        sl = slice(j * LANES, (j + 1) * LANES)
        q_ref[:, sl] = _rope(y[:, sl], cos, sin_lo, sin_hi, half).astype(BF16)
    k_ref[...] = _rope(y[:, qw:qw + LANES], cos, sin_lo, sin_hi, half).astype(BF16)
    vt_ref[0] = y[:, qw + LANES:qw + 2 * LANES].T.astype(BF16)

    o = qw + 2 * LANES
    C = CONV_CHANNELS
    gate_b = y[:, o:o + C]
    u = y[:, o + C:o + 2 * C] * y[:, o + 2 * C:o + 3 * C]
    tm = u.shape[0]
    rows = lax.broadcasted_iota(jnp.int32, u.shape, 0)
    tail = tail_ref[...]
    u1 = jnp.where(rows == 0, tail[7:8, :], pltpu.roll(u, 1, 0))
    u2 = jnp.where(rows == 0, tail[6:7, :], jnp.where(rows == 1, tail[7:8, :], pltpu.roll(u, 2, 0)))
    cw = cw_ref[...]
    oc_ref[...] = (gate_b * (cw[0:1, :] * u2 + cw[1:2, :] * u1 + cw[2:3, :] * u)).astype(BF16)
    tail_ref[...] = u[tm - 8:tm, :]


def _l1_pre(x2, pos2, g, w_in, conv_w, B, S, tm):
    N = B * S
    ns = S // tm
    lane = jnp.arange(LANES)
    half = HEAD_DIM // 2
    freq = ROPE_THETA ** (-jnp.arange(0, HEAD_DIM, 2, dtype=F32) / HEAD_DIM)
    invf = freq[lane % half][None, :].astype(F32)
    m_lo = jnp.where(lane % HEAD_DIM < half, -1.0, 0.0)[None, :].astype(F32)
    m_hi = jnp.where(lane % HEAD_DIM >= half, 1.0, 0.0)[None, :].astype(F32)
    row = lambda b, s: (b * ns + s, 0)
    const = lambda b, s: (0, 0)
    full = lambda a: pl.BlockSpec(a.shape, const)
    return pl.pallas_call(
        _l1_pre_kernel,
        grid=(B, ns),
        in_specs=[pl.BlockSpec((tm, D_MODEL), row), pl.BlockSpec((tm, 1), row),
                  full(g), full(w_in), full(conv_w), full(invf), full(m_lo), full(m_hi)],
        out_specs=[pl.BlockSpec((tm, SWA_HEADS * HEAD_DIM), row),
                   pl.BlockSpec((tm, LANES), row),
                   pl.BlockSpec((1, LANES, tm), lambda b, s: (b, 0, s)),
                   pl.BlockSpec((tm, CONV_CHANNELS), row)],
        out_shape=[jax.ShapeDtypeStruct((N, SWA_HEADS * HEAD_DIM), BF16),
                   jax.ShapeDtypeStruct((N, LANES), BF16),
                   jax.ShapeDtypeStruct((B, LANES, S), BF16),
                   jax.ShapeDtypeStruct((N, CONV_CHANNELS), BF16)],
        scratch_shapes=[pltpu.VMEM((8, CONV_CHANNELS), F32)],
        compiler_params=_params("arbitrary", "arbitrary"),
        name="l1_pre",
    )(x2, pos2, g, w_in, conv_w, invf, m_lo, m_hi)


def _swa_kernel(sink_ref, q_ref, k_ref, kp_ref, vt_ref, vtp_ref, o_ref, *, tq):
    s_idx = pl.program_id(1)
    first = lax.broadcasted_iota(jnp.int32, (1, LANES), 1) < HEAD_DIM
    G = SWA_HEADS // SWA_KV_HEADS
    W = WINDOW

    k_all = jnp.concatenate([kp_ref[0], k_ref[0]], axis=0)
    k_roll = pltpu.roll(k_all, HEAD_DIM, 1)
    k_dup = (jnp.where(first, k_all, k_roll), jnp.where(first, k_roll, k_all))
    vt_all = jnp.concatenate([vtp_ref[0], vt_ref[0]], axis=1)
    q = q_ref[0]
    zero = jnp.zeros((W, LANES), q.dtype)
    cols = G * W
    key_i = lax.broadcasted_iota(jnp.int32, (2 * W, cols), 0)
    col_i = lax.broadcasted_iota(jnp.int32, (2 * W, cols), 1)
    rel = col_i % W + W - key_i
    in_window = (rel >= 0) & (rel < W)
    lane = lax.broadcasted_iota(jnp.int32, (1, cols), 1)
    for r in range(tq // W):
        keep = in_window & (s_idx * tq - W + key_i >= 0) if r == 0 else in_window
        outs = []
        for g in range(SWA_KV_HEADS):
            kb = k_dup[g][r * W:r * W + 2 * W]
            vtb = vt_all[g * HEAD_DIM:(g + 1) * HEAD_DIM, r * W:r * W + 2 * W]
            parts = []
            sink = jnp.zeros((1, cols), F32)
            for j in range(G // 2):
                blk = g * (G // 2) + j
                q2 = q[r * W:(r + 1) * W, blk * LANES:(blk + 1) * LANES]
                parts += [jnp.where(first, q2, zero), jnp.where(first, zero, q2)]
                for half in range(2):
                    c0 = (2 * j + half) * W
                    sink = jnp.where((lane >= c0) & (lane < c0 + W), sink_ref[2 * blk + half], sink)
            qst = jnp.concatenate(parts, axis=0)
            s = _dot_nt(kb, qst)
            s = jnp.where(keep, s, MASK_VALUE)
            m = jnp.maximum(jnp.max(s, axis=0, keepdims=True), sink)
            e = jnp.exp(s - m)
            den = jnp.sum(e, axis=0, keepdims=True) + jnp.exp(sink - m)
            ot = _dot(vtb, (e * (1.0 / den)).astype(BF16))
            outs += [ot[:, j * W:(j + 1) * W] for j in range(G)]
        o_ref[0, r * W:(r + 1) * W, :] = jnp.concatenate(outs, axis=0).T.astype(o_ref.dtype)


def _swa(sinks, q, k, vt, B, S, tq):
    ns = S // tq
    per = tq // WINDOW
    prev = lambda s: jnp.maximum(s * per - 1, 0)
    return pl.pallas_call(
        functools.partial(_swa_kernel, tq=tq),
        grid=(B, ns),
        in_specs=[pl.BlockSpec(memory_space=pltpu.SMEM),
                  pl.BlockSpec((1, tq, SWA_HEADS * HEAD_DIM), lambda b, s: (b, s, 0)),
                  pl.BlockSpec((1, tq, LANES), lambda b, s: (b, s, 0)),
                  pl.BlockSpec((1, WINDOW, LANES), lambda b, s: (b, prev(s), 0)),
                  pl.BlockSpec((1, LANES, tq), lambda b, s: (b, 0, s)),
                  pl.BlockSpec((1, LANES, WINDOW), lambda b, s: (b, 0, prev(s)))],
        out_specs=pl.BlockSpec((1, tq, SWA_HEADS * HEAD_DIM), lambda b, s: (b, s, 0)),
        out_shape=jax.ShapeDtypeStruct((B, S, SWA_HEADS * HEAD_DIM), BF16),
        compiler_params=_params("arbitrary", "arbitrary"),
        name="swa_attn",
    )(sinks, q, k, k, vt, vt)


def _pad_heads(w, n_heads, width, offset=0):
    K = w.shape[0]
    w3 = w.reshape(K, n_heads, width)
    out = jnp.zeros((K, n_heads, LANES), w.dtype).at[:, :, offset:offset + width].set(w3)
    return out.reshape(K, n_heads * LANES)


def kernel(x, positions, l0_norm_mix, l0_w_in, l0_b_forget, l0_q_norm, l0_w_q_up, l0_kv_norm,
           l0_w_kv_up, l0_w_out, l0_norm_ffn, l0_w_gate, l0_w_up, l0_w_down,
           l1_norm_mix, l1_w_in, l1_sinks, l1_conv_w, l1_w_out, l1_norm_ffn, l1_w_router,
           l1_w_gate, l1_w_up, l1_w_down, final_norm):
    B, S, D = x.shape
    assert D == D_MODEL and S % 512 == 0
    N = B * S
    tm = 512
    x2 = x.reshape(N, D)
    pos2 = positions.reshape(N, 1)
    rowv = lambda v: v.reshape(1, -1).astype(F32)

    fox_w = FOX_HEADS * HEAD_DIM
    o = 3 * fox_w
    w_fq = l0_w_in[:, :fox_w] * (HEAD_DIM ** -0.5 * LOG2E)
    w_f = l0_w_in[:, o:o + FOX_HEADS]
    o += FOX_HEADS
    w_cq = l0_w_in[:, o:o + MLA_Q_RANK]
    o += MLA_Q_RANK
    w_ckv = l0_w_in[:, o:o + MLA_KV_RANK]
    o += MLA_KV_RANK
    w_kr = l0_w_in[:, o:o + MLA_ROPE_DIM]
    w_ext = jnp.concatenate(
        [w_fq, l0_w_in[:, fox_w:3 * fox_w], w_cq, w_ckv,
         _pad_heads(w_kr, 1, MLA_ROPE_DIM, MLA_NOPE_DIM), _pad_heads(w_f, 1, FOX_HEADS)],
        axis=1).astype(BF16)
    bf_row = jnp.zeros((1, LANES), F32).at[0, :FOX_HEADS].set(l0_b_forget.astype(F32))
    qk_dim = MLA_NOPE_DIM + MLA_ROPE_DIM
    wq = _pad_heads(l0_w_q_up * (qk_dim ** -0.5 * LOG2E), MLA_HEADS, qk_dim).astype(BF16)
    kv3 = l0_w_kv_up.reshape(MLA_KV_RANK, MLA_HEADS, MLA_NOPE_DIM + MLA_V_DIM)
    wk = _pad_heads(kv3[:, :, :MLA_NOPE_DIM].reshape(MLA_KV_RANK, -1), MLA_HEADS,
                    MLA_NOPE_DIM).astype(BF16)
    wv = kv3[:, :, MLA_NOPE_DIM:].reshape(MLA_KV_RANK, -1).astype(BF16)

    q_f, k_f, vt_f, q_m, k_m, vt_m, ct = _l0_pre(
        x2, pos2, rowv(l0_norm_mix), w_ext, bf_row, rowv(l0_q_norm), wq, rowv(l0_kv_norm),
        wk, wv, B, S, tm)

    tq, tk, tw = ATTN_TQ, ATTN_TK, ATTN_TW
    o_fox = _attention(q_f.reshape(B, S, fox_w), k_f.reshape(B, S, 2 * fox_w), vt_f,
                       ct.reshape(B, FOX_HEADS // 2, 2, S), B, S, tq, tk, tw, True)
    o_mla = _attention(q_m.reshape(B, S, MLA_PAD), k_m.reshape(B, S, MLA_PAD), vt_m, None,
                       B, S, tq, tk, tw, False)

    w_out0 = l0_w_out.astype(BF16)
    h1 = _ffn(x2, o_fox.reshape(N, -1), o_mla.reshape(N, -1), w_out0[:fox_w], w_out0[fox_w:],
              rowv(l0_norm_ffn), l0_w_gate.astype(BF16), l0_w_up.astype(BF16),
              l0_w_down.astype(BF16), FFN_TM, FFN_TF)

    swa_w = SWA_HEADS * HEAD_DIM
    w_in1 = jnp.concatenate([l1_w_in[:, :swa_w] * (HEAD_DIM ** -0.5), l1_w_in[:, swa_w:]],
                            axis=1).astype(BF16)
    q_s, k_s, vt_s, o_conv = _l1_pre(h1, pos2, rowv(l1_norm_mix), w_in1, l1_conv_w.astype(F32),
                                     B, S, tm)
    o_swa = _swa(l1_sinks.astype(F32), q_s.reshape(B, S, swa_w), k_s.reshape(B, S, LANES), vt_s,
                 B, S, 512)
    w_out1 = l1_w_out.astype(BF16)
    w_router = jnp.zeros((D_MODEL, LANES), F32).at[:, :N_EXPERTS].set(l1_w_router.astype(F32))
    out = _moe(h1, o_swa.reshape(N, -1), o_conv, w_out1[:swa_w], w_out1[swa_w:],
               rowv(l1_norm_ffn), w_router, l1_w_gate, l1_w_up, l1_w_down, rowv(final_norm))
    return out.reshape(B, S, D)
```

```python
import functools

import jax
import jax.numpy as jnp
from jax import lax
from jax.experimental import pallas as pl
from jax.experimental.pallas import tpu as pltpu

F32 = jnp.float32
BF16 = jnp.bfloat16

D_MODEL = 1024
HEAD_DIM = 64
RMS_EPS = 1e-6
ROPE_THETA = 10000.0
FOX_HEADS = 8
MLA_HEADS = 8
MLA_Q_RANK = 384
MLA_KV_RANK = 256
MLA_NOPE_DIM = 64
MLA_ROPE_DIM = 32
MLA_V_DIM = 64
SWA_HEADS = 8
SWA_KV_HEADS = 2
WINDOW = 128
CONV_CHANNELS = 512
CONV_WIDTH = 3
D_FF = 3584
N_EXPERTS = 8

LANES = 128
MASK_VALUE = -1e30
LOG2E = 1.4426950408889634
VMEM_LIMIT = 56 * 1024 * 1024

QKV_W = 3 * FOX_HEADS * HEAD_DIM
MLA_PAD = MLA_HEADS * LANES
L0_COLS = QKV_W + MLA_Q_RANK + MLA_KV_RANK + 2 * LANES
ATTN_TQ = 1024
ATTN_TK = 512
ATTN_TW = 256
ATTN_GROUP = 4
MOE_TM = 512
MOE_TR = 1024
MOE_TF = 512
FFN_TM = 1024
FFN_TF = 512
L1_COLS = SWA_HEADS * HEAD_DIM + 2 * SWA_KV_HEADS * HEAD_DIM + 3 * CONV_CHANNELS


def _params(*sem):
    return pltpu.CompilerParams(dimension_semantics=sem, vmem_limit_bytes=VMEM_LIMIT)


def _rms(x, g):
    return x * lax.rsqrt(jnp.mean(x * x, axis=-1, keepdims=True) + RMS_EPS) * g


def _dot(a, b):
    return jnp.dot(a, b, preferred_element_type=F32)


def _dot_nt(a, b):
    return lax.dot_general(a, b, (((1,), (1,)), ((), ())), preferred_element_type=F32)


def _rope(x, cos, sin_lo, sin_hi, half):
    return (x * cos + pltpu.roll(x, half, 1) * sin_hi
            + pltpu.roll(x, LANES - half, 1) * sin_lo)


def _l0_pre_kernel(x_ref, pos_ref, g_ref, w_ref, bf_ref, qn_ref, wq_ref, kvn_ref, wk_ref,
                   wv_ref, tri_ref, place_ref, invf_ref, mlo_ref, mhi_ref,
                   qf_ref, kf_ref, vtf_ref, q_ref, k_ref, vt_ref, ct_ref, carry_ref):
    @pl.when(pl.program_id(1) == 0)
    def _():
        carry_ref[...] = jnp.zeros_like(carry_ref)

    fox_w = FOX_HEADS * HEAD_DIM
    xn = _rms(x_ref[...], g_ref[...]).astype(BF16)
    y = _dot(xn, w_ref[...])
    qf_ref[...] = y[:, :fox_w].astype(BF16)
    vtf_ref[0] = y[:, 2 * fox_w:3 * fox_w].T.astype(BF16)
    o = QKV_W
    cq = y[:, o:o + MLA_Q_RANK]
    o += MLA_Q_RANK
    ckv = y[:, o:o + MLA_KV_RANK]
    o += MLA_KV_RANK
    kr = y[:, o:o + LANES]
    fl = y[:, o + LANES:o + 2 * LANES]

    q = _dot(_rms(cq, qn_ref[...]).astype(BF16), wq_ref[...])
    ckn = _rms(ckv, kvn_ref[...]).astype(BF16)
    kk = _dot(ckn, wk_ref[...])
    vt_ref[0] = _dot(ckn, wv_ref[...]).T.astype(BF16)

    ang = pos_ref[...].astype(F32) * invf_ref[...]
    cos = jnp.cos(ang)
    sin = jnp.sin(ang)
    sin_lo = sin * mlo_ref[...]
    sin_hi = sin * mhi_ref[...]
    half = MLA_ROPE_DIM // 2
    kpe = _rope(kr, cos, sin_lo, sin_hi, half)
    for h in range(MLA_HEADS):
        sl = slice(h * LANES, (h + 1) * LANES)
        q_ref[:, sl] = _rope(q[:, sl], cos, sin_lo, sin_hi, half).astype(BF16)
        k_ref[:, sl] = (kk[:, sl] + kpe).astype(BF16)

    z = fl + bf_ref[...]
    ls = jnp.minimum(z, 0.0) - jnp.log1p(jnp.exp(-jnp.abs(z)))
    tri = tri_ref[...]
    c = carry_ref[...]
    for part in _split3(ls):
        c = c + _dot(tri, part)
    tm = c.shape[0]
    carry_ref[...] = c[tm - 1:tm, :]
    c2 = c * LOG2E
    ct_ref[0] = c2.T[:FOX_HEADS, :]
    bias = None
    for i, part in enumerate(_split3(c2)):
        d = _dot(part, place_ref[i])
        bias = d if bias is None else bias + d
    for hp in range(FOX_HEADS // 2):
        kf_ref[:, 2 * hp * LANES:(2 * hp + 1) * LANES] = (
            y[:, fox_w + hp * LANES:fox_w + (hp + 1) * LANES].astype(BF16))
        kf_ref[:, (2 * hp + 1) * LANES:(2 * hp + 2) * LANES] = (
            bias[:, hp * LANES:(hp + 1) * LANES].astype(BF16))


def _split3(x):
    hi = x.astype(BF16)
    r1 = x - hi.astype(F32)
    mid = r1.astype(BF16)
    lo = (r1 - mid.astype(F32)).astype(BF16)
    return hi, mid, lo


def _l0_pre(x2, pos2, g, w_ext, bf_row, qn, wq, kvn, wk, wv, B, S, tm):
    N = B * S
    ns = S // tm
    tri = (jnp.arange(tm)[:, None] >= jnp.arange(tm)[None, :]).astype(BF16)
    n_pairs = FOX_HEADS // 2
    place = jnp.zeros((3, LANES, n_pairs * LANES), F32)
    for hp in range(n_pairs):
        for i in range(3):
            place = place.at[i, 2 * hp, hp * LANES + i].set(-1.0)
            place = place.at[i, 2 * hp + 1, hp * LANES + 3 + i].set(-1.0)
    place = place.astype(BF16)
    lane = jnp.arange(LANES)
    half = MLA_ROPE_DIM // 2
    in_pe = (lane >= MLA_NOPE_DIM) & (lane < MLA_NOPE_DIM + MLA_ROPE_DIM)
    freq = ROPE_THETA ** (-jnp.arange(0, MLA_ROPE_DIM, 2, dtype=F32) / MLA_ROPE_DIM)
    invf = jnp.where(in_pe, freq[(lane - MLA_NOPE_DIM) % half], 0.0)[None, :].astype(F32)
    m_lo = jnp.where(in_pe & (lane < MLA_NOPE_DIM + half), -1.0, 0.0)[None, :].astype(F32)
    m_hi = jnp.where(in_pe & (lane >= MLA_NOPE_DIM + half), 1.0, 0.0)[None, :].astype(F32)

    row = lambda b, s: (b * ns + s, 0)
    tcol = lambda b, s: (b, 0, s)
    full = lambda a: pl.BlockSpec(a.shape, lambda b, s: (0,) * a.ndim)
    fox_w = FOX_HEADS * HEAD_DIM
    vw = MLA_HEADS * MLA_V_DIM
    return pl.pallas_call(
        _l0_pre_kernel,
        grid=(B, ns),
        in_specs=[pl.BlockSpec((tm, D_MODEL), row), pl.BlockSpec((tm, 1), row),
                  full(g), full(w_ext), full(bf_row), full(qn), full(wq), full(kvn), full(wk),
                  full(wv), full(tri), full(place), full(invf), full(m_lo), full(m_hi)],
        out_specs=[pl.BlockSpec((tm, fox_w), row), pl.BlockSpec((tm, 2 * fox_w), row),
                   pl.BlockSpec((1, fox_w, tm), tcol),
                   pl.BlockSpec((tm, MLA_PAD), row), pl.BlockSpec((tm, MLA_PAD), row),
                   pl.BlockSpec((1, vw, tm), tcol),
                   pl.BlockSpec((1, FOX_HEADS, tm), tcol)],
        out_shape=[jax.ShapeDtypeStruct((N, fox_w), BF16),
                   jax.ShapeDtypeStruct((N, 2 * fox_w), BF16),
                   jax.ShapeDtypeStruct((B, fox_w, S), BF16),
                   jax.ShapeDtypeStruct((N, MLA_PAD), BF16),
                   jax.ShapeDtypeStruct((N, MLA_PAD), BF16),
                   jax.ShapeDtypeStruct((B, vw, S), BF16),
                   jax.ShapeDtypeStruct((B, FOX_HEADS, S), F32)],
        scratch_shapes=[pltpu.VMEM((1, LANES), F32)],
        compiler_params=_params("arbitrary", "arbitrary"),
        name="l0_pre",
    )(x2, pos2, g, w_ext, bf_row, qn, wq, kvn, wk, wv, tri, place, invf, m_lo, m_hi)


def _attn_kernel(*refs, fox, tq, tk, tw):
    if fox:
        q_ref, k_ref, vt_ref, cq_ref, ones_ref, o_ref, m_ref, l_ref, acc_ref = refs
    else:
        q_ref, k_ref, vt_ref, o_ref, m_ref, l_ref, acc_ref = refs
    qi = pl.program_id(2)
    n_strips = tq // tw

    q = q_ref[0]
    if fox:
        lane = lax.broadcasted_iota(jnp.int32, (1, LANES), 1)
        zero = jnp.zeros_like(q)
        qa = (jnp.concatenate([jnp.where(lane < HEAD_DIM, q, zero), ones_ref[0]], axis=1),
              jnp.concatenate([jnp.where(lane < HEAD_DIM, zero, q), ones_ref[1]], axis=1))
    else:
        qa = (q[:, :LANES], q[:, LANES:])

    m_ref[...] = jnp.full_like(m_ref, MASK_VALUE)
    l_ref[...] = jnp.zeros_like(l_ref)
    acc_ref[...] = jnp.zeros_like(acc_ref)

    sub = LANES

    def group(chunks):
        work = []
        for start, strips in chunks:
            kc = k_ref[0, pl.ds(start, tk), :]
            for i in range(2):
                for r, n_sub, mask_from in strips:
                    kh = kc[:n_sub * sub] if fox else kc[:n_sub * sub, i * LANES:(i + 1) * LANES]
                    s = _dot_nt(kh, qa[i][r * tw:(r + 1) * tw])
                    work.append((start, i, r, n_sub, mask_from, s))
        for start, i, r, n_sub, mask_from, s in work:
            tiles = []
            for c in range(n_sub):
                t = s[c * sub:(c + 1) * sub]
                if mask_from is not None and c >= mask_from:
                    keys = start + c * sub + lax.broadcasted_iota(jnp.int32, (sub, tw), 0)
                    qpos = qi * tq + r * tw + lax.broadcasted_iota(jnp.int32, (sub, tw), 1)
                    t = jnp.where(keys <= qpos, t, MASK_VALUE)
                tiles.append(t)
            col_max = tiles[0]
            for t in tiles[1:]:
                col_max = jnp.maximum(col_max, t)
            col_max = jnp.max(col_max, axis=0, keepdims=True)
            idx = i * n_strips + r
            m_prev = m_ref[idx]
            if fox:
                cq = cq_ref[0, 0, i:i + 1, r * tw:(r + 1) * tw]
                m_new = jnp.maximum(m_prev, col_max + cq)
                shift = m_new - cq
            else:
                m_new = jnp.maximum(m_prev, col_max)
                shift = m_new
            m_ref[idx] = m_new
            alpha = jnp.exp2(m_prev - m_new)
            psum = None
            probs = []
            for t in tiles:
                p = jnp.exp2(t - shift)
                psum = p if psum is None else psum + p
                probs.append(p.astype(BF16))
            l_ref[idx] = alpha * l_ref[idx] + jnp.sum(psum, axis=0, keepdims=True)
            vt = vt_ref[0, i * HEAD_DIM:(i + 1) * HEAD_DIM, pl.ds(start, n_sub * sub)]
            rows = slice(i * HEAD_DIM, (i + 1) * HEAD_DIM)
            cols = slice(r * tw, (r + 1) * tw)
            acc_ref[rows, cols] = (acc_ref[rows, cols] * alpha
                                   + _dot(vt, jnp.concatenate(probs, axis=0)))

    n_sub_full = tk // sub
    full_strips = [(r, n_sub_full, None) for r in range(n_strips)]
    per = tw // sub
    diag = []
    for d in range(tq // tk):
        strips = []
        for r in range(n_strips):
            n_sub = min(max((r + 1) * per - d * n_sub_full, 0), n_sub_full)
            if n_sub > 0:
                strips.append((r, n_sub, max(r * per - d * n_sub_full, 0)))
        diag.append((pl.multiple_of(qi * tq + d * tk, tk), strips))

    n_full = qi * (tq // tk)

    def full_group(j, carry):
        group([(pl.multiple_of((j * ATTN_GROUP + g) * tk, tk), full_strips)
               for g in range(ATTN_GROUP)])
        return carry

    lax.fori_loop(0, n_full // ATTN_GROUP, full_group, 0)
    for rem in range(ATTN_GROUP):
        @pl.when(n_full % ATTN_GROUP == rem)
        def _():
            base = n_full - rem
            group([(pl.multiple_of((base + g) * tk, tk), full_strips) for g in range(rem)] + diag)

    heads = []
    for i in range(2):
        rows = slice(i * HEAD_DIM, (i + 1) * HEAD_DIM)
        heads.append(jnp.concatenate(
            [acc_ref[rows, r * tw:(r + 1) * tw] * (1.0 / l_ref[i * n_strips + r])
             for r in range(n_strips)], axis=1))
    o_ref[0] = jnp.concatenate(heads, axis=0).T.astype(o_ref.dtype)


def _attention(q, k, vt, ct, B, S, tq, tk, tw, fox):
    n_pairs = FOX_HEADS // 2
    nq = S // tq
    qw = LANES if fox else 2 * LANES
    in_specs = [pl.BlockSpec((1, tq, qw), lambda b, h, i: (b, i, h)),
                pl.BlockSpec((1, S, 2 * LANES), lambda b, h, i: (b, 0, h)),
                pl.BlockSpec((1, LANES, S), lambda b, h, i: (b, h, 0))]
    args = [q, k, vt]
    if fox:
        in_specs.append(pl.BlockSpec((1, 1, 2, tq), lambda b, h, i: (b, h, 0, i)))
        args.append(ct)
        lane = jnp.arange(LANES)
        ones = jnp.stack([lane < 3, (lane >= 3) & (lane < 6)]).astype(BF16)
        in_specs.append(pl.BlockSpec((2, tq, LANES), lambda b, h, i: (0, 0, 0)))
        args.append(jnp.broadcast_to(ones[:, None, :], (2, tq, LANES)))
    n_strips = tq // tw
    return pl.pallas_call(
        functools.partial(_attn_kernel, fox=fox, tq=tq, tk=tk, tw=tw),
        grid=(B, n_pairs, nq),
        in_specs=in_specs,
        out_specs=pl.BlockSpec((1, tq, LANES), lambda b, h, i: (b, i, h)),
        out_shape=jax.ShapeDtypeStruct((B, S, n_pairs * LANES), BF16),
        scratch_shapes=[pltpu.VMEM((2 * n_strips, 1, tw), F32),
                        pltpu.VMEM((2 * n_strips, 1, tw), F32),
                        pltpu.VMEM((LANES, tq), F32)],
        compiler_params=_params("arbitrary", "arbitrary", "arbitrary"),
        name="fox_attn" if fox else "mla_attn",
    )(*args)


def _ffn_kernel(h_ref, a_ref, b_ref, wa_ref, wb_ref, g_ref, wg_ref, wu_ref, wd_ref,
                o_ref, hres_ref, xn_ref, acc_ref):
    f = pl.program_id(1)

    @pl.when(f == 0)
    def _():
        hres = h_ref[...] + _dot(a_ref[...], wa_ref[...]) + _dot(b_ref[...], wb_ref[...])
        hres_ref[...] = hres
        xn_ref[...] = _rms(hres, g_ref[...]).astype(BF16)
        acc_ref[...] = jnp.zeros_like(acc_ref)

    xn = xn_ref[...]
    gate = _dot(xn, wg_ref[...].astype(BF16))
    up = _dot(xn, wu_ref[...].astype(BF16))
    act = gate * (1.0 / (1.0 + jnp.exp(-gate))) * up
    acc_ref[...] += _dot(act.astype(BF16), wd_ref[...].astype(BF16))

    @pl.when(f == pl.num_programs(1) - 1)
    def _():
        o_ref[...] = hres_ref[...] + acc_ref[...]


def _ffn(h, a, b, wa, wb, g, w_gate, w_up, w_down, tm, tf):
    N = h.shape[0]
    row = lambda i, f: (i, 0)
    full = lambda x: pl.BlockSpec(x.shape, lambda i, f: (0, 0))
    half_w = a.shape[1]
    return pl.pallas_call(
        _ffn_kernel,
        grid=(N // tm, D_FF // tf),
        in_specs=[pl.BlockSpec((tm, D_MODEL), row), pl.BlockSpec((tm, half_w), row),
                  pl.BlockSpec((tm, half_w), row), full(wa), full(wb), full(g),
                  pl.BlockSpec((D_MODEL, tf), lambda i, f: (0, f)),
                  pl.BlockSpec((D_MODEL, tf), lambda i, f: (0, f)),
                  pl.BlockSpec((tf, D_MODEL), lambda i, f: (f, 0))],
        out_specs=pl.BlockSpec((tm, D_MODEL), row),
        out_shape=jax.ShapeDtypeStruct((N, D_MODEL), F32),
        scratch_shapes=[pltpu.VMEM((tm, D_MODEL), F32), pltpu.VMEM((tm, D_MODEL), BF16),
                        pltpu.VMEM((tm, D_MODEL), F32)],
        compiler_params=_params("arbitrary", "arbitrary"),
        name="dense_ffn",
    )(h, a, b, wa, wb, g, w_gate, w_up, w_down)


def _router_kernel(h_ref, a_ref, b_ref, wa_ref, wb_ref, g_ref, wr_ref, tri_ref,
                   hres_ref, xn_ref, slab_ref, meta_ref, cnt_ref):
    @pl.when(pl.program_id(0) == 0)
    def _():
        cnt_ref[...] = jnp.zeros_like(cnt_ref)

    hres = h_ref[...] + _dot(a_ref[...], wa_ref[...]) + _dot(b_ref[...], wb_ref[...])
    hres_ref[...] = hres
    xn = _rms(hres, g_ref[...])
    xn_ref[...] = xn
    xh = xn.astype(BF16)
    xl = (xn - xh.astype(F32)).astype(BF16)
    wr = wr_ref[...]
    wh = wr.astype(BF16)
    wl = (wr - wh.astype(F32)).astype(BF16)
    logits = _dot(xh, wh) + (_dot(xh, wl) + _dot(xl, wh))
    lane = lax.broadcasted_iota(jnp.int32, logits.shape, 1)
    logits = jnp.where(lane < N_EXPERTS, logits, MASK_VALUE)
    m1 = jnp.max(logits, axis=1, keepdims=True)
    i1 = jnp.min(jnp.where(logits == m1, lane, LANES), axis=1, keepdims=True)
    rest = jnp.where(lane == i1, MASK_VALUE, logits)
    m2 = jnp.max(rest, axis=1, keepdims=True)
    i2 = jnp.min(jnp.where(rest == m2, lane, LANES), axis=1, keepdims=True)
    e2 = jnp.exp(m2 - m1)
    w1 = 1.0 / (1.0 + e2)
    w2 = e2 / (1.0 + e2)
    hot1 = lane == i1
    hot2 = lane == i2
    onehot = jnp.where(hot1 | hot2, 1.0, 0.0)
    before = _dot(tri_ref[...], onehot.astype(BF16)) + cnt_ref[...]
    r1 = jnp.sum(jnp.where(hot1, before, 0.0), axis=1, keepdims=True)
    r2 = jnp.sum(jnp.where(hot2, before, 0.0), axis=1, keepdims=True)
    cnt_ref[...] += jnp.sum(onehot, axis=0, keepdims=True)
    slab = (jnp.where(lane == 0, i1.astype(F32), 0.0) + jnp.where(lane == 1, i2.astype(F32), 0.0)
            + jnp.where(lane == 2, r1, 0.0) + jnp.where(lane == 3, r2, 0.0)
            + jnp.where(lane == 4, w1, 0.0) + jnp.where(lane == 5, w2, 0.0))
    slab_ref[...] = slab
    meta_ref[...] = slab.T[:8, :]


def _router(h, a, b, wa, wb, g, w_router, tm):
    N = h.shape[0]
    tri = (jnp.arange(tm)[:, None] > jnp.arange(tm)[None, :]).astype(BF16)
    row = lambda i: (i, 0)
    full = lambda x: pl.BlockSpec(x.shape, lambda i: (0, 0))
    half_w = a.shape[1]
    return pl.pallas_call(
        _router_kernel,
        grid=(N // tm,),
        in_specs=[pl.BlockSpec((tm, D_MODEL), row), pl.BlockSpec((tm, half_w), row),
                  pl.BlockSpec((tm, half_w), row), full(wa), full(wb), full(g), full(w_router),
                  full(tri)],
        out_specs=[pl.BlockSpec((tm, D_MODEL), row), pl.BlockSpec((tm, D_MODEL), row),
                   pl.BlockSpec((tm, LANES), row), pl.BlockSpec((8, tm), lambda i: (0, i)),
                   pl.BlockSpec((1, LANES), lambda i: (0, 0))],
        out_shape=[jax.ShapeDtypeStruct((N, D_MODEL), F32), jax.ShapeDtypeStruct((N, D_MODEL), F32),
                   jax.ShapeDtypeStruct((N, LANES), F32), jax.ShapeDtypeStruct((8, N), F32),
                   jax.ShapeDtypeStruct((1, LANES), F32)],
        compiler_params=_params("arbitrary"),
        name="moe_router",
    )(h, a, b, wa, wb, g, w_router, tri)


def _row_copy(src, src_row, dst, dst_row, sem):
    return pltpu.make_async_copy(src.at[pl.ds(src_row, 1), :], dst.at[pl.ds(dst_row, 1), :], sem)


def _dispatch_kernel(ends_ref, cap_ref, pos_hbm, xn_ref, xs_hbm, pos_smem, zero_ref, idx_sem,
                     row_sem, zero_sem, *, tm, tr):
    i = pl.program_id(0)

    @pl.when(i == 0)
    def _():
        zero_ref[...] = jnp.zeros_like(zero_ref)

        def clear(start):
            copy = pltpu.make_async_copy(zero_ref, xs_hbm.at[pl.ds(start, tr), :], zero_sem)
            copy.start()
            copy.wait()

        for e in range(N_EXPERTS):
            @pl.when(cap_ref[e] > 0)
            def _():
                clear(pl.multiple_of(ends_ref[e] - tr, 8))
        n_rows = xs_hbm.shape[0]
        for t in range(N_EXPERTS):
            start = n_rows - (t + 1) * tr

            @pl.when(start >= ends_ref[N_EXPERTS - 1])
            def _():
                clear(start)

    idx_copy = pltpu.make_async_copy(pos_hbm.at[i], pos_smem, idx_sem)
    idx_copy.start()
    idx_copy.wait()

    def issue(j, carry):
        base = pl.multiple_of(j * 8, 8)
        for u in range(8):
            for k in range(2):
                _row_copy(xn_ref, base + u, xs_hbm, pos_smem[k * tm + base + u], row_sem).start()
        return carry

    lax.fori_loop(0, tm // 8, issue, 0)
    for k in range(2):
        pltpu.make_async_copy(xn_ref, xs_hbm.at[pl.ds(0, tm), :], row_sem).wait()


def _dispatch(ends, cap, pos_tiles, xn, n_rows, tm, tr):
    N = xn.shape[0]
    grid_spec = pltpu.PrefetchScalarGridSpec(
        num_scalar_prefetch=2,
        grid=(N // tm,),
        in_specs=[pl.BlockSpec(memory_space=pl.ANY),
                  pl.BlockSpec((tm, D_MODEL), lambda i, ends, cap: (i, 0))],
        out_specs=pl.BlockSpec(memory_space=pl.ANY),
        scratch_shapes=[pltpu.SMEM((2 * tm,), jnp.int32), pltpu.VMEM((tr, D_MODEL), F32),
                        pltpu.SemaphoreType.DMA, pltpu.SemaphoreType.DMA,
                        pltpu.SemaphoreType.DMA])
    return pl.pallas_call(
        functools.partial(_dispatch_kernel, tm=tm, tr=tr),
        grid_spec=grid_spec,
        out_shape=jax.ShapeDtypeStruct((n_rows, D_MODEL), F32),
        compiler_params=_params("arbitrary"),
        name="moe_dispatch",
    )(ends, cap, pos_tiles, xn)


def _expert_kernel(te_ref, tv_ref, x_ref, wg_ref, wu_ref, wd_ref, y_ref, xb_ref, acc_ref):
    del te_ref
    i = pl.program_id(0)
    f = pl.program_id(1)
    live = tv_ref[i] > 0

    @pl.when(live & (f == 0))
    def _():
        xb_ref[...] = x_ref[...].astype(BF16)
        acc_ref[...] = jnp.zeros_like(acc_ref)

    @pl.when(live)
    def _():
        xb = xb_ref[...]
        gate = _dot(xb, wg_ref[0].astype(BF16))
        up = _dot(xb, wu_ref[0].astype(BF16))
        act = gate * (1.0 / (1.0 + jnp.exp(-gate))) * up
        acc_ref[...] += _dot(act.astype(BF16), wd_ref[0].astype(BF16))

    @pl.when(f == pl.num_programs(1) - 1)
    def _():
        y_ref[...] = jnp.where(live, acc_ref[...], 0.0)


def _experts(tile_expert, tile_valid, xs, w_gate, w_up, w_down, tr, tf):
    n_rows = xs.shape[0]
    nf = D_FF // tf
    fsel = lambda f, i, tv: jnp.where(tv[i] > 0, f, 0)
    grid_spec = pltpu.PrefetchScalarGridSpec(
        num_scalar_prefetch=2,
        grid=(n_rows // tr, nf),
        in_specs=[pl.BlockSpec((tr, D_MODEL), lambda i, f, te, tv: (jnp.where(tv[i] > 0, i, 0), 0)),
                  pl.BlockSpec((1, D_MODEL, tf), lambda i, f, te, tv: (te[i], 0, fsel(f, i, tv))),
                  pl.BlockSpec((1, D_MODEL, tf), lambda i, f, te, tv: (te[i], 0, fsel(f, i, tv))),
                  pl.BlockSpec((1, tf, D_MODEL), lambda i, f, te, tv: (te[i], fsel(f, i, tv), 0))],
        out_specs=pl.BlockSpec((tr, D_MODEL), lambda i, f, te, tv: (i, 0)),
        scratch_shapes=[pltpu.VMEM((tr, D_MODEL), BF16), pltpu.VMEM((tr, D_MODEL), F32)])
    return pl.pallas_call(
        _expert_kernel,
        grid_spec=grid_spec,
        out_shape=jax.ShapeDtypeStruct((n_rows, D_MODEL), F32),
        compiler_params=_params("arbitrary", "arbitrary"),
        name="moe_experts",
    )(tile_expert, tile_valid, xs, w_gate, w_up, w_down)


def _combine_kernel(pos_hbm, hres_ref, slab_ref, fn_ref, ys_hbm, o_ref, pos_smem, ybuf_ref,
                    idx_sem, row_sem, *, tm):
    i = pl.program_id(0)
    idx_copy = pltpu.make_async_copy(pos_hbm.at[i], pos_smem, idx_sem)
    idx_copy.start()
    idx_copy.wait()

    def issue(j, carry):
        base = pl.multiple_of(j * 8, 8)
        for u in range(8):
            for k in range(2):
                _row_copy(ys_hbm, pos_smem[k * tm + base + u], ybuf_ref.at[k], base + u, row_sem).start()
        return carry

    lax.fori_loop(0, tm // 8, issue, 0)
    for k in range(2):
        pltpu.make_async_copy(ys_hbm.at[pl.ds(0, tm), :], ybuf_ref.at[k], row_sem).wait()

    slab = slab_ref[...]
    lane = lax.broadcasted_iota(jnp.int32, slab.shape, 1)
    w1 = jnp.sum(jnp.where(lane == 4, slab, 0.0), axis=1, keepdims=True)
    w2 = jnp.sum(jnp.where(lane == 5, slab, 0.0), axis=1, keepdims=True)
    out = hres_ref[...] + w1 * ybuf_ref[0] + w2 * ybuf_ref[1]
    o_ref[...] = _rms(out, fn_ref[...])


def _combine(pos_tiles, hres, slab, final_g, ys, tm):
    N = hres.shape[0]
    row = lambda i: (i, 0)
    return pl.pallas_call(
        functools.partial(_combine_kernel, tm=tm),
        grid=(N // tm,),
        in_specs=[pl.BlockSpec(memory_space=pl.ANY), pl.BlockSpec((tm, D_MODEL), row),
                  pl.BlockSpec((tm, LANES), row), pl.BlockSpec(final_g.shape, lambda i: (0, 0)),
                  pl.BlockSpec(memory_space=pl.ANY)],
        out_specs=pl.BlockSpec((tm, D_MODEL), row),
        out_shape=jax.ShapeDtypeStruct((N, D_MODEL), F32),
        scratch_shapes=[pltpu.SMEM((2 * tm,), jnp.int32), pltpu.VMEM((2, tm, D_MODEL), F32),
                        pltpu.SemaphoreType.DMA, pltpu.SemaphoreType.DMA],
        compiler_params=_params("arbitrary"),
        name="moe_combine",
    )(pos_tiles, hres, slab, final_g, ys)


def _moe(h, a, b, wa, wb, g, w_router, w_gate, w_up, w_down, final_g):
    N = h.shape[0]
    tm, tr = MOE_TM, MOE_TR
    hres, xn, slab, meta, cnt = _router(h, a, b, wa, wb, g, w_router, tm)
    e1, e2, r1, r2 = (meta[j].astype(jnp.int32) for j in range(4))
    counts = cnt[0, :N_EXPERTS].astype(jnp.int32)
    cap = (counts + tr - 1) // tr * tr
    ends = jnp.cumsum(cap)
    offs = ends - cap
    pos = jnp.stack([offs[e1] + r1, offs[e2] + r2])
    pos_tiles = pos.reshape(2, N // tm, tm).transpose(1, 0, 2).reshape(N // tm, 2 * tm)
    n_rows = 2 * N + N_EXPERTS * tr
    tile_start = jnp.arange(n_rows // tr, dtype=jnp.int32) * tr
    tile_expert = jnp.minimum(jnp.sum(tile_start[:, None] >= ends[None, :], axis=1),
                              N_EXPERTS - 1).astype(jnp.int32)
    tile_valid = jnp.clip(counts[tile_expert] - (tile_start - offs[tile_expert]), 0, tr)
    tile_valid = jnp.where(tile_start < ends[N_EXPERTS - 1], tile_valid, 0).astype(jnp.int32)
    xs = _dispatch(ends.astype(jnp.int32), cap.astype(jnp.int32), pos_tiles, xn, n_rows, tm, tr)
    ys = _experts(tile_expert, tile_valid, xs, w_gate, w_up, w_down, tr, MOE_TF)
    return _combine(pos_tiles, hres, slab, final_g, ys, tm)


def _l1_pre_kernel(x_ref, pos_ref, g_ref, w_ref, cw_ref, invf_ref, mlo_ref, mhi_ref,
                   q_ref, k_ref, vt_ref, oc_ref, tail_ref):
    @pl.when(pl.program_id(1) == 0)
    def _():
        tail_ref[...] = jnp.zeros_like(tail_ref)

    xn = _rms(x_ref[...], g_ref[...]).astype(BF16)
    y = _dot(xn, w_ref[...])
    ang = pos_ref[...].astype(F32) * invf_ref[...]
    cos = jnp.cos(ang)
    sin = jnp.sin(ang)
    sin_lo = sin * mlo_ref[...]
    sin_hi = sin * mhi_ref[...]
    half = HEAD_DIM // 2
    qw = SWA_HEADS * HEAD_DIM
    for j in range(qw // LANES):
        sl = slice(j * LANES, (j + 1) * LANES)
        q_ref[:, sl] = _rope(y[:, sl], cos, sin_lo, sin_hi, half).astype(BF16)
    k_ref[...] = _rope(y[:, qw:qw + LANES], cos, sin_lo, sin_hi, half).astype(BF16)
    vt_ref[0] = y[:, qw + LANES:qw + 2 * LANES].T.astype(BF16)

    o = qw + 2 * LANES
    C = CONV_CHANNELS
    gate_b = y[:, o:o + C]
    u = y[:, o + C:o + 2 * C] * y[:, o + 2 * C:o + 3 * C]
    tm = u.shape[0]
    rows = lax.broadcasted_iota(jnp.int32, u.shape, 0)
    tail = tail_ref[...]
    u1 = jnp.where(rows == 0, tail[7:8, :], pltpu.roll(u, 1, 0))
    u2 = jnp.where(rows == 0, tail[6:7, :], jnp.where(rows == 1, tail[7:8, :], pltpu.roll(u, 2, 0)))
    cw = cw_ref[...]
    oc_ref[...] = (gate_b * (cw[0:1, :] * u2 + cw[1:2, :] * u1 + cw[2:3, :] * u)).astype(BF16)
    tail_ref[...] = u[tm - 8:tm, :]


def _l1_pre(x2, pos2, g, w_in, conv_w, B, S, tm):
    N = B * S
    ns = S // tm
    lane = jnp.arange(LANES)
    half = HEAD_DIM // 2
    freq = ROPE_THETA ** (-jnp.arange(0, HEAD_DIM, 2, dtype=F32) / HEAD_DIM)
    invf = freq[lane % half][None, :].astype(F32)
    m_lo = jnp.where(lane % HEAD_DIM < half, -1.0, 0.0)[None, :].astype(F32)
    m_hi = jnp.where(lane % HEAD_DIM >= half, 1.0, 0.0)[None, :].astype(F32)
    row = lambda b, s: (b * ns + s, 0)
    const = lambda b, s: (0, 0)
    full = lambda a: pl.BlockSpec(a.shape, const)
    return pl.pallas_call(
        _l1_pre_kernel,
        grid=(B, ns),
        in_specs=[pl.BlockSpec((tm, D_MODEL), row), pl.BlockSpec((tm, 1), row),
                  full(g), full(w_in), full(conv_w), full(invf), full(m_lo), full(m_hi)],
        out_specs=[pl.BlockSpec((tm, SWA_HEADS * HEAD_DIM), row),
                   pl.BlockSpec((tm, LANES), row),
                   pl.BlockSpec((1, LANES, tm), lambda b, s: (b, 0, s)),
                   pl.BlockSpec((tm, CONV_CHANNELS), row)],
        out_shape=[jax.ShapeDtypeStruct((N, SWA_HEADS * HEAD_DIM), BF16),
                   jax.ShapeDtypeStruct((N, LANES), BF16),
                   jax.ShapeDtypeStruct((B, LANES, S), BF16),
                   jax.ShapeDtypeStruct((N, CONV_CHANNELS), BF16)],
        scratch_shapes=[pltpu.VMEM((8, CONV_CHANNELS), F32)],
        compiler_params=_params("arbitrary", "arbitrary"),
        name="l1_pre",
    )(x2, pos2, g, w_in, conv_w, invf, m_lo, m_hi)


def _swa_kernel(sink_ref, q_ref, k_ref, kp_ref, vt_ref, vtp_ref, o_ref, *, tq):
    s_idx = pl.program_id(1)
    first = lax.broadcasted_iota(jnp.int32, (1, LANES), 1) < HEAD_DIM
    G = SWA_HEADS // SWA_KV_HEADS
    W = WINDOW

    k_all = jnp.concatenate([kp_ref[0], k_ref[0]], axis=0)
    k_roll = pltpu.roll(k_all, HEAD_DIM, 1)
    k_dup = (jnp.where(first, k_all, k_roll), jnp.where(first, k_roll, k_all))
    vt_all = jnp.concatenate([vtp_ref[0], vt_ref[0]], axis=1)
    q = q_ref[0]
    zero = jnp.zeros((W, LANES), q.dtype)
    cols = G * W
    key_i = lax.broadcasted_iota(jnp.int32, (2 * W, cols), 0)
    col_i = lax.broadcasted_iota(jnp.int32, (2 * W, cols), 1)
    rel = col_i % W + W - key_i
    in_window = (rel >= 0) & (rel < W)
    lane = lax.broadcasted_iota(jnp.int32, (1, cols), 1)
    for r in range(tq // W):
        keep = in_window & (s_idx * tq - W + key_i >= 0) if r == 0 else in_window
        outs = []
        for g in range(SWA_KV_HEADS):
            kb = k_dup[g][r * W:r * W + 2 * W]
            vtb = vt_all[g * HEAD_DIM:(g + 1) * HEAD_DIM, r * W:r * W + 2 * W]
            parts = []
            sink = jnp.zeros((1, cols), F32)
            for j in range(G // 2):
                blk = g * (G // 2) + j
                q2 = q[r * W:(r + 1) * W, blk * LANES:(blk + 1) * LANES]
                parts += [jnp.where(first, q2, zero), jnp.where(first, zero, q2)]
                for half in range(2):
                    c0 = (2 * j + half) * W
                    sink = jnp.where((lane >= c0) & (lane < c0 + W), sink_ref[2 * blk + half], sink)
            qst = jnp.concatenate(parts, axis=0)
            s = _dot_nt(kb, qst)
            s = jnp.where(keep, s, MASK_VALUE)
            m = jnp.maximum(jnp.max(s, axis=0, keepdims=True), sink)
            e = jnp.exp(s - m)
            den = jnp.sum(e, axis=0, keepdims=True) + jnp.exp(sink - m)
            ot = _dot(vtb, (e * (1.0 / den)).astype(BF16))
            outs += [ot[:, j * W:(j + 1) * W] for j in range(G)]
        o_ref[0, r * W:(r + 1) * W, :] = jnp.concatenate(outs, axis=0).T.astype(o_ref.dtype)


def _swa(sinks, q, k, vt, B, S, tq):
    ns = S // tq
    per = tq // WINDOW
    prev = lambda s: jnp.maximum(s * per - 1, 0)
    return pl.pallas_call(
        functools.partial(_swa_kernel, tq=tq),
        grid=(B, ns),
        in_specs=[pl.BlockSpec(memory_space=pltpu.SMEM),
                  pl.BlockSpec((1, tq, SWA_HEADS * HEAD_DIM), lambda b, s: (b, s, 0)),
                  pl.BlockSpec((1, tq, LANES), lambda b, s: (b, s, 0)),
                  pl.BlockSpec((1, WINDOW, LANES), lambda b, s: (b, prev(s), 0)),
                  pl.BlockSpec((1, LANES, tq), lambda b, s: (b, 0, s)),
                  pl.BlockSpec((1, LANES, WINDOW), lambda b, s: (b, 0, prev(s)))],
        out_specs=pl.BlockSpec((1, tq, SWA_HEADS * HEAD_DIM), lambda b, s: (b, s, 0)),
        out_shape=jax.ShapeDtypeStruct((B, S, SWA_HEADS * HEAD_DIM), BF16),
        compiler_params=_params("arbitrary", "arbitrary"),
        name="swa_attn",
    )(sinks, q, k, k, vt, vt)


def _pad_heads(w, n_heads, width, offset=0):
    K = w.shape[0]
    w3 = w.reshape(K, n_heads, width)
    out = jnp.zeros((K, n_heads, LANES), w.dtype).at[:, :, offset:offset + width].set(w3)
    return out.reshape(K, n_heads * LANES)


def kernel(x, positions, l0_norm_mix, l0_w_in, l0_b_forget, l0_q_norm, l0_w_q_up, l0_kv_norm,
           l0_w_kv_up, l0_w_out, l0_norm_ffn, l0_w_gate, l0_w_up, l0_w_down,
           l1_norm_mix, l1_w_in, l1_sinks, l1_conv_w, l1_w_out, l1_norm_ffn, l1_w_router,
           l1_w_gate, l1_w_up, l1_w_down, final_norm):
    B, S, D = x.shape
    assert D == D_MODEL and S % 512 == 0
    N = B * S
    tm = 512
    x2 = x.reshape(N, D)
    pos2 = positions.reshape(N, 1)
    rowv = lambda v: v.reshape(1, -1).astype(F32)

    fox_w = FOX_HEADS * HEAD_DIM
    o = 3 * fox_w
    w_fq = l0_w_in[:, :fox_w] * (HEAD_DIM ** -0.5 * LOG2E)
    w_f = l0_w_in[:, o:o + FOX_HEADS]
    o += FOX_HEADS
    w_cq = l0_w_in[:, o:o + MLA_Q_RANK]
    o += MLA_Q_RANK
    w_ckv = l0_w_in[:, o:o + MLA_KV_RANK]
    o += MLA_KV_RANK
    w_kr = l0_w_in[:, o:o + MLA_ROPE_DIM]
    w_ext = jnp.concatenate(
        [w_fq, l0_w_in[:, fox_w:3 * fox_w], w_cq, w_ckv,
         _pad_heads(w_kr, 1, MLA_ROPE_DIM, MLA_NOPE_DIM), _pad_heads(w_f, 1, FOX_HEADS)],
        axis=1).astype(BF16)
    bf_row = jnp.zeros((1, LANES), F32).at[0, :FOX_HEADS].set(l0_b_forget.astype(F32))
    qk_dim = MLA_NOPE_DIM + MLA_ROPE_DIM
    wq = _pad_heads(l0_w_q_up * (qk_dim ** -0.5 * LOG2E), MLA_HEADS, qk_dim).astype(BF16)
    kv3 = l0_w_kv_up.reshape(MLA_KV_RANK, MLA_HEADS, MLA_NOPE_DIM + MLA_V_DIM)
    wk = _pad_heads(kv3[:, :, :MLA_NOPE_DIM].reshape(MLA_KV_RANK, -1), MLA_HEADS,
                    MLA_NOPE_DIM).astype(BF16)
    wv = kv3[:, :, MLA_NOPE_DIM:].reshape(MLA_KV_RANK, -1).astype(BF16)

    q_f, k_f, vt_f, q_m, k_m, vt_m, ct = _l0_pre(
        x2, pos2, rowv(l0_norm_mix), w_ext, bf_row, rowv(l0_q_norm), wq, rowv(l0_kv_norm),
        wk, wv, B, S, tm)

    tq, tk, tw = ATTN_TQ, ATTN_TK, ATTN_TW
    o_fox = _attention(q_f.reshape(B, S, fox_w), k_f.reshape(B, S, 2 * fox_w), vt_f,
                       ct.reshape(B, FOX_HEADS // 2, 2, S), B, S, tq, tk, tw, True)
    o_mla = _attention(q_m.reshape(B, S, MLA_PAD), k_m.reshape(B, S, MLA_PAD), vt_m, None,
                       B, S, tq, tk, tw, False)

    w_out0 = l0_w_out.astype(BF16)
    h1 = _ffn(x2, o_fox.reshape(N, -1), o_mla.reshape(N, -1), w_out0[:fox_w], w_out0[fox_w:],
              rowv(l0_norm_ffn), l0_w_gate, l0_w_up, l0_w_down, FFN_TM, FFN_TF)

    swa_w = SWA_HEADS * HEAD_DIM
    w_in1 = jnp.concatenate([l1_w_in[:, :swa_w] * (HEAD_DIM ** -0.5), l1_w_in[:, swa_w:]],
                            axis=1).astype(BF16)
    q_s, k_s, vt_s, o_conv = _l1_pre(h1, pos2, rowv(l1_norm_mix), w_in1, l1_conv_w.astype(F32),
                                     B, S, tm)
    o_swa = _swa(l1_sinks.astype(F32), q_s.reshape(B, S, swa_w), k_s.reshape(B, S, LANES), vt_s,
                 B, S, 512)
    w_out1 = l1_w_out.astype(BF16)
    w_router = jnp.zeros((D_MODEL, LANES), F32).at[:, :N_EXPERTS].set(l1_w_router.astype(F32))
    out = _moe(h1, o_swa.reshape(N, -1), o_conv, w_out1[:swa_w], w_out1[swa_w:],
               rowv(l1_norm_ffn), w_router, l1_w_gate, l1_w_up, l1_w_down, rowv(final_norm))
    return out.reshape(B, S, D)
```

```python
import functools

import jax
import jax.numpy as jnp
from jax import lax
from jax.experimental import pallas as pl
from jax.experimental.pallas import tpu as pltpu

F32 = jnp.float32
BF16 = jnp.bfloat16

D_MODEL = 1024
HEAD_DIM = 64
RMS_EPS = 1e-6
ROPE_THETA = 10000.0
FOX_HEADS = 8
MLA_HEADS = 8
MLA_Q_RANK = 384
MLA_KV_RANK = 256
MLA_NOPE_DIM = 64
MLA_ROPE_DIM = 32
MLA_V_DIM = 64
SWA_HEADS = 8
SWA_KV_HEADS = 2
WINDOW = 128
CONV_CHANNELS = 512
CONV_WIDTH = 3
D_FF = 3584
N_EXPERTS = 8

LANES = 128
MASK_VALUE = -1e30
LOG2E = 1.4426950408889634
VMEM_LIMIT = 56 * 1024 * 1024

QKV_W = 3 * FOX_HEADS * HEAD_DIM
MLA_PAD = MLA_HEADS * LANES
L0_COLS = QKV_W + MLA_Q_RANK + MLA_KV_RANK + 2 * LANES
ATTN_TQ = 1024
ATTN_TK = 512
ATTN_TW = 256
ATTN_GROUP = 4
MOE_TM = 512
MOE_TR = 1024
MOE_TF = 512
FFN_TM = 1024
FFN_TF = 512
L1_COLS = SWA_HEADS * HEAD_DIM + 2 * SWA_KV_HEADS * HEAD_DIM + 3 * CONV_CHANNELS


def _params(*sem):
    return pltpu.CompilerParams(dimension_semantics=sem, vmem_limit_bytes=VMEM_LIMIT)


def _rms(x, g):
    return x * lax.rsqrt(jnp.mean(x * x, axis=-1, keepdims=True) + RMS_EPS) * g


def _dot(a, b):
    return jnp.dot(a, b, preferred_element_type=F32)


def _dot_nt(a, b):
    return lax.dot_general(a, b, (((1,), (1,)), ((), ())), preferred_element_type=F32)


def _rope(x, cos, sin_lo, sin_hi, half):
    return (x * cos + pltpu.roll(x, half, 1) * sin_hi
            + pltpu.roll(x, LANES - half, 1) * sin_lo)


def _l0_pre_kernel(x_ref, pos_ref, g_ref, w_ref, bf_ref, qn_ref, wq_ref, kvn_ref, wk_ref,
                   wv_ref, tri_ref, place_ref, invf_ref, mlo_ref, mhi_ref,
                   qf_ref, kf_ref, vtf_ref, q_ref, k_ref, vt_ref, ct_ref, carry_ref):
    @pl.when(pl.program_id(1) == 0)
    def _():
        carry_ref[...] = jnp.zeros_like(carry_ref)

    fox_w = FOX_HEADS * HEAD_DIM
    xn = _rms(x_ref[...], g_ref[...]).astype(BF16)
    y = _dot(xn, w_ref[...])
    qf_ref[...] = y[:, :fox_w].astype(BF16)
    vtf_ref[0] = y[:, 2 * fox_w:3 * fox_w].T.astype(BF16)
    o = QKV_W
    cq = y[:, o:o + MLA_Q_RANK]
    o += MLA_Q_RANK
    ckv = y[:, o:o + MLA_KV_RANK]
    o += MLA_KV_RANK
    kr = y[:, o:o + LANES]
    fl = y[:, o + LANES:o + 2 * LANES]

    q = _dot(_rms(cq, qn_ref[...]).astype(BF16), wq_ref[...])
    ckn = _rms(ckv, kvn_ref[...]).astype(BF16)
    kk = _dot(ckn, wk_ref[...])
    vt_ref[0] = _dot(ckn, wv_ref[...]).T.astype(BF16)

    ang = pos_ref[...].astype(F32) * invf_ref[...]
    cos = jnp.cos(ang)
    sin = jnp.sin(ang)
    sin_lo = sin * mlo_ref[...]
    sin_hi = sin * mhi_ref[...]
    half = MLA_ROPE_DIM // 2
    kpe = _rope(kr, cos, sin_lo, sin_hi, half)
    for h in range(MLA_HEADS):
        sl = slice(h * LANES, (h + 1) * LANES)
        q_ref[:, sl] = _rope(q[:, sl], cos, sin_lo, sin_hi, half).astype(BF16)
        k_ref[:, sl] = (kk[:, sl] + kpe).astype(BF16)

    z = fl + bf_ref[...]
    ls = jnp.minimum(z, 0.0) - jnp.log1p(jnp.exp(-jnp.abs(z)))
    tri = tri_ref[...]
    c = carry_ref[...]
    for part in _split3(ls):
        c = c + _dot(tri, part)
    tm = c.shape[0]
    carry_ref[...] = c[tm - 1:tm, :]
    c2 = c * LOG2E
    ct_ref[0] = c2.T[:FOX_HEADS, :]
    bias = None
    for i, part in enumerate(_split3(c2)):
        d = _dot(part, place_ref[i])
        bias = d if bias is None else bias + d
    for hp in range(FOX_HEADS // 2):
        kf_ref[:, 2 * hp * LANES:(2 * hp + 1) * LANES] = (
            y[:, fox_w + hp * LANES:fox_w + (hp + 1) * LANES].astype(BF16))
        kf_ref[:, (2 * hp + 1) * LANES:(2 * hp + 2) * LANES] = (
            bias[:, hp * LANES:(hp + 1) * LANES].astype(BF16))


def _split3(x):
    hi = x.astype(BF16)
    r1 = x - hi.astype(F32)
    mid = r1.astype(BF16)
    lo = (r1 - mid.astype(F32)).astype(BF16)
    return hi, mid, lo


def _l0_pre(x2, pos2, g, w_ext, bf_row, qn, wq, kvn, wk, wv, B, S, tm):
    N = B * S
    ns = S // tm
    tri = (jnp.arange(tm)[:, None] >= jnp.arange(tm)[None, :]).astype(BF16)
    n_pairs = FOX_HEADS // 2
    place = jnp.zeros((3, LANES, n_pairs * LANES), F32)
    for hp in range(n_pairs):
        for i in range(3):
            place = place.at[i, 2 * hp, hp * LANES + i].set(-1.0)
            place = place.at[i, 2 * hp + 1, hp * LANES + 3 + i].set(-1.0)
    place = place.astype(BF16)
    lane = jnp.arange(LANES)
    half = MLA_ROPE_DIM // 2
    in_pe = (lane >= MLA_NOPE_DIM) & (lane < MLA_NOPE_DIM + MLA_ROPE_DIM)
    freq = ROPE_THETA ** (-jnp.arange(0, MLA_ROPE_DIM, 2, dtype=F32) / MLA_ROPE_DIM)
    invf = jnp.where(in_pe, freq[(lane - MLA_NOPE_DIM) % half], 0.0)[None, :].astype(F32)
    m_lo = jnp.where(in_pe & (lane < MLA_NOPE_DIM + half), -1.0, 0.0)[None, :].astype(F32)
    m_hi = jnp.where(in_pe & (lane >= MLA_NOPE_DIM + half), 1.0, 0.0)[None, :].astype(F32)

    row = lambda b, s: (b * ns + s, 0)
    tcol = lambda b, s: (b, 0, s)
    full = lambda a: pl.BlockSpec(a.shape, lambda b, s: (0,) * a.ndim)
    fox_w = FOX_HEADS * HEAD_DIM
    vw = MLA_HEADS * MLA_V_DIM
    return pl.pallas_call(
        _l0_pre_kernel,
        grid=(B, ns),
        in_specs=[pl.BlockSpec((tm, D_MODEL), row), pl.BlockSpec((tm, 1), row),
                  full(g), full(w_ext), full(bf_row), full(qn), full(wq), full(kvn), full(wk),
                  full(wv), full(tri), full(place), full(invf), full(m_lo), full(m_hi)],
        out_specs=[pl.BlockSpec((tm, fox_w), row), pl.BlockSpec((tm, 2 * fox_w), row),
                   pl.BlockSpec((1, fox_w, tm), tcol),
                   pl.BlockSpec((tm, MLA_PAD), row), pl.BlockSpec((tm, MLA_PAD), row),
                   pl.BlockSpec((1, vw, tm), tcol),
                   pl.BlockSpec((1, FOX_HEADS, tm), tcol)],
        out_shape=[jax.ShapeDtypeStruct((N, fox_w), BF16),
                   jax.ShapeDtypeStruct((N, 2 * fox_w), BF16),
                   jax.ShapeDtypeStruct((B, fox_w, S), BF16),
                   jax.ShapeDtypeStruct((N, MLA_PAD), BF16),
                   jax.ShapeDtypeStruct((N, MLA_PAD), BF16),
                   jax.ShapeDtypeStruct((B, vw, S), BF16),
                   jax.ShapeDtypeStruct((B, FOX_HEADS, S), F32)],
        scratch_shapes=[pltpu.VMEM((1, LANES), F32)],
        compiler_params=_params("arbitrary", "arbitrary"),
        name="l0_pre",
    )(x2, pos2, g, w_ext, bf_row, qn, wq, kvn, wk, wv, tri, place, invf, m_lo, m_hi)


def _attn_kernel(*refs, fox, tq, tk, tw):
    if fox:
        q_ref, k_ref, vt_ref, cq_ref, ones_ref, o_ref, m_ref, l_ref, acc_ref = refs
    else:
        q_ref, k_ref, vt_ref, o_ref, m_ref, l_ref, acc_ref = refs
    qi = pl.program_id(2)
    n_strips = tq // tw

    q = q_ref[0]
    if fox:
        lane = lax.broadcasted_iota(jnp.int32, (1, LANES), 1)
        zero = jnp.zeros_like(q)
        qa = (jnp.concatenate([jnp.where(lane < HEAD_DIM, q, zero), ones_ref[0]], axis=1),
              jnp.concatenate([jnp.where(lane < HEAD_DIM, zero, q), ones_ref[1]], axis=1))
    else:
        qa = (q[:, :LANES], q[:, LANES:])

    m_ref[...] = jnp.full_like(m_ref, MASK_VALUE)
    l_ref[...] = jnp.zeros_like(l_ref)
    acc_ref[...] = jnp.zeros_like(acc_ref)

    sub = LANES

    def group(chunks):
        work = []
        for start, strips in chunks:
            kc = k_ref[0, pl.ds(start, tk), :]
            for i in range(2):
                for r, n_sub, mask_from in strips:
                    kh = kc[:n_sub * sub] if fox else kc[:n_sub * sub, i * LANES:(i + 1) * LANES]
                    s = _dot_nt(kh, qa[i][r * tw:(r + 1) * tw])
                    work.append((start, i, r, n_sub, mask_from, s))
        for start, i, r, n_sub, mask_from, s in work:
            tiles = []
            for c in range(n_sub):
                t = s[c * sub:(c + 1) * sub]
                if mask_from is not None and c >= mask_from:
                    keys = start + c * sub + lax.broadcasted_iota(jnp.int32, (sub, tw), 0)
                    qpos = qi * tq + r * tw + lax.broadcasted_iota(jnp.int32, (sub, tw), 1)
                    t = jnp.where(keys <= qpos, t, MASK_VALUE)
                tiles.append(t)
            col_max = tiles[0]
            for t in tiles[1:]:
                col_max = jnp.maximum(col_max, t)
            col_max = jnp.max(col_max, axis=0, keepdims=True)
            idx = i * n_strips + r
            m_prev = m_ref[idx]
            if fox:
                cq = cq_ref[0, 0, i:i + 1, r * tw:(r + 1) * tw]
                m_new = jnp.maximum(m_prev, col_max + cq)
                shift = m_new - cq
            else:
                m_new = jnp.maximum(m_prev, col_max)
                shift = m_new
            m_ref[idx] = m_new
            alpha = jnp.exp2(m_prev - m_new)
            psum = None
            probs = []
            for t in tiles:
                p = jnp.exp2(t - shift)
                psum = p if psum is None else psum + p
                probs.append(p.astype(BF16))
            l_ref[idx] = alpha * l_ref[idx] + jnp.sum(psum, axis=0, keepdims=True)
            vt = vt_ref[0, i * HEAD_DIM:(i + 1) * HEAD_DIM, pl.ds(start, n_sub * sub)]
            rows = slice(i * HEAD_DIM, (i + 1) * HEAD_DIM)
            cols = slice(r * tw, (r + 1) * tw)
            acc_ref[rows, cols] = (acc_ref[rows, cols] * alpha
                                   + _dot(vt, jnp.concatenate(probs, axis=0)))

    n_sub_full = tk // sub
    full_strips = [(r, n_sub_full, None) for r in range(n_strips)]
    per = tw // sub
    diag = []
    for d in range(tq // tk):
        strips = []
        for r in range(n_strips):
            n_sub = min(max((r + 1) * per - d * n_sub_full, 0), n_sub_full)
            if n_sub > 0:
                strips.append((r, n_sub, max(r * per - d * n_sub_full, 0)))
        diag.append((pl.multiple_of(qi * tq + d * tk, tk), strips))

    n_full = qi * (tq // tk)

    def full_group(j, carry):
        group([(pl.multiple_of((j * ATTN_GROUP + g) * tk, tk), full_strips)
               for g in range(ATTN_GROUP)])
        return carry

    lax.fori_loop(0, n_full // ATTN_GROUP, full_group, 0)
    for rem in range(ATTN_GROUP):
        @pl.when(n_full % ATTN_GROUP == rem)
        def _():
            base = n_full - rem
            group([(pl.multiple_of((base + g) * tk, tk), full_strips) for g in range(rem)] + diag)

    heads = []
    for i in range(2):
        rows = slice(i * HEAD_DIM, (i + 1) * HEAD_DIM)
        heads.append(jnp.concatenate(
            [acc_ref[rows, r * tw:(r + 1) * tw] * (1.0 / l_ref[i * n_strips + r])
             for r in range(n_strips)], axis=1))
    o_ref[0] = jnp.concatenate(heads, axis=0).T.astype(o_ref.dtype)


def _attention(q, k, vt, ct, B, S, tq, tk, tw, fox):
    n_pairs = FOX_HEADS // 2
    nq = S // tq
    qw = LANES if fox else 2 * LANES
    in_specs = [pl.BlockSpec((1, tq, qw), lambda b, h, i: (b, i, h)),
                pl.BlockSpec((1, S, 2 * LANES), lambda b, h, i: (b, 0, h)),
                pl.BlockSpec((1, LANES, S), lambda b, h, i: (b, h, 0))]
    args = [q, k, vt]
    if fox:
        in_specs.append(pl.BlockSpec((1, 1, 2, tq), lambda b, h, i: (b, h, 0, i)))
        args.append(ct)
        lane = jnp.arange(LANES)
        ones = jnp.stack([lane < 3, (lane >= 3) & (lane < 6)]).astype(BF16)
        in_specs.append(pl.BlockSpec((2, tq, LANES), lambda b, h, i: (0, 0, 0)))
        args.append(jnp.broadcast_to(ones[:, None, :], (2, tq, LANES)))
    n_strips = tq // tw
    return pl.pallas_call(
        functools.partial(_attn_kernel, fox=fox, tq=tq, tk=tk, tw=tw),
        grid=(B, n_pairs, nq),
        in_specs=in_specs,
        out_specs=pl.BlockSpec((1, tq, LANES), lambda b, h, i: (b, i, h)),
        out_shape=jax.ShapeDtypeStruct((B, S, n_pairs * LANES), BF16),
        scratch_shapes=[pltpu.VMEM((2 * n_strips, 1, tw), F32),
                        pltpu.VMEM((2 * n_strips, 1, tw), F32),
                        pltpu.VMEM((LANES, tq), F32)],
        compiler_params=_params("arbitrary", "arbitrary", "arbitrary"),
        name="fox_attn" if fox else "mla_attn",
    )(*args)


def _ffn_kernel(h_ref, a_ref, b_ref, wa_ref, wb_ref, g_ref, wg_ref, wu_ref, wd_ref,
                o_ref, hres_ref, xn_ref, acc_ref):
    f = pl.program_id(1)

    @pl.when(f == 0)
    def _():
        hres = h_ref[...] + _dot(a_ref[...], wa_ref[...]) + _dot(b_ref[...], wb_ref[...])
        hres_ref[...] = hres
        xn_ref[...] = _rms(hres, g_ref[...]).astype(BF16)
        acc_ref[...] = jnp.zeros_like(acc_ref)

    xn = xn_ref[...]
    gate = _dot(xn, wg_ref[...].astype(BF16))
    up = _dot(xn, wu_ref[...].astype(BF16))
    act = gate * (1.0 / (1.0 + jnp.exp(-gate))) * up
    acc_ref[...] += _dot(act.astype(BF16), wd_ref[...].astype(BF16))

    @pl.when(f == pl.num_programs(1) - 1)
    def _():
        o_ref[...] = hres_ref[...] + acc_ref[...]


def _ffn(h, a, b, wa, wb, g, w_gate, w_up, w_down, tm, tf):
    N = h.shape[0]
    row = lambda i, f: (i, 0)
    full = lambda x: pl.BlockSpec(x.shape, lambda i, f: (0, 0))
    half_w = a.shape[1]
    return pl.pallas_call(
        _ffn_kernel,
        grid=(N // tm, D_FF // tf),
        in_specs=[pl.BlockSpec((tm, D_MODEL), row), pl.BlockSpec((tm, half_w), row),
                  pl.BlockSpec((tm, half_w), row), full(wa), full(wb), full(g),
                  pl.BlockSpec((D_MODEL, tf), lambda i, f: (0, f)),
                  pl.BlockSpec((D_MODEL, tf), lambda i, f: (0, f)),
                  pl.BlockSpec((tf, D_MODEL), lambda i, f: (f, 0))],
        out_specs=pl.BlockSpec((tm, D_MODEL), row),
        out_shape=jax.ShapeDtypeStruct((N, D_MODEL), F32),
        scratch_shapes=[pltpu.VMEM((tm, D_MODEL), F32), pltpu.VMEM((tm, D_MODEL), BF16),
                        pltpu.VMEM((tm, D_MODEL), F32)],
        compiler_params=_params("arbitrary", "arbitrary"),
        name="dense_ffn",
    )(h, a, b, wa, wb, g, w_gate, w_up, w_down)


def _router_kernel(h_ref, a_ref, b_ref, wa_ref, wb_ref, g_ref, wr_ref, tri_ref,
                   hres_ref, xn_ref, slab_ref, meta_ref, cnt_ref):
    @pl.when(pl.program_id(0) == 0)
    def _():
        cnt_ref[...] = jnp.zeros_like(cnt_ref)

    hres = h_ref[...] + _dot(a_ref[...], wa_ref[...]) + _dot(b_ref[...], wb_ref[...])
    hres_ref[...] = hres
    xn = _rms(hres, g_ref[...])
    xn_ref[...] = xn
    xh = xn.astype(BF16)
    xl = (xn - xh.astype(F32)).astype(BF16)
    wr = wr_ref[...]
    wh = wr.astype(BF16)
    wl = (wr - wh.astype(F32)).astype(BF16)
    logits = _dot(xh, wh) + (_dot(xh, wl) + _dot(xl, wh))
    lane = lax.broadcasted_iota(jnp.int32, logits.shape, 1)
    logits = jnp.where(lane < N_EXPERTS, logits, MASK_VALUE)
    m1 = jnp.max(logits, axis=1, keepdims=True)
    i1 = jnp.min(jnp.where(logits == m1, lane, LANES), axis=1, keepdims=True)
    rest = jnp.where(lane == i1, MASK_VALUE, logits)
    m2 = jnp.max(rest, axis=1, keepdims=True)
    i2 = jnp.min(jnp.where(rest == m2, lane, LANES), axis=1, keepdims=True)
    e2 = jnp.exp(m2 - m1)
    w1 = 1.0 / (1.0 + e2)
    w2 = e2 / (1.0 + e2)
    hot1 = lane == i1
    hot2 = lane == i2
    onehot = jnp.where(hot1 | hot2, 1.0, 0.0)
    before = _dot(tri_ref[...], onehot.astype(BF16)) + cnt_ref[...]
    r1 = jnp.sum(jnp.where(hot1, before, 0.0), axis=1, keepdims=True)
    r2 = jnp.sum(jnp.where(hot2, before, 0.0), axis=1, keepdims=True)
    cnt_ref[...] += jnp.sum(onehot, axis=0, keepdims=True)
    slab = (jnp.where(lane == 0, i1.astype(F32), 0.0) + jnp.where(lane == 1, i2.astype(F32), 0.0)
            + jnp.where(lane == 2, r1, 0.0) + jnp.where(lane == 3, r2, 0.0)
            + jnp.where(lane == 4, w1, 0.0) + jnp.where(lane == 5, w2, 0.0))
    slab_ref[...] = slab
    meta_ref[...] = slab.T[:8, :]


def _router(h, a, b, wa, wb, g, w_router, tm):
    N = h.shape[0]
    tri = (jnp.arange(tm)[:, None] > jnp.arange(tm)[None, :]).astype(BF16)
    row = lambda i: (i, 0)
    full = lambda x: pl.BlockSpec(x.shape, lambda i: (0, 0))
    half_w = a.shape[1]
    return pl.pallas_call(
        _router_kernel,
        grid=(N // tm,),
        in_specs=[pl.BlockSpec((tm, D_MODEL), row), pl.BlockSpec((tm, half_w), row),
                  pl.BlockSpec((tm, half_w), row), full(wa), full(wb), full(g), full(w_router),
                  full(tri)],
        out_specs=[pl.BlockSpec((tm, D_MODEL), row), pl.BlockSpec((tm, D_MODEL), row),
                   pl.BlockSpec((tm, LANES), row), pl.BlockSpec((8, tm), lambda i: (0, i)),
                   pl.BlockSpec((1, LANES), lambda i: (0, 0))],
        out_shape=[jax.ShapeDtypeStruct((N, D_MODEL), F32), jax.ShapeDtypeStruct((N, D_MODEL), F32),
                   jax.ShapeDtypeStruct((N, LANES), F32), jax.ShapeDtypeStruct((8, N), F32),
                   jax.ShapeDtypeStruct((1, LANES), F32)],
        compiler_params=_params("arbitrary"),
        name="moe_router",
    )(h, a, b, wa, wb, g, w_router, tri)


def _row_copy(src, src_row, dst, dst_row, sem):
    return pltpu.make_async_copy(src.at[pl.ds(src_row, 1), :], dst.at[pl.ds(dst_row, 1), :], sem)


def _dispatch_kernel(ends_ref, cap_ref, pos_hbm, xn_ref, xs_hbm, pos_smem, zero_ref, idx_sem,
                     row_sem, zero_sem, *, tm, tr):
    i = pl.program_id(0)

    @pl.when(i == 0)
    def _():
        zero_ref[...] = jnp.zeros_like(zero_ref)

        def clear(start):
            copy = pltpu.make_async_copy(zero_ref, xs_hbm.at[pl.ds(start, tr), :], zero_sem)
            copy.start()
            copy.wait()

        for e in range(N_EXPERTS):
            @pl.when(cap_ref[e] > 0)
            def _():
                clear(pl.multiple_of(ends_ref[e] - tr, 8))
        n_rows = xs_hbm.shape[0]
        for t in range(N_EXPERTS):
            start = n_rows - (t + 1) * tr

            @pl.when(start >= ends_ref[N_EXPERTS - 1])
            def _():
                clear(start)

    idx_copy = pltpu.make_async_copy(pos_hbm.at[i], pos_smem, idx_sem)
    idx_copy.start()
    idx_copy.wait()

    def issue(j, carry):
        base = pl.multiple_of(j * 8, 8)
        for u in range(8):
            for k in range(2):
                _row_copy(xn_ref, base + u, xs_hbm, pos_smem[k * tm + base + u], row_sem).start()
        return carry

    lax.fori_loop(0, tm // 8, issue, 0)
    for k in range(2):
        pltpu.make_async_copy(xn_ref, xs_hbm.at[pl.ds(0, tm), :], row_sem).wait()


def _dispatch(ends, cap, pos_tiles, xn, n_rows, tm, tr):
    N = xn.shape[0]
    grid_spec = pltpu.PrefetchScalarGridSpec(
        num_scalar_prefetch=2,
        grid=(N // tm,),
        in_specs=[pl.BlockSpec(memory_space=pl.ANY),
                  pl.BlockSpec((tm, D_MODEL), lambda i, ends, cap: (i, 0))],
        out_specs=pl.BlockSpec(memory_space=pl.ANY),
        scratch_shapes=[pltpu.SMEM((2 * tm,), jnp.int32), pltpu.VMEM((tr, D_MODEL), F32),
                        pltpu.SemaphoreType.DMA, pltpu.SemaphoreType.DMA,
                        pltpu.SemaphoreType.DMA])
    return pl.pallas_call(
        functools.partial(_dispatch_kernel, tm=tm, tr=tr),
        grid_spec=grid_spec,
        out_shape=jax.ShapeDtypeStruct((n_rows, D_MODEL), F32),
        compiler_params=_params("arbitrary"),
        name="moe_dispatch",
    )(ends, cap, pos_tiles, xn)


def _expert_kernel(te_ref, tv_ref, x_ref, wg_ref, wu_ref, wd_ref, y_ref, xb_ref, acc_ref):
    del te_ref
    i = pl.program_id(0)
    f = pl.program_id(1)
    live = tv_ref[i] > 0

    @pl.when(live & (f == 0))
    def _():
        xb_ref[...] = x_ref[...].astype(BF16)
        acc_ref[...] = jnp.zeros_like(acc_ref)

    @pl.when(live)
    def _():
        xb = xb_ref[...]
        gate = _dot(xb, wg_ref[0].astype(BF16))
        up = _dot(xb, wu_ref[0].astype(BF16))
        act = gate * (1.0 / (1.0 + jnp.exp(-gate))) * up
        acc_ref[...] += _dot(act.astype(BF16), wd_ref[0].astype(BF16))

    @pl.when(f == pl.num_programs(1) - 1)
    def _():
        y_ref[...] = jnp.where(live, acc_ref[...], 0.0)


def _experts(tile_expert, tile_valid, xs, w_gate, w_up, w_down, tr, tf):
    n_rows = xs.shape[0]
    nf = D_FF // tf
    fsel = lambda f, i, tv: jnp.where(tv[i] > 0, f, 0)
    grid_spec = pltpu.PrefetchScalarGridSpec(
        num_scalar_prefetch=2,
        grid=(n_rows // tr, nf),
        in_specs=[pl.BlockSpec((tr, D_MODEL), lambda i, f, te, tv: (jnp.where(tv[i] > 0, i, 0), 0)),
                  pl.BlockSpec((1, D_MODEL, tf), lambda i, f, te, tv: (te[i], 0, fsel(f, i, tv))),
                  pl.BlockSpec((1, D_MODEL, tf), lambda i, f, te, tv: (te[i], 0, fsel(f, i, tv))),
                  pl.BlockSpec((1, tf, D_MODEL), lambda i, f, te, tv: (te[i], fsel(f, i, tv), 0))],
        out_specs=pl.BlockSpec((tr, D_MODEL), lambda i, f, te, tv: (i, 0)),
        scratch_shapes=[pltpu.VMEM((tr, D_MODEL), BF16), pltpu.VMEM((tr, D_MODEL), F32)])
    return pl.pallas_call(
        _expert_kernel,
        grid_spec=grid_spec,
        out_shape=jax.ShapeDtypeStruct((n_rows, D_MODEL), F32),
        compiler_params=_params("arbitrary", "arbitrary"),
        name="moe_experts",
    )(tile_expert, tile_valid, xs, w_gate, w_up, w_down)


def _combine_kernel(pos_hbm, hres_ref, slab_ref, fn_ref, ys_hbm, o_ref, pos_smem, ybuf_ref,
                    idx_sem, row_sem, *, tm):
    i = pl.program_id(0)
    slot = i % 2

    def fetch(tile, s):
        idx_copy = pltpu.make_async_copy(pos_hbm.at[tile], pos_smem.at[s], idx_sem)
        idx_copy.start()
        idx_copy.wait()

        def issue(j, carry):
            base = pl.multiple_of(j * 8, 8)
            for u in range(8):
                for k in range(2):
                    _row_copy(ys_hbm, pos_smem[s, k * tm + base + u], ybuf_ref.at[s, k], base + u,
                              row_sem.at[s]).start()
            return carry

        lax.fori_loop(0, tm // 8, issue, 0)

    @pl.when(i == 0)
    def _():
        fetch(0, 0)

    @pl.when(i + 1 < pl.num_programs(0))
    def _():
        fetch(i + 1, 1 - slot)

    for k in range(2):
        pltpu.make_async_copy(ys_hbm.at[pl.ds(0, tm), :], ybuf_ref.at[slot, k],
                              row_sem.at[slot]).wait()

    slab = slab_ref[...]
    lane = lax.broadcasted_iota(jnp.int32, slab.shape, 1)
    w1 = jnp.sum(jnp.where(lane == 4, slab, 0.0), axis=1, keepdims=True)
    w2 = jnp.sum(jnp.where(lane == 5, slab, 0.0), axis=1, keepdims=True)
    out = hres_ref[...] + w1 * ybuf_ref[slot, 0] + w2 * ybuf_ref[slot, 1]
    o_ref[...] = _rms(out, fn_ref[...])


def _combine(pos_tiles, hres, slab, final_g, ys, tm):
    N = hres.shape[0]
    row = lambda i: (i, 0)
    return pl.pallas_call(
        functools.partial(_combine_kernel, tm=tm),
        grid=(N // tm,),
        in_specs=[pl.BlockSpec(memory_space=pl.ANY), pl.BlockSpec((tm, D_MODEL), row),
                  pl.BlockSpec((tm, LANES), row), pl.BlockSpec(final_g.shape, lambda i: (0, 0)),
                  pl.BlockSpec(memory_space=pl.ANY)],
        out_specs=pl.BlockSpec((tm, D_MODEL), row),
        out_shape=jax.ShapeDtypeStruct((N, D_MODEL), F32),
        scratch_shapes=[pltpu.SMEM((2, 2 * tm), jnp.int32), pltpu.VMEM((2, 2, tm, D_MODEL), F32),
                        pltpu.SemaphoreType.DMA, pltpu.SemaphoreType.DMA((2,))],
        compiler_params=_params("arbitrary"),
        name="moe_combine",
    )(pos_tiles, hres, slab, final_g, ys)


def _moe(h, a, b, wa, wb, g, w_router, w_gate, w_up, w_down, final_g):
    N = h.shape[0]
    tm, tr = MOE_TM, MOE_TR
    hres, xn, slab, meta, cnt = _router(h, a, b, wa, wb, g, w_router, tm)
    e1, e2, r1, r2 = (meta[j].astype(jnp.int32) for j in range(4))
    counts = cnt[0, :N_EXPERTS].astype(jnp.int32)
    cap = (counts + tr - 1) // tr * tr
    ends = jnp.cumsum(cap)
    offs = ends - cap
    pos = jnp.stack([offs[e1] + r1, offs[e2] + r2])
    pos_tiles = pos.reshape(2, N // tm, tm).transpose(1, 0, 2).reshape(N // tm, 2 * tm)
    n_rows = 2 * N + N_EXPERTS * tr
    tile_start = jnp.arange(n_rows // tr, dtype=jnp.int32) * tr
    tile_expert = jnp.minimum(jnp.sum(tile_start[:, None] >= ends[None, :], axis=1),
                              N_EXPERTS - 1).astype(jnp.int32)
    tile_valid = jnp.clip(counts[tile_expert] - (tile_start - offs[tile_expert]), 0, tr)
    tile_valid = jnp.where(tile_start < ends[N_EXPERTS - 1], tile_valid, 0).astype(jnp.int32)
    xs = _dispatch(ends.astype(jnp.int32), cap.astype(jnp.int32), pos_tiles, xn, n_rows, tm, tr)
    ys = _experts(tile_expert, tile_valid, xs, w_gate, w_up, w_down, tr, MOE_TF)
    return _combine(pos_tiles, hres, slab, final_g, ys, tm)


def _l1_pre_kernel(x_ref, pos_ref, g_ref, w_ref, cw_ref, invf_ref, mlo_ref, mhi_ref,
                   q_ref, k_ref, vt_ref, oc_ref, tail_ref):
    @pl.when(pl.program_id(1) == 0)
    def _():
        tail_ref[...] = jnp.zeros_like(tail_ref)

    xn = _rms(x_ref[...], g_ref[...]).astype(BF16)
    y = _dot(xn, w_ref[...])
    ang = pos_ref[...].astype(F32) * invf_ref[...]
    cos = jnp.cos(ang)
    sin = jnp.sin(ang)
    sin_lo = sin * mlo_ref[...]
    sin_hi = sin * mhi_ref[...]
    half = HEAD_DIM // 2
    qw = SWA_HEADS * HEAD_DIM
    for j in range(qw // LANES):
        sl = slice(j * LANES, (j + 1) * LANES)
        q_ref[:, sl] = _rope(y[:, sl], cos, sin_lo, sin_hi, half).astype(BF16)
    k_ref[...] = _rope(y[:, qw:qw + LANES], cos, sin_lo, sin_hi, half).astype(BF16)
    vt_ref[0] = y[:, qw + LANES:qw + 2 * LANES].T.astype(BF16)

    o = qw + 2 * LANES
    C = CONV_CHANNELS
    gate_b = y[:, o:o + C]
    u = y[:, o + C:o + 2 * C] * y[:, o + 2 * C:o + 3 * C]
    tm = u.shape[0]
    rows = lax.broadcasted_iota(jnp.int32, u.shape, 0)
    tail = tail_ref[...]
    u1 = jnp.where(rows == 0, tail[7:8, :], pltpu.roll(u, 1, 0))
    u2 = jnp.where(rows == 0, tail[6:7, :], jnp.where(rows == 1, tail[7:8, :], pltpu.roll(u, 2, 0)))
    cw = cw_ref[...]
    oc_ref[...] = (gate_b * (cw[0:1, :] * u2 + cw[1:2, :] * u1 + cw[2:3, :] * u)).astype(BF16)
    tail_ref[...] = u[tm - 8:tm, :]


def _l1_pre(x2, pos2, g, w_in, conv_w, B, S, tm):
    N = B * S
    ns = S // tm
    lane = jnp.arange(LANES)
    half = HEAD_DIM // 2
    freq = ROPE_THETA ** (-jnp.arange(0, HEAD_DIM, 2, dtype=F32) / HEAD_DIM)
    invf = freq[lane % half][None, :].astype(F32)
    m_lo = jnp.where(lane % HEAD_DIM < half, -1.0, 0.0)[None, :].astype(F32)
    m_hi = jnp.where(lane % HEAD_DIM >= half, 1.0, 0.0)[None, :].astype(F32)
    row = lambda b, s: (b * ns + s, 0)
    const = lambda b, s: (0, 0)
    full = lambda a: pl.BlockSpec(a.shape, const)
    return pl.pallas_call(
        _l1_pre_kernel,
        grid=(B, ns),
        in_specs=[pl.BlockSpec((tm, D_MODEL), row), pl.BlockSpec((tm, 1), row),
                  full(g), full(w_in), full(conv_w), full(invf), full(m_lo), full(m_hi)],
        out_specs=[pl.BlockSpec((tm, SWA_HEADS * HEAD_DIM), row),
                   pl.BlockSpec((tm, LANES), row),
                   pl.BlockSpec((1, LANES, tm), lambda b, s: (b, 0, s)),
                   pl.BlockSpec((tm, CONV_CHANNELS), row)],
        out_shape=[jax.ShapeDtypeStruct((N, SWA_HEADS * HEAD_DIM), BF16),
                   jax.ShapeDtypeStruct((N, LANES), BF16),
                   jax.ShapeDtypeStruct((B, LANES, S), BF16),
                   jax.ShapeDtypeStruct((N, CONV_CHANNELS), BF16)],
        scratch_shapes=[pltpu.VMEM((8, CONV_CHANNELS), F32)],
        compiler_params=_params("arbitrary", "arbitrary"),
        name="l1_pre",
    )(x2, pos2, g, w_in, conv_w, invf, m_lo, m_hi)


def _swa_kernel(sink_ref, q_ref, k_ref, kp_ref, vt_ref, vtp_ref, o_ref, *, tq):
    s_idx = pl.program_id(1)
    first = lax.broadcasted_iota(jnp.int32, (1, LANES), 1) < HEAD_DIM
    G = SWA_HEADS // SWA_KV_HEADS
    W = WINDOW

    k_all = jnp.concatenate([kp_ref[0], k_ref[0]], axis=0)
    k_roll = pltpu.roll(k_all, HEAD_DIM, 1)
    k_dup = (jnp.where(first, k_all, k_roll), jnp.where(first, k_roll, k_all))
    vt_all = jnp.concatenate([vtp_ref[0], vt_ref[0]], axis=1)
    q = q_ref[0]
    zero = jnp.zeros((W, LANES), q.dtype)
    cols = G * W
    key_i = lax.broadcasted_iota(jnp.int32, (2 * W, cols), 0)
    col_i = lax.broadcasted_iota(jnp.int32, (2 * W, cols), 1)
    rel = col_i % W + W - key_i
    in_window = (rel >= 0) & (rel < W)
    lane = lax.broadcasted_iota(jnp.int32, (1, cols), 1)
    for r in range(tq // W):
        keep = in_window & (s_idx * tq - W + key_i >= 0) if r == 0 else in_window
        outs = []
        for g in range(SWA_KV_HEADS):
            kb = k_dup[g][r * W:r * W + 2 * W]
            vtb = vt_all[g * HEAD_DIM:(g + 1) * HEAD_DIM, r * W:r * W + 2 * W]
            parts = []
            sink = jnp.zeros((1, cols), F32)
            for j in range(G // 2):
                blk = g * (G // 2) + j
                q2 = q[r * W:(r + 1) * W, blk * LANES:(blk + 1) * LANES]
                parts += [jnp.where(first, q2, zero), jnp.where(first, zero, q2)]
                for half in range(2):
                    c0 = (2 * j + half) * W
                    sink = jnp.where((lane >= c0) & (lane < c0 + W), sink_ref[2 * blk + half], sink)
            qst = jnp.concatenate(parts, axis=0)
            s = _dot_nt(kb, qst)
            s = jnp.where(keep, s, MASK_VALUE)
            m = jnp.maximum(jnp.max(s, axis=0, keepdims=True), sink)
            e = jnp.exp(s - m)
            den = jnp.sum(e, axis=0, keepdims=True) + jnp.exp(sink - m)
            ot = _dot(vtb, (e * (1.0 / den)).astype(BF16))
            outs += [ot[:, j * W:(j + 1) * W] for j in range(G)]
        o_ref[0, r * W:(r + 1) * W, :] = jnp.concatenate(outs, axis=0).T.astype(o_ref.dtype)


def _swa(sinks, q, k, vt, B, S, tq):
    ns = S // tq
    per = tq // WINDOW
    prev = lambda s: jnp.maximum(s * per - 1, 0)
    return pl.pallas_call(
        functools.partial(_swa_kernel, tq=tq),
        grid=(B, ns),
        in_specs=[pl.BlockSpec(memory_space=pltpu.SMEM),
                  pl.BlockSpec((1, tq, SWA_HEADS * HEAD_DIM), lambda b, s: (b, s, 0)),
                  pl.BlockSpec((1, tq, LANES), lambda b, s: (b, s, 0)),
                  pl.BlockSpec((1, WINDOW, LANES), lambda b, s: (b, prev(s), 0)),
                  pl.BlockSpec((1, LANES, tq), lambda b, s: (b, 0, s)),
                  pl.BlockSpec((1, LANES, WINDOW), lambda b, s: (b, 0, prev(s)))],
        out_specs=pl.BlockSpec((1, tq, SWA_HEADS * HEAD_DIM), lambda b, s: (b, s, 0)),
        out_shape=jax.ShapeDtypeStruct((B, S, SWA_HEADS * HEAD_DIM), BF16),
        compiler_params=_params("arbitrary", "arbitrary"),
        name="swa_attn",
    )(sinks, q, k, k, vt, vt)


def _pad_heads(w, n_heads, width, offset=0):
    K = w.shape[0]
    w3 = w.reshape(K, n_heads, width)
    out = jnp.zeros((K, n_heads, LANES), w.dtype).at[:, :, offset:offset + width].set(w3)
    return out.reshape(K, n_heads * LANES)


def kernel(x, positions, l0_norm_mix, l0_w_in, l0_b_forget, l0_q_norm, l0_w_q_up, l0_kv_norm,
           l0_w_kv_up, l0_w_out, l0_norm_ffn, l0_w_gate, l0_w_up, l0_w_down,
           l1_norm_mix, l1_w_in, l1_sinks, l1_conv_w, l1_w_out, l1_norm_ffn, l1_w_router,
           l1_w_gate, l1_w_up, l1_w_down, final_norm):
    B, S, D = x.shape
    assert D == D_MODEL and S % 512 == 0
    N = B * S
    tm = 512
    x2 = x.reshape(N, D)
    pos2 = positions.reshape(N, 1)
    rowv = lambda v: v.reshape(1, -1).astype(F32)

    fox_w = FOX_HEADS * HEAD_DIM
    o = 3 * fox_w
    w_fq = l0_w_in[:, :fox_w] * (HEAD_DIM ** -0.5 * LOG2E)
    w_f = l0_w_in[:, o:o + FOX_HEADS]
    o += FOX_HEADS
    w_cq = l0_w_in[:, o:o + MLA_Q_RANK]
    o += MLA_Q_RANK
    w_ckv = l0_w_in[:, o:o + MLA_KV_RANK]
    o += MLA_KV_RANK
    w_kr = l0_w_in[:, o:o + MLA_ROPE_DIM]
    w_ext = jnp.concatenate(
        [w_fq, l0_w_in[:, fox_w:3 * fox_w], w_cq, w_ckv,
         _pad_heads(w_kr, 1, MLA_ROPE_DIM, MLA_NOPE_DIM), _pad_heads(w_f, 1, FOX_HEADS)],
        axis=1).astype(BF16)
    bf_row = jnp.zeros((1, LANES), F32).at[0, :FOX_HEADS].set(l0_b_forget.astype(F32))
    qk_dim = MLA_NOPE_DIM + MLA_ROPE_DIM
    wq = _pad_heads(l0_w_q_up * (qk_dim ** -0.5 * LOG2E), MLA_HEADS, qk_dim).astype(BF16)
    kv3 = l0_w_kv_up.reshape(MLA_KV_RANK, MLA_HEADS, MLA_NOPE_DIM + MLA_V_DIM)
    wk = _pad_heads(kv3[:, :, :MLA_NOPE_DIM].reshape(MLA_KV_RANK, -1), MLA_HEADS,
                    MLA_NOPE_DIM).astype(BF16)
    wv = kv3[:, :, MLA_NOPE_DIM:].reshape(MLA_KV_RANK, -1).astype(BF16)

    q_f, k_f, vt_f, q_m, k_m, vt_m, ct = _l0_pre(
        x2, pos2, rowv(l0_norm_mix), w_ext, bf_row, rowv(l0_q_norm), wq, rowv(l0_kv_norm),
        wk, wv, B, S, tm)

    tq, tk, tw = ATTN_TQ, ATTN_TK, ATTN_TW
    o_fox = _attention(q_f.reshape(B, S, fox_w), k_f.reshape(B, S, 2 * fox_w), vt_f,
                       ct.reshape(B, FOX_HEADS // 2, 2, S), B, S, tq, tk, tw, True)
    o_mla = _attention(q_m.reshape(B, S, MLA_PAD), k_m.reshape(B, S, MLA_PAD), vt_m, None,
                       B, S, tq, tk, tw, False)

    w_out0 = l0_w_out.astype(BF16)
    h1 = _ffn(x2, o_fox.reshape(N, -1), o_mla.reshape(N, -1), w_out0[:fox_w], w_out0[fox_w:],
              rowv(l0_norm_ffn), l0_w_gate, l0_w_up, l0_w_down, FFN_TM, FFN_TF)

    swa_w = SWA_HEADS * HEAD_DIM
    w_in1 = jnp.concatenate([l1_w_in[:, :swa_w] * (HEAD_DIM ** -0.5), l1_w_in[:, swa_w:]],
                            axis=1).astype(BF16)
    q_s, k_s, vt_s, o_conv = _l1_pre(h1, pos2, rowv(l1_norm_mix), w_in1, l1_conv_w.astype(F32),
                                     B, S, tm)
    o_swa = _swa(l1_sinks.astype(F32), q_s.reshape(B, S, swa_w), k_s.reshape(B, S, LANES), vt_s,
                 B, S, 512)
    w_out1 = l1_w_out.astype(BF16)
    w_router = jnp.zeros((D_MODEL, LANES), F32).at[:, :N_EXPERTS].set(l1_w_router.astype(F32))
    out = _moe(h1, o_swa.reshape(N, -1), o_conv, w_out1[:swa_w], w_out1[swa_w:],
               rowv(l1_norm_ffn), w_router, l1_w_gate, l1_w_up, l1_w_down, rowv(final_norm))
    return out.reshape(B, S, D)
```

```python
import functools

import jax
import jax.numpy as jnp
from jax import lax
from jax.experimental import pallas as pl
from jax.experimental.pallas import tpu as pltpu

F32 = jnp.float32
BF16 = jnp.bfloat16

D_MODEL = 1024
HEAD_DIM = 64
RMS_EPS = 1e-6
ROPE_THETA = 10000.0
FOX_HEADS = 8
MLA_HEADS = 8
MLA_Q_RANK = 384
MLA_KV_RANK = 256
MLA_NOPE_DIM = 64
MLA_ROPE_DIM = 32
MLA_V_DIM = 64
SWA_HEADS = 8
SWA_KV_HEADS = 2
WINDOW = 128
CONV_CHANNELS = 512
CONV_WIDTH = 3
D_FF = 3584
N_EXPERTS = 8

LANES = 128
SUBLANES = 8
SLAB_E1, SLAB_E2, SLAB_R1, SLAB_R2, SLAB_W1, SLAB_W2 = range(6)
MASK_VALUE = -1e30
LOG2E = 1.4426950408889634
VMEM_LIMIT = 56 * 1024 * 1024

QKV_W = 3 * FOX_HEADS * HEAD_DIM
MLA_PAD = MLA_HEADS * LANES
L0_COLS = QKV_W + MLA_Q_RANK + MLA_KV_RANK + 2 * LANES
ATTN_TQ = 1024
ATTN_TK = 512
ATTN_TW = 256
ATTN_GROUP = 4
MOE_TM = 512
MOE_TR = 1024
MOE_TF = 512
FFN_TM = 1024
FFN_TF = 512
L1_COLS = SWA_HEADS * HEAD_DIM + 2 * SWA_KV_HEADS * HEAD_DIM + 3 * CONV_CHANNELS


def _params(*sem):
    return pltpu.CompilerParams(dimension_semantics=sem, vmem_limit_bytes=VMEM_LIMIT)


def _rms(x, g):
    return x * lax.rsqrt(jnp.mean(x * x, axis=-1, keepdims=True) + RMS_EPS) * g


def _dot(a, b):
    return jnp.dot(a, b, preferred_element_type=F32)


def _dot_nt(a, b):
    return lax.dot_general(a, b, (((1,), (1,)), ((), ())), preferred_element_type=F32)


def _rope(x, cos, sin_lo, sin_hi, half):
    return (x * cos + pltpu.roll(x, half, 1) * sin_hi
            + pltpu.roll(x, LANES - half, 1) * sin_lo)


def _l0_pre_kernel(x_ref, pos_ref, g_ref, w_ref, bf_ref, qn_ref, wq_ref, kvn_ref, wk_ref,
                   wv_ref, tri_ref, place_ref, invf_ref, mlo_ref, mhi_ref,
                   qf_ref, kf_ref, vtf_ref, q_ref, k_ref, vt_ref, ct_ref, carry_ref):
    @pl.when(pl.program_id(1) == 0)
    def _():
        carry_ref[...] = jnp.zeros_like(carry_ref)

    fox_w = FOX_HEADS * HEAD_DIM
    xn = _rms(x_ref[...], g_ref[...]).astype(BF16)
    y = _dot(xn, w_ref[...])
    qf_ref[...] = y[:, :fox_w].astype(BF16)
    vtf_ref[0] = y[:, 2 * fox_w:3 * fox_w].T.astype(BF16)
    o = QKV_W
    cq = y[:, o:o + MLA_Q_RANK]
    o += MLA_Q_RANK
    ckv = y[:, o:o + MLA_KV_RANK]
    o += MLA_KV_RANK
    kr = y[:, o:o + LANES]
    fl = y[:, o + LANES:o + 2 * LANES]

    q = _dot(_rms(cq, qn_ref[...]).astype(BF16), wq_ref[...])
    ckn = _rms(ckv, kvn_ref[...]).astype(BF16)
    kk = _dot(ckn, wk_ref[...])
    vt_ref[0] = _dot(ckn, wv_ref[...]).T.astype(BF16)

    ang = pos_ref[...].astype(F32) * invf_ref[...]
    cos = jnp.cos(ang)
    sin = jnp.sin(ang)
    sin_lo = sin * mlo_ref[...]
    sin_hi = sin * mhi_ref[...]
    half = MLA_ROPE_DIM // 2
    kpe = _rope(kr, cos, sin_lo, sin_hi, half)
    for h in range(MLA_HEADS):
        sl = slice(h * LANES, (h + 1) * LANES)
        q_ref[:, sl] = _rope(q[:, sl], cos, sin_lo, sin_hi, half).astype(BF16)
        k_ref[:, sl] = (kk[:, sl] + kpe).astype(BF16)

    z = fl + bf_ref[...]
    ls = jnp.minimum(z, 0.0) - jnp.log1p(jnp.exp(-jnp.abs(z)))
    tri = tri_ref[...]
    c = carry_ref[...]
    for part in _split3(ls):
        c = c + _dot(tri, part)
    tm = c.shape[0]
    carry_ref[...] = c[tm - 1:tm, :]
    c2 = c * LOG2E
    ct_ref[0] = c2.T[:FOX_HEADS, :]
    bias = None
    for i, part in enumerate(_split3(c2)):
        d = _dot(part, place_ref[i])
        bias = d if bias is None else bias + d
    for hp in range(FOX_HEADS // 2):
        kf_ref[:, 2 * hp * LANES:(2 * hp + 1) * LANES] = (
            y[:, fox_w + hp * LANES:fox_w + (hp + 1) * LANES].astype(BF16))
        kf_ref[:, (2 * hp + 1) * LANES:(2 * hp + 2) * LANES] = (
            bias[:, hp * LANES:(hp + 1) * LANES].astype(BF16))


def _split3(x):
    hi = x.astype(BF16)
    r1 = x - hi.astype(F32)
    mid = r1.astype(BF16)
    lo = (r1 - mid.astype(F32)).astype(BF16)
    return hi, mid, lo


def _l0_pre(x2, pos2, g, w_ext, bf_row, qn, wq, kvn, wk, wv, B, S, tm):
    N = B * S
    ns = S // tm
    tri = (jnp.arange(tm)[:, None] >= jnp.arange(tm)[None, :]).astype(BF16)
    n_pairs = FOX_HEADS // 2
    place = jnp.zeros((3, LANES, n_pairs * LANES), F32)
    for hp in range(n_pairs):
        for i in range(3):
            place = place.at[i, 2 * hp, hp * LANES + i].set(-1.0)
            place = place.at[i, 2 * hp + 1, hp * LANES + 3 + i].set(-1.0)
    place = place.astype(BF16)
    lane = jnp.arange(LANES)
    half = MLA_ROPE_DIM // 2
    in_pe = (lane >= MLA_NOPE_DIM) & (lane < MLA_NOPE_DIM + MLA_ROPE_DIM)
    freq = ROPE_THETA ** (-jnp.arange(0, MLA_ROPE_DIM, 2, dtype=F32) / MLA_ROPE_DIM)
    invf = jnp.where(in_pe, freq[(lane - MLA_NOPE_DIM) % half], 0.0)[None, :].astype(F32)
    m_lo = jnp.where(in_pe & (lane < MLA_NOPE_DIM + half), -1.0, 0.0)[None, :].astype(F32)
    m_hi = jnp.where(in_pe & (lane >= MLA_NOPE_DIM + half), 1.0, 0.0)[None, :].astype(F32)

    row = lambda b, s: (b * ns + s, 0)
    tcol = lambda b, s: (b, 0, s)
    full = lambda a: pl.BlockSpec(a.shape, lambda b, s: (0,) * a.ndim)
    fox_w = FOX_HEADS * HEAD_DIM
    vw = MLA_HEADS * MLA_V_DIM
    return pl.pallas_call(
        _l0_pre_kernel,
        grid=(B, ns),
        in_specs=[pl.BlockSpec((tm, D_MODEL), row), pl.BlockSpec((tm, 1), row),
                  full(g), full(w_ext), full(bf_row), full(qn), full(wq), full(kvn), full(wk),
                  full(wv), full(tri), full(place), full(invf), full(m_lo), full(m_hi)],
        out_specs=[pl.BlockSpec((tm, fox_w), row), pl.BlockSpec((tm, 2 * fox_w), row),
                   pl.BlockSpec((1, fox_w, tm), tcol),
                   pl.BlockSpec((tm, MLA_PAD), row), pl.BlockSpec((tm, MLA_PAD), row),
                   pl.BlockSpec((1, vw, tm), tcol),
                   pl.BlockSpec((1, FOX_HEADS, tm), tcol)],
        out_shape=[jax.ShapeDtypeStruct((N, fox_w), BF16),
                   jax.ShapeDtypeStruct((N, 2 * fox_w), BF16),
                   jax.ShapeDtypeStruct((B, fox_w, S), BF16),
                   jax.ShapeDtypeStruct((N, MLA_PAD), BF16),
                   jax.ShapeDtypeStruct((N, MLA_PAD), BF16),
                   jax.ShapeDtypeStruct((B, vw, S), BF16),
                   jax.ShapeDtypeStruct((B, FOX_HEADS, S), F32)],
        scratch_shapes=[pltpu.VMEM((1, LANES), F32)],
        compiler_params=_params("arbitrary", "arbitrary"),
        name="l0_pre",
    )(x2, pos2, g, w_ext, bf_row, qn, wq, kvn, wk, wv, tri, place, invf, m_lo, m_hi)


def _attn_kernel(*refs, fox, tq, tk, tw):
    if fox:
        q_ref, k_ref, vt_ref, cq_ref, ones_ref, o_ref, m_ref, l_ref, acc_ref = refs
    else:
        q_ref, k_ref, vt_ref, o_ref, m_ref, l_ref, acc_ref = refs
    qi = pl.program_id(2)
    n_strips = tq // tw

    q = q_ref[0]
    if fox:
        lane = lax.broadcasted_iota(jnp.int32, (1, LANES), 1)
        zero = jnp.zeros_like(q)
        qa = (jnp.concatenate([jnp.where(lane < HEAD_DIM, q, zero), ones_ref[0]], axis=1),
              jnp.concatenate([jnp.where(lane < HEAD_DIM, zero, q), ones_ref[1]], axis=1))
    else:
        qa = (q[:, :LANES], q[:, LANES:])

    m_ref[...] = jnp.full_like(m_ref, MASK_VALUE)
    l_ref[...] = jnp.zeros_like(l_ref)
    acc_ref[...] = jnp.zeros_like(acc_ref)

    sub = LANES

    def group(chunks):
        work = []
        for start, strips in chunks:
            kc = k_ref[0, pl.ds(start, tk), :]
            for i in range(2):
                for r, n_sub, mask_from in strips:
                    kh = kc[:n_sub * sub] if fox else kc[:n_sub * sub, i * LANES:(i + 1) * LANES]
                    s = _dot_nt(kh, qa[i][r * tw:(r + 1) * tw])
                    work.append((start, i, r, n_sub, mask_from, s))
        for start, i, r, n_sub, mask_from, s in work:
            tiles = []
            for c in range(n_sub):
                t = s[c * sub:(c + 1) * sub]
                if mask_from is not None and c >= mask_from:
                    keys = start + c * sub + lax.broadcasted_iota(jnp.int32, (sub, tw), 0)
                    qpos = qi * tq + r * tw + lax.broadcasted_iota(jnp.int32, (sub, tw), 1)
                    t = jnp.where(keys <= qpos, t, MASK_VALUE)
                tiles.append(t)
            col_max = tiles[0]
            for t in tiles[1:]:
                col_max = jnp.maximum(col_max, t)
            col_max = jnp.max(col_max, axis=0, keepdims=True)
            idx = i * n_strips + r
            m_prev = m_ref[idx]
            if fox:
                cq = cq_ref[0, 0, i:i + 1, r * tw:(r + 1) * tw]
                m_new = jnp.maximum(m_prev, col_max + cq)
                shift = m_new - cq
            else:
                m_new = jnp.maximum(m_prev, col_max)
                shift = m_new
            m_ref[idx] = m_new
            alpha = jnp.exp2(m_prev - m_new)
            psum = None
            probs = []
            for t in tiles:
                p = jnp.exp2(t - shift)
                psum = p if psum is None else psum + p
                probs.append(p.astype(BF16))
            l_ref[idx] = alpha * l_ref[idx] + jnp.sum(psum, axis=0, keepdims=True)
            vt = vt_ref[0, i * HEAD_DIM:(i + 1) * HEAD_DIM, pl.ds(start, n_sub * sub)]
            rows = slice(i * HEAD_DIM, (i + 1) * HEAD_DIM)
            cols = slice(r * tw, (r + 1) * tw)
            acc_ref[rows, cols] = (acc_ref[rows, cols] * alpha
                                   + _dot(vt, jnp.concatenate(probs, axis=0)))

    n_sub_full = tk // sub
    full_strips = [(r, n_sub_full, None) for r in range(n_strips)]
    per = tw // sub
    diag = []
    for d in range(tq // tk):
        strips = []
        for r in range(n_strips):
            n_sub = min(max((r + 1) * per - d * n_sub_full, 0), n_sub_full)
            if n_sub > 0:
                strips.append((r, n_sub, max(r * per - d * n_sub_full, 0)))
        diag.append((pl.multiple_of(qi * tq + d * tk, tk), strips))

    n_full = qi * (tq // tk)

    def full_group(j, carry):
        group([(pl.multiple_of((j * ATTN_GROUP + g) * tk, tk), full_strips)
               for g in range(ATTN_GROUP)])
        return carry

    lax.fori_loop(0, n_full // ATTN_GROUP, full_group, 0)
    for rem in range(ATTN_GROUP):
        @pl.when(n_full % ATTN_GROUP == rem)
        def _():
            base = n_full - rem
            group([(pl.multiple_of((base + g) * tk, tk), full_strips) for g in range(rem)] + diag)

    heads = []
    for i in range(2):
        rows = slice(i * HEAD_DIM, (i + 1) * HEAD_DIM)
        heads.append(jnp.concatenate(
            [acc_ref[rows, r * tw:(r + 1) * tw] * (1.0 / l_ref[i * n_strips + r])
             for r in range(n_strips)], axis=1))
    o_ref[0] = jnp.concatenate(heads, axis=0).T.astype(o_ref.dtype)


def _attention(q, k, vt, ct, B, S, tq, tk, tw, fox):
    n_pairs = FOX_HEADS // 2
    nq = S // tq
    qw = LANES if fox else 2 * LANES
    in_specs = [pl.BlockSpec((1, tq, qw), lambda b, h, i: (b, i, h)),
                pl.BlockSpec((1, S, 2 * LANES), lambda b, h, i: (b, 0, h)),
                pl.BlockSpec((1, LANES, S), lambda b, h, i: (b, h, 0))]
    args = [q, k, vt]
    if fox:
        in_specs.append(pl.BlockSpec((1, 1, 2, tq), lambda b, h, i: (b, h, 0, i)))
        args.append(ct)
        lane = jnp.arange(LANES)
        ones = jnp.stack([lane < 3, (lane >= 3) & (lane < 6)]).astype(BF16)
        in_specs.append(pl.BlockSpec((2, tq, LANES), lambda b, h, i: (0, 0, 0)))
        args.append(jnp.broadcast_to(ones[:, None, :], (2, tq, LANES)))
    n_strips = tq // tw
    return pl.pallas_call(
        functools.partial(_attn_kernel, fox=fox, tq=tq, tk=tk, tw=tw),
        grid=(B, n_pairs, nq),
        in_specs=in_specs,
        out_specs=pl.BlockSpec((1, tq, LANES), lambda b, h, i: (b, i, h)),
        out_shape=jax.ShapeDtypeStruct((B, S, n_pairs * LANES), BF16),
        scratch_shapes=[pltpu.VMEM((2 * n_strips, 1, tw), F32),
                        pltpu.VMEM((2 * n_strips, 1, tw), F32),
                        pltpu.VMEM((LANES, tq), F32)],
        compiler_params=_params("arbitrary", "arbitrary", "arbitrary"),
        name="fox_attn" if fox else "mla_attn",
    )(*args)


def _ffn_kernel(h_ref, a_ref, b_ref, wa_ref, wb_ref, g_ref, wg_ref, wu_ref, wd_ref,
                o_ref, hres_ref, xn_ref, acc_ref):
    f = pl.program_id(1)

    @pl.when(f == 0)
    def _():
        hres = h_ref[...] + _dot(a_ref[...], wa_ref[...]) + _dot(b_ref[...], wb_ref[...])
        hres_ref[...] = hres
        xn_ref[...] = _rms(hres, g_ref[...]).astype(BF16)
        acc_ref[...] = jnp.zeros_like(acc_ref)

    xn = xn_ref[...]
    gate = _dot(xn, wg_ref[...].astype(BF16))
    up = _dot(xn, wu_ref[...].astype(BF16))
    act = gate * (1.0 / (1.0 + jnp.exp(-gate))) * up
    acc_ref[...] += _dot(act.astype(BF16), wd_ref[...].astype(BF16))

    @pl.when(f == pl.num_programs(1) - 1)
    def _():
        o_ref[...] = hres_ref[...] + acc_ref[...]


def _ffn(h, a, b, wa, wb, g, w_gate, w_up, w_down, tm, tf):
    N = h.shape[0]
    row = lambda i, f: (i, 0)
    full = lambda x: pl.BlockSpec(x.shape, lambda i, f: (0, 0))
    half_w = a.shape[1]
    return pl.pallas_call(
        _ffn_kernel,
        grid=(N // tm, D_FF // tf),
        in_specs=[pl.BlockSpec((tm, D_MODEL), row), pl.BlockSpec((tm, half_w), row),
                  pl.BlockSpec((tm, half_w), row), full(wa), full(wb), full(g),
                  pl.BlockSpec((D_MODEL, tf), lambda i, f: (0, f)),
                  pl.BlockSpec((D_MODEL, tf), lambda i, f: (0, f)),
                  pl.BlockSpec((tf, D_MODEL), lambda i, f: (f, 0))],
        out_specs=pl.BlockSpec((tm, D_MODEL), row),
        out_shape=jax.ShapeDtypeStruct((N, D_MODEL), F32),
        scratch_shapes=[pltpu.VMEM((tm, D_MODEL), F32), pltpu.VMEM((tm, D_MODEL), BF16),
                        pltpu.VMEM((tm, D_MODEL), F32)],
        compiler_params=_params("arbitrary", "arbitrary"),
        name="dense_ffn",
    )(h, a, b, wa, wb, g, w_gate, w_up, w_down)


def _router_kernel(h_ref, a_ref, b_ref, wa_ref, wb_ref, g_ref, wr_ref, tri_ref,
                   hres_ref, xn_ref, slab_ref, meta_ref, cnt_ref):
    @pl.when(pl.program_id(0) == 0)
    def _():
        cnt_ref[...] = jnp.zeros_like(cnt_ref)

    hres = h_ref[...] + _dot(a_ref[...], wa_ref[...]) + _dot(b_ref[...], wb_ref[...])
    hres_ref[...] = hres
    xn = _rms(hres, g_ref[...])
    xn_ref[...] = xn
    xh = xn.astype(BF16)
    xl = (xn - xh.astype(F32)).astype(BF16)
    wr = wr_ref[...]
    wh = wr.astype(BF16)
    wl = (wr - wh.astype(F32)).astype(BF16)
    logits = _dot(xh, wh) + (_dot(xh, wl) + _dot(xl, wh))
    lane = lax.broadcasted_iota(jnp.int32, logits.shape, 1)
    logits = jnp.where(lane < N_EXPERTS, logits, MASK_VALUE)
    m1 = jnp.max(logits, axis=1, keepdims=True)
    i1 = jnp.min(jnp.where(logits == m1, lane, LANES), axis=1, keepdims=True)
    rest = jnp.where(lane == i1, MASK_VALUE, logits)
    m2 = jnp.max(rest, axis=1, keepdims=True)
    i2 = jnp.min(jnp.where(rest == m2, lane, LANES), axis=1, keepdims=True)
    e2 = jnp.exp(m2 - m1)
    w1 = 1.0 / (1.0 + e2)
    w2 = e2 / (1.0 + e2)
    hot1 = lane == i1
    hot2 = lane == i2
    onehot = jnp.where(hot1 | hot2, 1.0, 0.0)
    before = _dot(tri_ref[...], onehot.astype(BF16)) + cnt_ref[...]
    r1 = jnp.sum(jnp.where(hot1, before, 0.0), axis=1, keepdims=True)
    r2 = jnp.sum(jnp.where(hot2, before, 0.0), axis=1, keepdims=True)
    cnt_ref[...] += jnp.sum(onehot, axis=0, keepdims=True)
    slab = (jnp.where(lane == SLAB_E1, i1.astype(F32), 0.0)
            + jnp.where(lane == SLAB_E2, i2.astype(F32), 0.0)
            + jnp.where(lane == SLAB_R1, r1, 0.0) + jnp.where(lane == SLAB_R2, r2, 0.0)
            + jnp.where(lane == SLAB_W1, w1, 0.0) + jnp.where(lane == SLAB_W2, w2, 0.0))
    slab_ref[...] = slab
    meta_ref[...] = slab.T[:SUBLANES, :]


def _router(h, a, b, wa, wb, g, w_router, tm):
    N = h.shape[0]
    tri = (jnp.arange(tm)[:, None] > jnp.arange(tm)[None, :]).astype(BF16)
    row = lambda i: (i, 0)
    full = lambda x: pl.BlockSpec(x.shape, lambda i: (0, 0))
    half_w = a.shape[1]
    return pl.pallas_call(
        _router_kernel,
        grid=(N // tm,),
        in_specs=[pl.BlockSpec((tm, D_MODEL), row), pl.BlockSpec((tm, half_w), row),
                  pl.BlockSpec((tm, half_w), row), full(wa), full(wb), full(g), full(w_router),
                  full(tri)],
        out_specs=[pl.BlockSpec((tm, D_MODEL), row), pl.BlockSpec((tm, D_MODEL), row),
                   pl.BlockSpec((tm, LANES), row), pl.BlockSpec((SUBLANES, tm), lambda i: (0, i)),
                   pl.BlockSpec((1, LANES), lambda i: (0, 0))],
        out_shape=[jax.ShapeDtypeStruct((N, D_MODEL), F32), jax.ShapeDtypeStruct((N, D_MODEL), F32),
                   jax.ShapeDtypeStruct((N, LANES), F32), jax.ShapeDtypeStruct((SUBLANES, N), F32),
                   jax.ShapeDtypeStruct((1, LANES), F32)],
        compiler_params=_params("arbitrary"),
        name="moe_router",
    )(h, a, b, wa, wb, g, w_router, tri)


def _row_copy(src, src_row, dst, dst_row, sem):
    return pltpu.make_async_copy(src.at[pl.ds(src_row, 1), :], dst.at[pl.ds(dst_row, 1), :], sem)


def _dispatch_kernel(ends_ref, cap_ref, pos_hbm, xn_ref, xs_hbm, pos_smem, zero_ref, idx_sem,
                     row_sem, zero_sem, *, tm, tr):
    i = pl.program_id(0)

    @pl.when(i == 0)
    def _():
        zero_ref[...] = jnp.zeros_like(zero_ref)

        def clear(start):
            copy = pltpu.make_async_copy(zero_ref, xs_hbm.at[pl.ds(start, tr), :], zero_sem)
            copy.start()
            copy.wait()

        for e in range(N_EXPERTS):
            @pl.when(cap_ref[e] > 0)
            def _():
                clear(pl.multiple_of(ends_ref[e] - tr, 8))
        n_rows = xs_hbm.shape[0]
        for t in range(N_EXPERTS):
            start = n_rows - (t + 1) * tr

            @pl.when(start >= ends_ref[N_EXPERTS - 1])
            def _():
                clear(start)

    idx_copy = pltpu.make_async_copy(pos_hbm.at[i], pos_smem, idx_sem)
    idx_copy.start()
    idx_copy.wait()

    def issue(j, carry):
        base = pl.multiple_of(j * SUBLANES, SUBLANES)
        for u in range(SUBLANES):
            for k in range(2):
                _row_copy(xn_ref, base + u, xs_hbm, pos_smem[k * tm + base + u], row_sem).start()
        return carry

    lax.fori_loop(0, tm // SUBLANES, issue, 0)
    for k in range(2):
        pltpu.make_async_copy(xn_ref, xs_hbm.at[pl.ds(0, tm), :], row_sem).wait()


def _dispatch(ends, cap, pos_tiles, xn, n_rows, tm, tr):
    N = xn.shape[0]
    grid_spec = pltpu.PrefetchScalarGridSpec(
        num_scalar_prefetch=2,
        grid=(N // tm,),
        in_specs=[pl.BlockSpec(memory_space=pl.ANY),
                  pl.BlockSpec((tm, D_MODEL), lambda i, ends, cap: (i, 0))],
        out_specs=pl.BlockSpec(memory_space=pl.ANY),
        scratch_shapes=[pltpu.SMEM((2 * tm,), jnp.int32), pltpu.VMEM((tr, D_MODEL), F32),
                        pltpu.SemaphoreType.DMA, pltpu.SemaphoreType.DMA,
                        pltpu.SemaphoreType.DMA])
    return pl.pallas_call(
        functools.partial(_dispatch_kernel, tm=tm, tr=tr),
        grid_spec=grid_spec,
        out_shape=jax.ShapeDtypeStruct((n_rows, D_MODEL), F32),
        compiler_params=_params("arbitrary"),
        name="moe_dispatch",
    )(ends, cap, pos_tiles, xn)


def _expert_kernel(te_ref, tv_ref, x_ref, wg_ref, wu_ref, wd_ref, y_ref, xb_ref, acc_ref):
    del te_ref
    i = pl.program_id(0)
    f = pl.program_id(1)
    live = tv_ref[i] > 0

    @pl.when(live & (f == 0))
    def _():
        xb_ref[...] = x_ref[...].astype(BF16)
        acc_ref[...] = jnp.zeros_like(acc_ref)

    @pl.when(live)
    def _():
        xb = xb_ref[...]
        gate = _dot(xb, wg_ref[0].astype(BF16))
        up = _dot(xb, wu_ref[0].astype(BF16))
        act = gate * (1.0 / (1.0 + jnp.exp(-gate))) * up
        acc_ref[...] += _dot(act.astype(BF16), wd_ref[0].astype(BF16))

    @pl.when(f == pl.num_programs(1) - 1)
    def _():
        y_ref[...] = jnp.where(live, acc_ref[...], 0.0)


def _experts(tile_expert, tile_valid, xs, w_gate, w_up, w_down, tr, tf):
    n_rows = xs.shape[0]
    nf = D_FF // tf
    fsel = lambda f, i, tv: jnp.where(tv[i] > 0, f, 0)
    grid_spec = pltpu.PrefetchScalarGridSpec(
        num_scalar_prefetch=2,
        grid=(n_rows // tr, nf),
        in_specs=[pl.BlockSpec((tr, D_MODEL), lambda i, f, te, tv: (jnp.where(tv[i] > 0, i, 0), 0)),
                  pl.BlockSpec((1, D_MODEL, tf), lambda i, f, te, tv: (te[i], 0, fsel(f, i, tv))),
                  pl.BlockSpec((1, D_MODEL, tf), lambda i, f, te, tv: (te[i], 0, fsel(f, i, tv))),
                  pl.BlockSpec((1, tf, D_MODEL), lambda i, f, te, tv: (te[i], fsel(f, i, tv), 0))],
        out_specs=pl.BlockSpec((tr, D_MODEL), lambda i, f, te, tv: (i, 0)),
        scratch_shapes=[pltpu.VMEM((tr, D_MODEL), BF16), pltpu.VMEM((tr, D_MODEL), F32)])
    return pl.pallas_call(
        _expert_kernel,
        grid_spec=grid_spec,
        out_shape=jax.ShapeDtypeStruct((n_rows, D_MODEL), F32),
        compiler_params=_params("arbitrary", "arbitrary"),
        name="moe_experts",
    )(tile_expert, tile_valid, xs, w_gate, w_up, w_down)


def _combine_kernel(pos_hbm, hres_ref, slab_ref, fn_ref, ys_hbm, o_ref, pos_smem, ybuf_ref,
                    idx_sem, row_sem, *, tm):
    i = pl.program_id(0)
    idx_copy = pltpu.make_async_copy(pos_hbm.at[i], pos_smem, idx_sem)
    idx_copy.start()
    idx_copy.wait()

    def issue(j, carry):
        base = pl.multiple_of(j * SUBLANES, SUBLANES)
        for u in range(SUBLANES):
            for k in range(2):
                _row_copy(ys_hbm, pos_smem[k * tm + base + u], ybuf_ref.at[k], base + u, row_sem).start()
        return carry

    lax.fori_loop(0, tm // SUBLANES, issue, 0)
    for k in range(2):
        pltpu.make_async_copy(ys_hbm.at[pl.ds(0, tm), :], ybuf_ref.at[k], row_sem).wait()

    slab = slab_ref[...]
    lane = lax.broadcasted_iota(jnp.int32, slab.shape, 1)
    w1 = jnp.sum(jnp.where(lane == SLAB_W1, slab, 0.0), axis=1, keepdims=True)
    w2 = jnp.sum(jnp.where(lane == SLAB_W2, slab, 0.0), axis=1, keepdims=True)
    out = hres_ref[...] + w1 * ybuf_ref[0] + w2 * ybuf_ref[1]
    o_ref[...] = _rms(out, fn_ref[...])


def _combine(pos_tiles, hres, slab, final_g, ys, tm):
    N = hres.shape[0]
    row = lambda i: (i, 0)
    return pl.pallas_call(
        functools.partial(_combine_kernel, tm=tm),
        grid=(N // tm,),
        in_specs=[pl.BlockSpec(memory_space=pl.ANY), pl.BlockSpec((tm, D_MODEL), row),
                  pl.BlockSpec((tm, LANES), row), pl.BlockSpec(final_g.shape, lambda i: (0, 0)),
                  pl.BlockSpec(memory_space=pl.ANY)],
        out_specs=pl.BlockSpec((tm, D_MODEL), row),
        out_shape=jax.ShapeDtypeStruct((N, D_MODEL), F32),
        scratch_shapes=[pltpu.SMEM((2 * tm,), jnp.int32), pltpu.VMEM((2, tm, D_MODEL), F32),
                        pltpu.SemaphoreType.DMA, pltpu.SemaphoreType.DMA],
        compiler_params=_params("arbitrary"),
        name="moe_combine",
    )(pos_tiles, hres, slab, final_g, ys)


def _moe(h, a, b, wa, wb, g, w_router, w_gate, w_up, w_down, final_g):
    N = h.shape[0]
    tm, tr = MOE_TM, MOE_TR
    hres, xn, slab, meta, cnt = _router(h, a, b, wa, wb, g, w_router, tm)
    e1, e2, r1, r2 = (meta[j].astype(jnp.int32) for j in (SLAB_E1, SLAB_E2, SLAB_R1, SLAB_R2))
    counts = cnt[0, :N_EXPERTS].astype(jnp.int32)
    cap = (counts + tr - 1) // tr * tr
    ends = jnp.cumsum(cap)
    offs = ends - cap
    pos = jnp.stack([offs[e1] + r1, offs[e2] + r2])
    pos_tiles = pos.reshape(2, N // tm, tm).transpose(1, 0, 2).reshape(N // tm, 2 * tm)
    n_rows = 2 * N + N_EXPERTS * tr
    tile_start = jnp.arange(n_rows // tr, dtype=jnp.int32) * tr
    tile_expert = jnp.minimum(jnp.sum(tile_start[:, None] >= ends[None, :], axis=1),
                              N_EXPERTS - 1).astype(jnp.int32)
    tile_valid = jnp.clip(counts[tile_expert] - (tile_start - offs[tile_expert]), 0, tr)
    tile_valid = jnp.where(tile_start < ends[N_EXPERTS - 1], tile_valid, 0).astype(jnp.int32)
    xs = _dispatch(ends.astype(jnp.int32), cap.astype(jnp.int32), pos_tiles, xn, n_rows, tm, tr)
    ys = _experts(tile_expert, tile_valid, xs, w_gate, w_up, w_down, tr, MOE_TF)
    return _combine(pos_tiles, hres, slab, final_g, ys, tm)


def _l1_pre_kernel(x_ref, pos_ref, g_ref, w_ref, cw_ref, invf_ref, mlo_ref, mhi_ref,
                   q_ref, k_ref, vt_ref, oc_ref, tail_ref):
    @pl.when(pl.program_id(1) == 0)
    def _():
        tail_ref[...] = jnp.zeros_like(tail_ref)

    xn = _rms(x_ref[...], g_ref[...]).astype(BF16)
    y = _dot(xn, w_ref[...])
    ang = pos_ref[...].astype(F32) * invf_ref[...]
    cos = jnp.cos(ang)
    sin = jnp.sin(ang)
    sin_lo = sin * mlo_ref[...]
    sin_hi = sin * mhi_ref[...]
    half = HEAD_DIM // 2
    qw = SWA_HEADS * HEAD_DIM
    for j in range(qw // LANES):
        sl = slice(j * LANES, (j + 1) * LANES)
        q_ref[:, sl] = _rope(y[:, sl], cos, sin_lo, sin_hi, half).astype(BF16)
    k_ref[...] = _rope(y[:, qw:qw + LANES], cos, sin_lo, sin_hi, half).astype(BF16)
    vt_ref[0] = y[:, qw + LANES:qw + 2 * LANES].T.astype(BF16)

    o = qw + 2 * LANES
    C = CONV_CHANNELS
    gate_b = y[:, o:o + C]
    u = y[:, o + C:o + 2 * C] * y[:, o + 2 * C:o + 3 * C]
    tm = u.shape[0]
    rows = lax.broadcasted_iota(jnp.int32, u.shape, 0)
    tail = tail_ref[...]
    u1 = jnp.where(rows == 0, tail[7:8, :], pltpu.roll(u, 1, 0))
    u2 = jnp.where(rows == 0, tail[6:7, :], jnp.where(rows == 1, tail[7:8, :], pltpu.roll(u, 2, 0)))
    cw = cw_ref[...]
    oc_ref[...] = (gate_b * (cw[0:1, :] * u2 + cw[1:2, :] * u1 + cw[2:3, :] * u)).astype(BF16)
    tail_ref[...] = u[tm - 8:tm, :]


def _l1_pre(x2, pos2, g, w_in, conv_w, B, S, tm):
    N = B * S
    ns = S // tm
    lane = jnp.arange(LANES)
    half = HEAD_DIM // 2
    freq = ROPE_THETA ** (-jnp.arange(0, HEAD_DIM, 2, dtype=F32) / HEAD_DIM)
    invf = freq[lane % half][None, :].astype(F32)
    m_lo = jnp.where(lane % HEAD_DIM < half, -1.0, 0.0)[None, :].astype(F32)
    m_hi = jnp.where(lane % HEAD_DIM >= half, 1.0, 0.0)[None, :].astype(F32)
    row = lambda b, s: (b * ns + s, 0)
    const = lambda b, s: (0, 0)
    full = lambda a: pl.BlockSpec(a.shape, const)
    return pl.pallas_call(
        _l1_pre_kernel,
        grid=(B, ns),
        in_specs=[pl.BlockSpec((tm, D_MODEL), row), pl.BlockSpec((tm, 1), row),
                  full(g), full(w_in), full(conv_w), full(invf), full(m_lo), full(m_hi)],
        out_specs=[pl.BlockSpec((tm, SWA_HEADS * HEAD_DIM), row),
                   pl.BlockSpec((tm, LANES), row),
                   pl.BlockSpec((1, LANES, tm), lambda b, s: (b, 0, s)),
                   pl.BlockSpec((tm, CONV_CHANNELS), row)],
        out_shape=[jax.ShapeDtypeStruct((N, SWA_HEADS * HEAD_DIM), BF16),
                   jax.ShapeDtypeStruct((N, LANES), BF16),
                   jax.ShapeDtypeStruct((B, LANES, S), BF16),
                   jax.ShapeDtypeStruct((N, CONV_CHANNELS), BF16)],
        scratch_shapes=[pltpu.VMEM((8, CONV_CHANNELS), F32)],
        compiler_params=_params("arbitrary", "arbitrary"),
        name="l1_pre",
    )(x2, pos2, g, w_in, conv_w, invf, m_lo, m_hi)


def _swa_kernel(sink_ref, q_ref, k_ref, kp_ref, vt_ref, vtp_ref, o_ref, *, tq):
    s_idx = pl.program_id(1)
    first = lax.broadcasted_iota(jnp.int32, (1, LANES), 1) < HEAD_DIM
    G = SWA_HEADS // SWA_KV_HEADS
    W = WINDOW

    k_all = jnp.concatenate([kp_ref[0], k_ref[0]], axis=0)
    k_roll = pltpu.roll(k_all, HEAD_DIM, 1)
    k_dup = (jnp.where(first, k_all, k_roll), jnp.where(first, k_roll, k_all))
    vt_all = jnp.concatenate([vtp_ref[0], vt_ref[0]], axis=1)
    q = q_ref[0]
    zero = jnp.zeros((W, LANES), q.dtype)
    cols = G * W
    key_i = lax.broadcasted_iota(jnp.int32, (2 * W, cols), 0)
    col_i = lax.broadcasted_iota(jnp.int32, (2 * W, cols), 1)
    rel = col_i % W + W - key_i
    in_window = (rel >= 0) & (rel < W)
    lane = lax.broadcasted_iota(jnp.int32, (1, cols), 1)
    for r in range(tq // W):
        keep = in_window & (s_idx * tq - W + key_i >= 0) if r == 0 else in_window
        outs = []
        for g in range(SWA_KV_HEADS):
            kb = k_dup[g][r * W:r * W + 2 * W]
            vtb = vt_all[g * HEAD_DIM:(g + 1) * HEAD_DIM, r * W:r * W + 2 * W]
            parts = []
            sink = jnp.zeros((1, cols), F32)
            for j in range(G // 2):
                blk = g * (G // 2) + j
                q2 = q[r * W:(r + 1) * W, blk * LANES:(blk + 1) * LANES]
                parts += [jnp.where(first, q2, zero), jnp.where(first, zero, q2)]
                for half in range(2):
                    c0 = (2 * j + half) * W
                    sink = jnp.where((lane >= c0) & (lane < c0 + W), sink_ref[2 * blk + half], sink)
            qst = jnp.concatenate(parts, axis=0)
            s = _dot_nt(kb, qst)
            s = jnp.where(keep, s, MASK_VALUE)
            m = jnp.maximum(jnp.max(s, axis=0, keepdims=True), sink)
            e = jnp.exp(s - m)
            den = jnp.sum(e, axis=0, keepdims=True) + jnp.exp(sink - m)
            ot = _dot(vtb, (e * (1.0 / den)).astype(BF16))
            outs += [ot[:, j * W:(j + 1) * W] for j in range(G)]
        o_ref[0, r * W:(r + 1) * W, :] = jnp.concatenate(outs, axis=0).T.astype(o_ref.dtype)


def _swa(sinks, q, k, vt, B, S, tq):
    ns = S // tq
    per = tq // WINDOW
    prev = lambda s: jnp.maximum(s * per - 1, 0)
    return pl.pallas_call(
        functools.partial(_swa_kernel, tq=tq),
        grid=(B, ns),
        in_specs=[pl.BlockSpec(memory_space=pltpu.SMEM),
                  pl.BlockSpec((1, tq, SWA_HEADS * HEAD_DIM), lambda b, s: (b, s, 0)),
                  pl.BlockSpec((1, tq, LANES), lambda b, s: (b, s, 0)),
                  pl.BlockSpec((1, WINDOW, LANES), lambda b, s: (b, prev(s), 0)),
                  pl.BlockSpec((1, LANES, tq), lambda b, s: (b, 0, s)),
                  pl.BlockSpec((1, LANES, WINDOW), lambda b, s: (b, 0, prev(s)))],
        out_specs=pl.BlockSpec((1, tq, SWA_HEADS * HEAD_DIM), lambda b, s: (b, s, 0)),
        out_shape=jax.ShapeDtypeStruct((B, S, SWA_HEADS * HEAD_DIM), BF16),
        compiler_params=_params("arbitrary", "arbitrary"),
        name="swa_attn",
    )(sinks, q, k, k, vt, vt)


def _pad_heads(w, n_heads, width, offset=0):
    K = w.shape[0]
    w3 = w.reshape(K, n_heads, width)
    out = jnp.zeros((K, n_heads, LANES), w.dtype).at[:, :, offset:offset + width].set(w3)
    return out.reshape(K, n_heads * LANES)


def kernel(x, positions, l0_norm_mix, l0_w_in, l0_b_forget, l0_q_norm, l0_w_q_up, l0_kv_norm,
           l0_w_kv_up, l0_w_out, l0_norm_ffn, l0_w_gate, l0_w_up, l0_w_down,
           l1_norm_mix, l1_w_in, l1_sinks, l1_conv_w, l1_w_out, l1_norm_ffn, l1_w_router,
           l1_w_gate, l1_w_up, l1_w_down, final_norm):
    B, S, D = x.shape
    assert D == D_MODEL and S % 512 == 0
    N = B * S
    tm = 512
    x2 = x.reshape(N, D)
    pos2 = positions.reshape(N, 1)
    rowv = lambda v: v.reshape(1, -1).astype(F32)

    fox_w = FOX_HEADS * HEAD_DIM
    o = 3 * fox_w
    w_fq = l0_w_in[:, :fox_w] * (HEAD_DIM ** -0.5 * LOG2E)
    w_f = l0_w_in[:, o:o + FOX_HEADS]
    o += FOX_HEADS
    w_cq = l0_w_in[:, o:o + MLA_Q_RANK]
    o += MLA_Q_RANK
    w_ckv = l0_w_in[:, o:o + MLA_KV_RANK]
    o += MLA_KV_RANK
    w_kr = l0_w_in[:, o:o + MLA_ROPE_DIM]
    w_ext = jnp.concatenate(
        [w_fq, l0_w_in[:, fox_w:3 * fox_w], w_cq, w_ckv,
         _pad_heads(w_kr, 1, MLA_ROPE_DIM, MLA_NOPE_DIM), _pad_heads(w_f, 1, FOX_HEADS)],
        axis=1).astype(BF16)
    bf_row = jnp.zeros((1, LANES), F32).at[0, :FOX_HEADS].set(l0_b_forget.astype(F32))
    qk_dim = MLA_NOPE_DIM + MLA_ROPE_DIM
    wq = _pad_heads(l0_w_q_up * (qk_dim ** -0.5 * LOG2E), MLA_HEADS, qk_dim).astype(BF16)
    kv3 = l0_w_kv_up.reshape(MLA_KV_RANK, MLA_HEADS, MLA_NOPE_DIM + MLA_V_DIM)
    wk = _pad_heads(kv3[:, :, :MLA_NOPE_DIM].reshape(MLA_KV_RANK, -1), MLA_HEADS,
                    MLA_NOPE_DIM).astype(BF16)
    wv = kv3[:, :, MLA_NOPE_DIM:].reshape(MLA_KV_RANK, -1).astype(BF16)

    q_f, k_f, vt_f, q_m, k_m, vt_m, ct = _l0_pre(
        x2, pos2, rowv(l0_norm_mix), w_ext, bf_row, rowv(l0_q_norm), wq, rowv(l0_kv_norm),
        wk, wv, B, S, tm)

    tq, tk, tw = ATTN_TQ, ATTN_TK, ATTN_TW
    o_fox = _attention(q_f.reshape(B, S, fox_w), k_f.reshape(B, S, 2 * fox_w), vt_f,
                       ct.reshape(B, FOX_HEADS // 2, 2, S), B, S, tq, tk, tw, True)
    o_mla = _attention(q_m.reshape(B, S, MLA_PAD), k_m.reshape(B, S, MLA_PAD), vt_m, None,
                       B, S, tq, tk, tw, False)

    w_out0 = l0_w_out.astype(BF16)
    h1 = _ffn(x2, o_fox.reshape(N, -1), o_mla.reshape(N, -1), w_out0[:fox_w], w_out0[fox_w:],
              rowv(l0_norm_ffn), l0_w_gate, l0_w_up, l0_w_down, FFN_TM, FFN_TF)

    swa_w = SWA_HEADS * HEAD_DIM
    w_in1 = jnp.concatenate([l1_w_in[:, :swa_w] * (HEAD_DIM ** -0.5), l1_w_in[:, swa_w:]],
                            axis=1).astype(BF16)
    q_s, k_s, vt_s, o_conv = _l1_pre(h1, pos2, rowv(l1_norm_mix), w_in1, l1_conv_w.astype(F32),
                                     B, S, tm)
    o_swa = _swa(l1_sinks.astype(F32), q_s.reshape(B, S, swa_w), k_s.reshape(B, S, LANES), vt_s,
                 B, S, 512)
    w_out1 = l1_w_out.astype(BF16)
    w_router = jnp.zeros((D_MODEL, LANES), F32).at[:, :N_EXPERTS].set(l1_w_router.astype(F32))
    out = _moe(h1, o_swa.reshape(N, -1), o_conv, w_out1[:swa_w], w_out1[swa_w:],
               rowv(l1_norm_ffn), w_router, l1_w_gate, l1_w_up, l1_w_down, rowv(final_norm))
    return out.reshape(B, S, D)
```
